```python
import math
import jax, jax.numpy as jnp
from jax import lax
import numpy as np

D_MODEL = 1024
BATCH = 16
SEQ = 2048
DEPTH = 2

GRID_W = 64
CTX_LEN = 256
NORM_EPS = 1e-6
N_BRANCH = 4
MIX_WIDTH = D_MODEL // N_BRANCH
NEG_BIG = -1e30
F_EPS = 1e-30

A_HEAD_DIM = 64
A_HEADS = MIX_WIDTH // A_HEAD_DIM
A_CHUNK = 128
A_FGATE_BIAS_LO = 3.0
A_FGATE_BIAS_HI = 6.0

B_SHORT = 3
B_BANDS = 8
B_FEAT = 1 + 2 * B_BANDS
B_FILTER_HIDDEN = 64
B_DECAY_TARGET = 1e-2
B_FAST_DECAY = 0.3
B_SLOW_DECAY = 1.5

C_CHUNK = 128
C_GROUPS = 4

D_KEY = 64
D_HEADS = MIX_WIDTH // D_KEY
D_VAL = MIX_WIDTH // D_HEADS
D_CHUNK = 64

FFN_HIDDEN = ((-(-8 * D_MODEL // 3)) + 255) // 256 * 256

IN_COLS = 4 * MIX_WIDTH + 4 * A_HEADS + 3 * MIX_WIDTH + 2 * MIX_WIDTH + 5 * MIX_WIDTH

kernel_name = 'hybrid_mlstm_hyena_gmlp_hgrn2_dit'

F32 = jnp.float32


def rms_norm(x, w):
    xf = x.astype(F32)
    y = xf * lax.rsqrt(jnp.mean(xf * xf, axis=-1, keepdims=True) + NORM_EPS)
    return (y * w.astype(F32)).astype(x.dtype)


def layer_norm(x, w, b):
    xf = x.astype(F32)
    mu = jnp.mean(xf, axis=-1, keepdims=True)
    var = jnp.mean(jnp.square(xf - mu), axis=-1, keepdims=True)
    return ((xf - mu) * lax.rsqrt(var + NORM_EPS) * w.astype(F32) + b.astype(F32)).astype(x.dtype)


def head_rms(h, w):
    y = h * lax.rsqrt(jnp.mean(h * h, axis=-1, keepdims=True) + NORM_EPS)
    bsz, nh, n, dh = y.shape
    return y.transpose(0, 2, 1, 3).reshape(bsz, n, nh * dh) * w.astype(F32)


def modulate(h, shift, scale):
    return h * (1 + scale[:, None]) + shift[:, None]


def grid_transpose(x, rows, cols):
    bsz, n, d = x.shape
    return x.reshape(bsz, rows, cols, d).swapaxes(1, 2).reshape(bsz, n, d)


def to_heads(x, nh):
    bsz, n, ch = x.shape
    return x.reshape(bsz, n, nh, ch // nh).transpose(0, 2, 1, 3)


def to_chunks(a, size):
    s = a.shape
    return jnp.moveaxis(a.reshape(s[:2] + (s[2] // size, size) + s[3:]), 2, 0)


def from_chunks(a):
    a = jnp.moveaxis(a, 0, 2)
    s = a.shape
    return a.reshape(s[:2] + (s[2] * s[3],) + s[4:])


def split_in(p):
    sizes = (4 * MIX_WIDTH, 4 * A_HEADS, 3 * MIX_WIDTH, 2 * MIX_WIDTH)
    idx, s = [], 0
    for sz in sizes:
        s += sz
        idx.append(s)
    return jnp.split(p, idx, axis=-1)


def mlstm_chunked(q, k, v, i_pre, f_pre, state):
    n_tok, dh = q.shape[2], q.shape[3]
    size = min(A_CHUNK, n_tok)
    q = q.astype(F32)
    k = k.astype(F32) * dh ** -0.5
    v = v.astype(F32)
    log_i = i_pre.astype(F32)
    log_f = jax.nn.log_sigmoid(f_pre.astype(F32))
    tri = jnp.tril(jnp.ones((size, size), bool))

    def step(carry, blk):
        C, nrm, m = carry
        qb, kb, vb, ib, fb = blk
        b = jnp.cumsum(fb, axis=-1)
        log_d_raw = b[..., :, None] - b[..., None, :] + ib[..., None, :]
        log_inter = b + m[..., None]
        m_t = jnp.maximum(log_inter, jnp.max(jnp.where(tri, log_d_raw, NEG_BIG), axis=-1))
        d_mat = jnp.where(tri, jnp.exp(jnp.where(tri, log_d_raw - m_t[..., None], 0.0)), 0.0)
        w_intra = d_mat * jnp.einsum('bhtd,bhsd->bhts', qb, kb)
        w_inter = jnp.exp(log_inter - m_t)
        num = w_inter[..., None] * jnp.einsum('bhtk,bhkv->bhtv', qb, C) + jnp.einsum('bhts,bhsv->bhtv', w_intra, vb)
        den = w_inter * jnp.einsum('bhtk,bhk->bht', qb, nrm) + jnp.sum(w_intra, axis=-1)
        h = num / jnp.maximum(jnp.abs(den), jnp.exp(-m_t))[..., None]
        b_end = b[..., -1]
        log_w = b_end[..., None] - b + ib
        m_new = jnp.maximum(b_end + m, jnp.max(log_w, axis=-1))
        w_st = jnp.exp(log_w - m_new[..., None])
        decay = jnp.exp(b_end + m - m_new)
        C = decay[..., None, None] * C + jnp.einsum('bhs,bhsk,bhsv->bhkv', w_st, kb, vb)
        nrm = decay[..., None] * nrm + jnp.einsum('bhs,bhsk->bhk', w_st, kb)
        return (C, nrm, m_new), h

    blocks = tuple(to_chunks(a, size) for a in (q, k, v, log_i, log_f))
    final, hs = lax.scan(step, state, blocks)
    return from_chunks(hs), final


def hgrn_chunked(q, k, v, log_f, S):
    n_tok = q.shape[2]
    size = min(D_CHUNK, n_tok)
    tri = jnp.tril(jnp.ones((size, size), bool))[:, :, None]

    def step(S, blk):
        qb, kb, vb, fb = blk
        b = jnp.cumsum(fb, axis=2)
        o_inter = jnp.einsum('bhtk,bhkv->bhtv', qb * jnp.exp(b), S)
        log_a = jnp.where(tri, b[:, :, :, None, :] - b[:, :, None, :, :], 0.0)
        a_mat = jnp.where(tri, jnp.exp(log_a), 0.0)
        scores = jnp.einsum('bhtk,bhsk,bhtsk->bhts', qb, kb, a_mat)
        o = o_inter + jnp.einsum('bhts,bhsv->bhtv', scores, vb)
        b_end = b[:, :, -1]
        S = jnp.exp(b_end)[..., None] * S + jnp.einsum('bhsk,bhsv->bhkv', kb * jnp.exp(b_end[:, :, None] - b), vb)
        return S, o

    blocks = tuple(to_chunks(a.astype(F32), size) for a in (q, k, v, log_f))
    final, os_ = lax.scan(step, S, blocks)
    return from_chunks(os_), final


def bidirectional(scan_fn, fwd_in, bwd_in, init_f, init_b):
    y_f, s_f = scan_fn(*fwd_in, init_f)
    y_b, s_b = scan_fn(*[jnp.flip(a, axis=2) for a in bwd_in], init_b)
    return y_f + jnp.flip(y_b, axis=2), s_f, s_b


def mlstm_inputs(pa, pg):
    q, k, v, o = jnp.split(pa, 4, axis=-1)
    q, k, v = to_heads(q, A_HEADS), to_heads(k, A_HEADS), to_heads(v, A_HEADS)
    bsz, n, _ = pg.shape
    i_f, f_f, i_b, f_b = pg.reshape(bsz, n, 4, A_HEADS).transpose(2, 0, 3, 1)
    return (q, k, v, i_f, f_f), (q, k, v, i_b, f_b), o


def hgrn_inputs(pd, lb):
    q, i, f_f, f_b, g = jnp.split(pd, 5, axis=-1)
    q = to_heads(jax.nn.silu(q), D_HEADS)
    v = to_heads(i, D_HEADS)
    lbh = lb.reshape(D_HEADS, D_KEY)[None, :, None, :]

    def gate(fp):
        fp = to_heads(fp, D_HEADS).astype(F32)
        f = lbh + (1 - lbh) * jax.nn.sigmoid(fp)
        log_f = jnp.log(jnp.maximum(f, F_EPS))
        k = (1 - lbh) * jax.nn.sigmoid(-fp)
        return k, log_f

    k_f, lf_f = gate(f_f)
    k_b, lf_b = gate(f_b)
    return (q, k_f, v, lf_f), (q, k_b, v, lf_b), g


def short_conv(x, w, b):
    ch = x.shape[-1]
    y = lax.conv_general_dilated(x, w.astype(x.dtype)[:, None, :], window_strides=(1,), padding='SAME',
                                 dimension_numbers=('NWC', 'WIO', 'NWC'), feature_group_count=ch)
    return y + b


def hyena_kernel(L, lp):
    t = jnp.linspace(0.0, 1.0, L, dtype=F32)[:, None]
    bands = jnp.linspace(1e-4, B_BANDS - 1, B_BANDS, dtype=F32)[None]
    ang = 2 * math.pi * bands * jnp.arange(L, dtype=F32)[:, None] / L
    z = jnp.concatenate([t, jnp.cos(ang), -jnp.sin(ang)], axis=-1)
    hd = jnp.sin(lp['hy_freq1'].astype(F32) * (z @ lp['hy_w1'].astype(F32) + lp['hy_b1'].astype(F32)))
    hd = jnp.sin(lp['hy_freq2'].astype(F32) * (hd @ lp['hy_w2'].astype(F32) + lp['hy_b2'].astype(F32)))
    hk = hd @ lp['hy_w3'].astype(F32)
    deltas = jnp.abs(jnp.linspace(math.log(B_DECAY_TARGET) / B_SLOW_DECAY,
                                  math.log(B_DECAY_TARGET) / B_FAST_DECAY, MIX_WIDTH, dtype=F32))
    decay = jnp.exp(-t * deltas)
    h_fwd = hk[:, :MIX_WIDTH] * decay
    h_bwd = hk[:, MIX_WIDTH:] * decay
    return jnp.concatenate([h_fwd, jnp.zeros((1, MIX_WIDTH), F32), jnp.flip(h_bwd[1:], axis=0)], axis=0)


def long_conv(u, kernel, d_skip):
    n = u.shape[1]
    uf = jnp.fft.rfft(u.astype(F32), n=2 * n, axis=1)
    kf = jnp.fft.rfft(kernel, axis=0)
    y = jnp.fft.irfft(uf * kf[None], n=2 * n, axis=1)[:, :n]
    return (y + u.astype(F32) * d_skip.astype(F32)).astype(u.dtype)


def hyena(pb, lp):
    n = pb.shape[1]
    x0, x1, v = jnp.split(short_conv(pb, lp['hy_short_w'], lp['hy_short_b']), 3, axis=-1)
    return x0 * long_conv(x1 * v, hyena_kernel(n, lp), lp['hy_bias'])


def gmlp(pc, lp):
    u, v = jnp.split(jax.nn.gelu(pc), 2, axis=-1)
    v = layer_norm(v, lp['gm_norm_w'], lp['gm_norm_b'])
    bsz, n, _ = v.shape
    vb = v.reshape(bsz, n // C_CHUNK, C_CHUNK, C_GROUPS, MIX_WIDTH // C_GROUPS)
    mixed = jnp.einsum('gts,bnsgc->bntgc', lp['gm_ws'], vb) + lp['gm_bs'].T[None, None, :, :, None]
    return u * mixed.reshape(bsz, n, MIX_WIDTH)


def merge_branches(h, ys, lp):
    acc = jax.nn.sigmoid(h @ lp['w_gate'][0]) * (ys[0] @ lp['w_branch'][0])
    for j in range(1, N_BRANCH):
        acc = acc + jax.nn.sigmoid(h @ lp['w_gate'][j]) * (ys[j] @ lp['w_branch'][j])
    return acc @ lp['w_out']


def token_mixing(h, hc, need_ctx, lp):
    pa, pg, pb, pc, pd = split_in(h @ lp['w_in'] + lp['b_in'])
    ca, cg, cb, cc, cd = split_in(hc @ lp['w_in'] + lp['b_in'])
    bsz = h.shape[0]
    lat_f, lat_b, lat_o = mlstm_inputs(pa, pg)
    ctx_f, ctx_b, ctx_o = mlstm_inputs(ca, cg)
    zero_a = (jnp.zeros((bsz, A_HEADS, A_HEAD_DIM, A_HEAD_DIM), F32),
              jnp.zeros((bsz, A_HEADS, A_HEAD_DIM), F32), jnp.zeros((bsz, A_HEADS), F32))
    ctx_ha, sa_f, sa_b = bidirectional(mlstm_chunked, ctx_f, ctx_b, zero_a, zero_a)
    lat_ha, _, _ = bidirectional(mlstm_chunked, lat_f, lat_b, sa_f, sa_b)
    lat_f, lat_b, lat_g = hgrn_inputs(pd, lp['lb'])
    ctx_f, ctx_b, ctx_g = hgrn_inputs(cd, lp['lb'])
    zero_d = jnp.zeros((bsz, D_HEADS, D_KEY, D_VAL), F32)
    ctx_hd, sd_f, sd_b = bidirectional(hgrn_chunked, ctx_f, ctx_b, zero_d, zero_d)
    lat_hd, _, _ = bidirectional(hgrn_chunked, lat_f, lat_b, sd_f, sd_b)

    def branches(ha, o, hd, g, pb_, pc_):
        y_a = (head_rms(ha, lp['mlstm_norm_w']) * jax.nn.sigmoid(o.astype(F32))).astype(h.dtype)
        y_b = hyena(pb_, lp)
        y_c = gmlp(pc_, lp)
        y_d = (head_rms(hd, lp['hg_norm_w']) * jax.nn.silu(g.astype(F32))).astype(h.dtype)
        return (y_a, y_b, y_c, y_d)

    y = merge_branches(h, branches(lat_ha, lat_o, lat_hd, lat_g, pb, pc), lp)
    yc = merge_branches(hc, branches(ctx_ha, ctx_o, ctx_hd, ctx_g, cb, cc), lp) if need_ctx else None
    return y, yc


def swiglu(h, w_up, w_down):
    a, g = jnp.split(h @ w_up, 2, axis=-1)
    return (jax.nn.silu(a) * g) @ w_down


def setup_inputs(seed: int = 0) -> dict:
    key = jax.random.key(seed)
    keys = jax.random.split(key, 48)
    counter = [0]

    def nrm(shape, scale):
        k = keys[counter[0]]
        counter[0] += 1
        return jax.random.normal(k, shape, jnp.float32) * scale

    D = D_MODEL
    x = nrm((BATCH, SEQ, D), 1.0)
    c = nrm((BATCH, D), 1.0)
    ctx = nrm((BATCH, CTX_LEN, D), 1.0)
    c_ctx = nrm((D,), 1.0)
    ada_w = nrm((DEPTH, D, 6 * D), D ** -0.5)
    ada_b = nrm((DEPTH, 6 * D), 0.02)
    norm1_w = 1.0 + nrm((DEPTH, D), 0.1)
    norm2_w = 1.0 + nrm((DEPTH, D), 0.1)
    w_in = nrm((DEPTH, D, IN_COLS), D ** -0.5)
    b_in = nrm((DEPTH, IN_COLS), 0.02)
    off = 4 * MIX_WIDTH
    f_bias = jnp.linspace(A_FGATE_BIAS_LO, A_FGATE_BIAS_HI, A_HEADS, dtype=jnp.float32)
    b_in = b_in.at[:, off + A_HEADS: off + 2 * A_HEADS].add(f_bias)
    b_in = b_in.at[:, off + 3 * A_HEADS: off + 4 * A_HEADS].add(f_bias)
    mlstm_norm_w = 1.0 + nrm((DEPTH, MIX_WIDTH), 0.1)
    hy_short_w = nrm((DEPTH, B_SHORT, 3 * MIX_WIDTH), B_SHORT ** -0.5)
    hy_short_b = nrm((DEPTH, 3 * MIX_WIDTH), 0.02)
    hy_w1 = nrm((DEPTH, B_FEAT, B_FILTER_HIDDEN), B_FEAT ** -0.5)
    hy_b1 = nrm((DEPTH, B_FILTER_HIDDEN), 0.1)
    hy_freq1 = 1.0 + nrm((DEPTH, B_FILTER_HIDDEN), 0.1)
    hy_w2 = nrm((DEPTH, B_FILTER_HIDDEN, B_FILTER_HIDDEN), B_FILTER_HIDDEN ** -0.5)
    hy_b2 = nrm((DEPTH, B_FILTER_HIDDEN), 0.1)
    hy_freq2 = 1.0 + nrm((DEPTH, B_FILTER_HIDDEN), 0.1)
    hy_w3 = nrm((DEPTH, B_FILTER_HIDDEN, 2 * MIX_WIDTH), 0.05 * B_FILTER_HIDDEN ** -0.5)
    hy_bias = nrm((DEPTH, MIX_WIDTH), 0.5)
    gm_norm_w = 1.0 + nrm((DEPTH, MIX_WIDTH), 0.1)
    gm_norm_b = nrm((DEPTH, MIX_WIDTH), 0.02)
    gm_ws = nrm((DEPTH, C_GROUPS, C_CHUNK, C_CHUNK), C_CHUNK ** -0.5)
    gm_bs = 1.0 + nrm((DEPTH, C_GROUPS, C_CHUNK), 0.1)
    hg_lb_logits = nrm((DEPTH, MIX_WIDTH), 0.5)
    hg_norm_w = 1.0 + nrm((DEPTH, MIX_WIDTH), 0.1)
    w_gate = nrm((DEPTH, N_BRANCH, D, D), D ** -0.5)
    w_branch = nrm((DEPTH, N_BRANCH, MIX_WIDTH, D), MIX_WIDTH ** -0.5)
    w_out = nrm((DEPTH, D, D), D ** -0.5)
    w_ffn_in = nrm((DEPTH, D, 2 * FFN_HIDDEN), D ** -0.5)
    w_ffn_out = nrm((DEPTH, FFN_HIDDEN, D), FFN_HIDDEN ** -0.5)
    final_norm_w = 1.0 + nrm((D,), 0.1)
    return {'x': x, 'c': c, 'ctx': ctx, 'c_ctx': c_ctx, 'ada_w': ada_w, 'ada_b': ada_b,
            'norm1_w': norm1_w, 'norm2_w': norm2_w, 'w_in': w_in, 'b_in': b_in, 'mlstm_norm_w': mlstm_norm_w,
            'hy_short_w': hy_short_w, 'hy_short_b': hy_short_b, 'hy_w1': hy_w1, 'hy_b1': hy_b1,
            'hy_freq1': hy_freq1, 'hy_w2': hy_w2, 'hy_b2': hy_b2, 'hy_freq2': hy_freq2, 'hy_w3': hy_w3,
            'hy_bias': hy_bias, 'gm_norm_w': gm_norm_w, 'gm_norm_b': gm_norm_b, 'gm_ws': gm_ws, 'gm_bs': gm_bs,
            'hg_lb_logits': hg_lb_logits, 'hg_norm_w': hg_norm_w, 'w_gate': w_gate, 'w_branch': w_branch,
            'w_out': w_out, 'w_ffn_in': w_ffn_in, 'w_ffn_out': w_ffn_out, 'final_norm_w': final_norm_w}


def reference(x, c, ctx, c_ctx, ada_w, ada_b, norm1_w, norm2_w, w_in, b_in, mlstm_norm_w,
              hy_short_w, hy_short_b, hy_w1, hy_b1, hy_freq1, hy_w2, hy_b2, hy_freq2, hy_w3,
              hy_bias, gm_norm_w, gm_norm_b, gm_ws, gm_bs, hg_lb_logits, hg_norm_w, w_gate, w_branch,
              w_out, w_ffn_in, w_ffn_out, final_norm_w):
    rows = x.shape[1] // GRID_W
    lb_prob = jax.nn.softmax(hg_lb_logits.astype(F32), axis=0)
    lb_all = jnp.cumsum(lb_prob, axis=0) - lb_prob[0]
    sc = jax.nn.silu(c)
    scc = jax.nn.silu(c_ctx)[None]
    xc = ctx
    for l in range(DEPTH):
        need_ctx = l < DEPTH - 1
        col_major = l % 2 == 1
        if col_major:
            x = grid_transpose(x, rows, GRID_W)
        lp = {'w_in': w_in[l], 'b_in': b_in[l], 'mlstm_norm_w': mlstm_norm_w[l],
              'hy_short_w': hy_short_w[l], 'hy_short_b': hy_short_b[l], 'hy_w1': hy_w1[l], 'hy_b1': hy_b1[l],
              'hy_freq1': hy_freq1[l], 'hy_w2': hy_w2[l], 'hy_b2': hy_b2[l], 'hy_freq2': hy_freq2[l],
              'hy_w3': hy_w3[l], 'hy_bias': hy_bias[l], 'gm_norm_w': gm_norm_w[l], 'gm_norm_b': gm_norm_b[l],
              'gm_ws': gm_ws[l], 'gm_bs': gm_bs[l], 'lb': lb_all[l], 'hg_norm_w': hg_norm_w[l],
              'w_gate': w_gate[l], 'w_branch': w_branch[l], 'w_out': w_out[l]}
        mod = jnp.split(sc @ ada_w[l] + ada_b[l], 6, axis=-1)
        mod_c = jnp.split(scc @ ada_w[l] + ada_b[l], 6, axis=-1)
        h = modulate(rms_norm(x, norm1_w[l]), mod[0], mod[1])
        hc = modulate(rms_norm(xc, norm1_w[l]), mod_c[0], mod_c[1])
        y, yc = token_mixing(h, hc, need_ctx, lp)
        x = x + mod[2][:, None] * y
        x = x + mod[5][:, None] * swiglu(modulate(rms_norm(x, norm2_w[l]), mod[3], mod[4]), w_ffn_in[l], w_ffn_out[l])
        if need_ctx:
            xc = xc + mod_c[2][:, None] * yc
            xc = xc + mod_c[5][:, None] * swiglu(modulate(rms_norm(xc, norm2_w[l]), mod_c[3], mod_c[4]),
                                                 w_ffn_in[l], w_ffn_out[l])
        if col_major:
            x = grid_transpose(x, GRID_W, rows)
    return rms_norm(x, final_norm_w)
```

```python
import functools
import math

import numpy as np
import jax
import jax.numpy as jnp
from jax import lax
from jax.experimental import pallas as pl
from jax.experimental.pallas import tpu as pltpu

F32 = jnp.float32
BF16 = jnp.bfloat16

GRID_W = 64
NORM_EPS = 1e-6
N_BRANCH = 4
NEG_BIG = -1e30
F_EPS = 1e-30
HEAD_DIM = 64
HEADS = 4
LANE = 128
A_CHUNK = 128
D_CHUNK = 128
C_CHUNK = 128
C_GROUPS = 4
B_SHORT = 3
B_BANDS = 8
B_DECAY_TARGET = 1e-2
B_FAST_DECAY = 0.3
B_SLOW_DECAY = 1.5
CONV_BLOCK = 256
VMEM_LIMIT = 56 * 1024 * 1024


def _cparams(n_axes):
    return pltpu.CompilerParams(dimension_semantics=("arbitrary",) * n_axes,
                                vmem_limit_bytes=VMEM_LIMIT)


def _dot(a, b):
    return jnp.dot(a.astype(BF16), b.astype(BF16), preferred_element_type=F32)


def _dot_nt(a, b):
    return lax.dot_general(a.astype(BF16), b.astype(BF16), (((1,), (1,)), ((), ())),
                           preferred_element_type=F32)


def _dot_tn(a, b):
    return lax.dot_general(a.astype(BF16), b.astype(BF16), (((0,), (0,)), ((), ())),
                           preferred_element_type=F32)


def _split3(x):
    hi = x.astype(BF16)
    r1 = x - hi.astype(F32)
    mid = r1.astype(BF16)
    lo = (r1 - mid.astype(F32)).astype(BF16)
    return hi, mid, lo


def _dot_exact_lhs01(sel_bf16, x):
    hi, mid, lo = _split3(x)
    return (jnp.dot(sel_bf16, hi, preferred_element_type=F32)
            + jnp.dot(sel_bf16, mid, preferred_element_type=F32)
            + jnp.dot(sel_bf16, lo, preferred_element_type=F32))


def _dot_f32(a, b):
    a_hi, a_mid, a_lo = _split3(a)
    b_hi, b_mid, b_lo = _split3(b)
    d = lambda u, v: jnp.dot(u, v, preferred_element_type=F32)
    return (d(a_hi, b_hi) + (d(a_hi, b_mid) + d(a_mid, b_hi))
            + (d(a_hi, b_lo) + d(a_mid, b_mid) + d(a_lo, b_hi)))


def _sigmoid(x):
    return 1.0 / (1.0 + jnp.exp(-x))


def _silu(x):
    return x * _sigmoid(x)


def _log_sigmoid(x):
    return jnp.minimum(x, 0.0) - jnp.log(1.0 + jnp.exp(-jnp.abs(x)))


def _norm_mod(x, nw, shift, scale):
    ms = jnp.mean(x * x, axis=-1, keepdims=True)
    y = x * lax.rsqrt(ms + NORM_EPS) * nw
    return y * (1.0 + scale) + shift


def _ada_kernel(c_ref, w_ref, b_ref, o_ref):
    o_ref[0] = _dot_f32(_silu(c_ref[...]), w_ref[0]) + b_ref[0]


def _ada_call(cvec, ada_w, ada_b):
    depth, d, n6 = ada_w.shape
    mb = cvec.shape[0]
    tn = 512
    return pl.pallas_call(
        _ada_kernel,
        grid=(depth, n6 // tn),
        in_specs=[pl.BlockSpec((mb, d), lambda l, j: (0, 0)),
                  pl.BlockSpec((1, d, tn), lambda l, j: (l, 0, j)),
                  pl.BlockSpec((1, 1, tn), lambda l, j: (l, 0, j))],
        out_specs=pl.BlockSpec((1, mb, tn), lambda l, j: (l, 0, j)),
        out_shape=jax.ShapeDtypeStruct((depth, mb, n6), F32),
        compiler_params=_cparams(2),
        name="ada_mod",
    )(cvec, ada_w, ada_b.reshape(depth, 1, n6))


def _in_kernel(*refs, d, n_seg):
    x_ref, mod_ref, nw_ref = refs[:3]
    w_refs = refs[3:3 + n_seg]
    b_refs = refs[3 + n_seg:3 + 2 * n_seg]
    o_refs = refs[3 + 2 * n_seg:]
    h = _norm_mod(x_ref[...], nw_ref[...], mod_ref[0, :, 0:d], mod_ref[0, :, d:2 * d]).astype(BF16)
    for w_ref, b_ref, o_ref in zip(w_refs, b_refs, o_refs):
        o_ref[...] = jnp.dot(h, w_ref[...], preferred_element_type=F32) + b_ref[...]


def _in_call(x2d, mod, nw, ws, bs, rows_per_mod, tm):
    r, d = x2d.shape
    n_seg = len(ws)
    blocks_per_mod = rows_per_mod // tm
    const = lambda i: (0, 0)
    in_specs = [pl.BlockSpec((tm, d), lambda i: (i, 0)),
                pl.BlockSpec((1, 1, mod.shape[-1]), lambda i: (i // blocks_per_mod, 0, 0)),
                pl.BlockSpec((1, d), const)]
    in_specs += [pl.BlockSpec(w.shape, const) for w in ws]
    in_specs += [pl.BlockSpec(b.shape, const) for b in bs]
    return pl.pallas_call(
        functools.partial(_in_kernel, d=d, n_seg=n_seg),
        grid=(r // tm,),
        in_specs=in_specs,
        out_specs=[pl.BlockSpec((tm, w.shape[1]), lambda i: (i, 0)) for w in ws],
        out_shape=[jax.ShapeDtypeStruct((r, w.shape[1]), F32) for w in ws],
        compiler_params=_cparams(1),
        name="in_proj",
    )(x2d, mod, nw, *ws, *bs)


def _mlstm_chunk(blk, tri_bf16, mask, c_ref, m_ref, slot0, reverse, lane_lo, lane0, lane64):
    L = blk.shape[0]
    gates = blk[:, 4 * 256:4 * 256 + LANE]
    cum = _dot_exact_lhs01(tri_bf16, _log_sigmoid(gates))
    gates_t = gates.T
    cum_t = cum.T
    end = 0 if reverse else L - 1
    gi = 8 if reverse else 0
    outs = []
    for pair in range(HEADS // 2):
        q_pair = blk[:, 128 * pair:128 * pair + 128]
        k_pair = blk[:, 256 + 128 * pair:256 + 128 * pair + 128] * (HEAD_DIM ** -0.5)
        v_pair = blk[:, 512 + 128 * pair:512 + 128 * pair + 128]
        halves = []
        for sub in range(2):
            hd = 2 * pair + sub
            own = lane_lo if sub == 0 else jnp.logical_not(lane_lo)
            k_h = jnp.where(own, k_pair, 0.0)
            v_aug = jnp.where(own, v_pair, jnp.where(lane64 if sub == 0 else lane0, 1.0, 0.0))
            i_col = gates[:, gi + hd:gi + hd + 1]
            b_col = cum[:, gi + 4 + hd:gi + 4 + hd + 1]
            i_row = gates_t[gi + hd:gi + hd + 1, :]
            b_row = cum_t[gi + 4 + hd:gi + 4 + hd + 1, :]
            slot = slot0 + hd
            c_aug = c_ref[slot]
            m_prev = m_ref[slot:slot + 1, 0:1]
            log_d = jnp.where(mask, b_col - b_row + i_row, NEG_BIG)
            log_inter = b_col + m_prev
            m_t = jnp.maximum(log_inter, jnp.max(log_d, axis=-1, keepdims=True))
            d_mat = jnp.exp(log_d - m_t)
            w_intra = d_mat * _dot_nt(q_pair, k_h)
            w_inter = jnp.exp(log_inter - m_t)
            r = w_inter * _dot(q_pair, c_aug) + _dot(w_intra, v_aug)
            den = r[:, 64:65] if sub == 0 else r[:, 0:1]
            halves.append(r / jnp.maximum(jnp.abs(den), jnp.exp(-m_t)))
            b_end = b_col[end:end + 1, :]
            log_w = b_end - b_col + i_col
            m_new = jnp.maximum(b_end + m_prev, jnp.max(log_w, axis=0, keepdims=True))
            w_st = jnp.exp(log_w - m_new)
            decay = jnp.exp(b_end + m_prev - m_new)
            c_ref[slot] = decay * c_aug + _dot_tn(k_h * w_st, v_aug)
            m_ref[slot:slot + 1, :] = jnp.broadcast_to(m_new, (1, LANE))
        outs.append(jnp.where(lane_lo, halves[0], halves[1]))
    return jnp.concatenate(outs, axis=1)


def _mlstm_kernel(*refs, need_ctx):
    ctx_ref, lat_ref, trif_ref, trib_ref, nw_ref = refs[:5]
    if need_ctx:
        yc_ref, yl_ref, ybc_ref, ybl_ref, c_ref, m_ref = refs[5:]
    else:
        yl_ref, ybc_ref, ybl_ref, c_ref, m_ref = refs[5:]
        yc_ref = None
    L = A_CHUNK
    lane = lax.broadcasted_iota(jnp.int32, (L, LANE), 1)
    lane_lo, lane0, lane64 = lane < 64, lane == 0, lane == 64
    row = lax.broadcasted_iota(jnp.int32, (L, L), 0)
    col = lax.broadcasted_iota(jnp.int32, (L, L), 1)
    mask_f, mask_b = row >= col, row <= col
    c_ref[...] = jnp.zeros_like(c_ref)
    m_ref[...] = jnp.zeros_like(m_ref)

    def scan(src_ref, yf_ref, yb_ref):
        n_chunks = src_ref.shape[1] // L

        def body(j, carry):
            sf = pl.multiple_of(j * L, L)
            sb = pl.multiple_of((n_chunks - 1 - j) * L, L)
            hf = _mlstm_chunk(src_ref[0, pl.ds(sf, L), :], trif_ref[...], mask_f, c_ref, m_ref, 0, False,
                              lane_lo, lane0, lane64)
            hb = _mlstm_chunk(src_ref[0, pl.ds(sb, L), :], trib_ref[...], mask_b, c_ref, m_ref, HEADS, True,
                              lane_lo, lane0, lane64)
            if yf_ref is not None:
                yf_ref[0, pl.ds(sf, L), :] = hf
                yb_ref[pl.ds(sb, L), :] = hb
            return carry

        lax.fori_loop(0, n_chunks, body, 0)

    def finalize(src_ref, y_ref, yb_ref):
        n_chunks = src_ref.shape[1] // L

        def body(j, carry):
            s = pl.multiple_of(j * L, L)
            for pair in range(HEADS // 2):
                cs = slice(128 * pair, 128 * pair + 128)
                y = y_ref[0, pl.ds(s, L), cs] + yb_ref[pl.ds(s, L), cs]
                y2 = y * y
                s_lo = jnp.sum(jnp.where(lane_lo, y2, 0.0), axis=-1, keepdims=True)
                s_all = jnp.sum(y2, axis=-1, keepdims=True)
                ms = jnp.where(lane_lo, s_lo, s_all - s_lo) * (1.0 / HEAD_DIM)
                o_gate = src_ref[0, pl.ds(s, L), 768 + 128 * pair:768 + 128 * pair + 128]
                y_ref[0, pl.ds(s, L), cs] = y * lax.rsqrt(ms + NORM_EPS) * nw_ref[:, cs] * _sigmoid(o_gate)
            return carry

        lax.fori_loop(0, n_chunks, body, 0)

    scan(ctx_ref, yc_ref, ybc_ref)
    scan(lat_ref, yl_ref, ybl_ref)
    if need_ctx:
        finalize(ctx_ref, yc_ref, ybc_ref)
    finalize(lat_ref, yl_ref, ybl_ref)


def _mlstm_call(pa_ctx, pa_lat, nw, need_ctx):
    b, nc, wa = pa_ctx.shape
    n = pa_lat.shape[1]
    L = A_CHUNK
    tri = np.tril(np.ones((L, L), np.float32))
    tri_f = jnp.asarray(tri, BF16)
    tri_b = jnp.asarray(tri.T, BF16)
    const = lambda i: (0, 0)
    out_specs = [pl.BlockSpec((1, n, 256), lambda i: (i, 0, 0))]
    out_shape = [jax.ShapeDtypeStruct((b, n, 256), F32)]
    if need_ctx:
        out_specs.insert(0, pl.BlockSpec((1, nc, 256), lambda i: (i, 0, 0)))
        out_shape.insert(0, jax.ShapeDtypeStruct((b, nc, 256), F32))
    res = pl.pallas_call(
        functools.partial(_mlstm_kernel, need_ctx=need_ctx),
        grid=(b,),
        in_specs=[pl.BlockSpec((1, nc, wa), lambda i: (i, 0, 0)),
                  pl.BlockSpec((1, n, wa), lambda i: (i, 0, 0)),
                  pl.BlockSpec((L, L), const), pl.BlockSpec((L, L), const),
                  pl.BlockSpec((1, 256), const)],
        out_specs=out_specs,
        out_shape=out_shape,
        scratch_shapes=[pltpu.VMEM((nc, 256), F32), pltpu.VMEM((n, 256), F32),
                        pltpu.VMEM((2 * HEADS, LANE, LANE), F32), pltpu.VMEM((2 * HEADS, LANE), F32)],
        compiler_params=_cparams(1),
        name="mlstm",
    )(pa_ctx, pa_lat, tri_f, tri_b, nw)
    return (res[0], res[1]) if need_ctx else (None, res[0])


def _hgrn_tables(L):
    n_lev = int(math.log2(L))
    t = np.arange(L)
    tri_f = (t[:, None] >= t[None, :]).astype(np.float32)
    tri_b = (t[:, None] <= t[None, :]).astype(np.float32)
    cum_f, cum_b, mask_f, mask_b = [tri_f], [tri_b], [], []
    for lev in range(n_lev):
        half, size = 1 << lev, 2 << lev
        start = (t // size) * size
        cum_f.append(tri_f[start + half - 1])
        cum_b.append(tri_b[start + half])
        same = (t[:, None] // size) == (t[None, :] // size)
        upper = (t // half) % 2 == 1
        mask_f.append((same & upper[:, None] & ~upper[None, :]).astype(np.float32))
        mask_b.append((same & ~upper[:, None] & upper[None, :]).astype(np.float32))
    eye = np.eye(L, dtype=np.float32)
    mask_f.append(eye)
    mask_b.append(eye)
    return (np.concatenate(cum_f, 0), np.concatenate(cum_b, 0), np.stack(mask_f), np.stack(mask_b), n_lev)


def _hgrn_chunk(blk, f_pre, lb, cum_ref, mask_ref, s_ref, slot0, reverse, lane_lo, n_lev):
    L = blk.shape[0]
    qs = _silu(blk[:, 0:256])
    v = blk[:, 256:512]
    sig = _sigmoid(f_pre)
    log_f = jnp.log(jnp.maximum(lb + (1.0 - lb) * sig, F_EPS))
    k = (1.0 - lb) * _sigmoid(-f_pre)
    br = _dot_exact_lhs01(cum_ref[...], log_f)
    b = br[0:L]
    end = 0 if reverse else L - 1
    b_end = b[end:end + 1, :]
    q_lev, e_lev = [], []
    for lev in range(n_lev):
        ref_b = br[(lev + 1) * L:(lev + 2) * L]
        q_lev.append(qs * jnp.exp(jnp.minimum(b - ref_b, 0.0)))
        e_lev.append(jnp.exp(jnp.minimum(ref_b - b, 0.0)))
    q_in = qs * jnp.exp(b)
    k_out = k * jnp.exp(b_end - b)
    decay = jnp.exp(b_end)
    outs = []
    for pair in range(HEADS // 2):
        cs = slice(128 * pair, 128 * pair + 128)
        v_pair = v[:, cs]
        halves = []
        for sub in range(2):
            hd = 2 * pair + sub
            own = lane_lo if sub == 0 else jnp.logical_not(lane_lo)
            k_h = jnp.where(own, k[:, cs], 0.0)
            scores = mask_ref[n_lev] * _dot_nt(qs[:, cs], k_h)
            for lev in range(n_lev):
                scores = scores + mask_ref[lev] * _dot_nt(q_lev[lev][:, cs], k_h * e_lev[lev][:, cs])
            slot = slot0 + hd
            s_t = s_ref[slot]
            halves.append(_dot(scores, v_pair) + _dot_nt(q_in[:, cs], s_t))
            s_ref[slot] = decay[:, cs] * s_t + _dot_tn(v_pair, jnp.where(own, k_out[:, cs], 0.0))
        outs.append(jnp.where(lane_lo, halves[0], halves[1]))
    return jnp.concatenate(outs, axis=1)


def _hgrn_kernel(*refs, need_ctx, layer, n_lev):
    ctx_ref, lat_ref, cumf_ref, cumb_ref, maskf_ref, maskb_ref, lbl_ref, nw_ref = refs[:8]
    if need_ctx:
        yc_ref, yl_ref, ybc_ref, ybl_ref, s_ref = refs[8:]
    else:
        yl_ref, ybc_ref, ybl_ref, s_ref = refs[8:]
        yc_ref = None
    L = D_CHUNK
    lane_lo = lax.broadcasted_iota(jnp.int32, (L, LANE), 1) < 64
    logits = lbl_ref[...]
    e = jnp.exp(logits - jnp.max(logits, axis=0, keepdims=True))
    prob = e / jnp.sum(e, axis=0, keepdims=True)
    lb = jnp.sum(prob[0:layer + 1], axis=0, keepdims=True) - prob[0:1]
    s_ref[...] = jnp.zeros_like(s_ref)

    def scan(src_ref, yf_ref, yb_ref):
        n_chunks = src_ref.shape[1] // L

        def body(j, carry):
            sf = pl.multiple_of(j * L, L)
            sb = pl.multiple_of((n_chunks - 1 - j) * L, L)
            blk_f = src_ref[0, pl.ds(sf, L), :]
            blk_b = src_ref[0, pl.ds(sb, L), :]
            of = _hgrn_chunk(blk_f, blk_f[:, 512:768], lb, cumf_ref, maskf_ref, s_ref, 0, False, lane_lo, n_lev)
            ob = _hgrn_chunk(blk_b, blk_b[:, 768:1024], lb, cumb_ref, maskb_ref, s_ref, HEADS, True, lane_lo, n_lev)
            if yf_ref is not None:
                yf_ref[0, pl.ds(sf, L), :] = of
                yb_ref[pl.ds(sb, L), :] = ob
            return carry

        lax.fori_loop(0, n_chunks, body, 0)

    def finalize(src_ref, y_ref, yb_ref):
        n_chunks = src_ref.shape[1] // L

        def body(j, carry):
            s = pl.multiple_of(j * L, L)
            for pair in range(HEADS // 2):
                cs = slice(128 * pair, 128 * pair + 128)
                y = y_ref[0, pl.ds(s, L), cs] + yb_ref[pl.ds(s, L), cs]
                y2 = y * y
                s_lo = jnp.sum(jnp.where(lane_lo, y2, 0.0), axis=-1, keepdims=True)
                s_all = jnp.sum(y2, axis=-1, keepdims=True)
                ms = jnp.where(lane_lo, s_lo, s_all - s_lo) * (1.0 / HEAD_DIM)
                g = src_ref[0, pl.ds(s, L), 1024 + 128 * pair:1024 + 128 * pair + 128]
                y_ref[0, pl.ds(s, L), cs] = y * lax.rsqrt(ms + NORM_EPS) * nw_ref[:, cs] * _silu(g)
            return carry

        lax.fori_loop(0, n_chunks, body, 0)

    scan(ctx_ref, yc_ref, ybc_ref)
    scan(lat_ref, yl_ref, ybl_ref)
    if need_ctx:
        finalize(ctx_ref, yc_ref, ybc_ref)
    finalize(lat_ref, yl_ref, ybl_ref)


def _hgrn_call(pd_ctx, pd_lat, lb_logits, nw, layer, need_ctx):
    b, nc, wd = pd_ctx.shape
    n = pd_lat.shape[1]
    L = D_CHUNK
    cum_f, cum_b, mask_f, mask_b, n_lev = _hgrn_tables(L)
    cum_f, cum_b = jnp.asarray(cum_f, BF16), jnp.asarray(cum_b, BF16)
    mask_f, mask_b = jnp.asarray(mask_f, F32), jnp.asarray(mask_b, F32)
    c2 = lambda i: (0, 0)
    c3 = lambda i: (0, 0, 0)
    out_specs = [pl.BlockSpec((1, n, 256), lambda i: (i, 0, 0))]
    out_shape = [jax.ShapeDtypeStruct((b, n, 256), F32)]
    if need_ctx:
        out_specs.insert(0, pl.BlockSpec((1, nc, 256), lambda i: (i, 0, 0)))
        out_shape.insert(0, jax.ShapeDtypeStruct((b, nc, 256), F32))
    res = pl.pallas_call(
        functools.partial(_hgrn_kernel, need_ctx=need_ctx, layer=layer, n_lev=n_lev),
        grid=(b,),
        in_specs=[pl.BlockSpec((1, nc, wd), lambda i: (i, 0, 0)),
                  pl.BlockSpec((1, n, wd), lambda i: (i, 0, 0)),
                  pl.BlockSpec(cum_f.shape, c2), pl.BlockSpec(cum_b.shape, c2),
                  pl.BlockSpec(mask_f.shape, c3), pl.BlockSpec(mask_b.shape, c3),
                  pl.BlockSpec(lb_logits.shape, c2), pl.BlockSpec((1, 256), c2)],
        out_specs=out_specs,
        out_shape=out_shape,
        scratch_shapes=[pltpu.VMEM((nc, 256), F32), pltpu.VMEM((n, 256), F32),
                        pltpu.VMEM((2 * HEADS, LANE, LANE), F32)],
        compiler_params=_cparams(1),
        name="hgrn2",
    )(pd_ctx, pd_lat, cum_f, cum_b, mask_f, mask_b, lb_logits, nw)
    return (res[0], res[1]) if need_ctx else (None, res[0])


def _gmlp_kernel(pc_ref, ws_ref, bs_ref, nw_ref, nb_ref, o_ref):
    L = C_CHUNK
    gw = 256 // C_GROUPS
    n_chunks = pc_ref.shape[0] // L
    for c in range(n_chunks):
        x = pc_ref[c * L:(c + 1) * L, :]
        g = 0.5 * x * (1.0 + jnp.tanh(math.sqrt(2.0 / math.pi) * (x + 0.044715 * (x * x * x))))
        u, v = g[:, 0:256], g[:, 256:512]
        mu = jnp.mean(v, axis=-1, keepdims=True)
        vc = v - mu
        var = jnp.mean(vc * vc, axis=-1, keepdims=True)
        vn = vc * lax.rsqrt(var + NORM_EPS) * nw_ref[...] + nb_ref[...]
        mixed = [_dot(ws_ref[gi], vn[:, gi * gw:(gi + 1) * gw]) + bs_ref[:, gi:gi + 1] for gi in range(C_GROUPS)]
        o_ref[c * L:(c + 1) * L, :] = u * jnp.concatenate(mixed, axis=1)


def _gmlp_call(pc2d, ws, bs_t, nw, nb, tm):
    r = pc2d.shape[0]
    c2 = lambda i: (0, 0)
    return pl.pallas_call(
        _gmlp_kernel,
        grid=(r // tm,),
        in_specs=[pl.BlockSpec((tm, 512), lambda i: (i, 0)),
                  pl.BlockSpec(ws.shape, lambda i: (0, 0, 0)),
                  pl.BlockSpec(bs_t.shape, c2), pl.BlockSpec((1, 256), c2), pl.BlockSpec((1, 256), c2)],
        out_specs=pl.BlockSpec((tm, 256), lambda i: (i, 0)),
        out_shape=jax.ShapeDtypeStruct((r, 256), F32),
        compiler_params=_cparams(1),
        name="gmlp",
    )(pc2d, ws, bs_t, nw, nb)


def _filter_feats(n):
    t = np.linspace(0.0, 1.0, n, dtype=np.float32)[:, None]
    bands = np.linspace(1e-4, B_BANDS - 1, B_BANDS, dtype=np.float32)[None]
    ang = (np.float32(2 * math.pi) * bands * np.arange(n, dtype=np.float32)[:, None] / np.float32(n)).astype(np.float32)
    z = np.concatenate([t, np.cos(ang), -np.sin(ang)], axis=-1).astype(np.float32)
    deltas = np.abs(np.linspace(math.log(B_DECAY_TARGET) / B_SLOW_DECAY,
                                math.log(B_DECAY_TARGET) / B_FAST_DECAY, 256, dtype=np.float32))
    neg_t_deltas = (-t * deltas[None]).astype(np.float32)
    return z, neg_t_deltas


def _filter_kernel(z_ref, ntd_ref, w1_ref, b1_ref, f1_ref, w2_ref, b2_ref, f2_ref, w3_ref, o_ref):
    hd = jnp.sin(f1_ref[...] * (_dot_f32(z_ref[...], w1_ref[...]) + b1_ref[...]))
    hd = jnp.sin(f2_ref[...] * (_dot_f32(hd, w2_ref[...]) + b2_ref[...]))
    hk = _dot_f32(hd, w3_ref[...])
    decay = jnp.exp(ntd_ref[...])
    o_ref[...] = hk * jnp.concatenate([decay, decay], axis=1)


def _filter_call(n, w1, b1, f1, w2, b2, f2, w3):
    z, ntd = _filter_feats(n)
    k_pad = 32
    z = np.pad(z, ((0, 0), (0, k_pad - z.shape[1])))
    w1 = jnp.pad(w1, ((0, k_pad - w1.shape[0]), (0, 0)))
    args = (jnp.asarray(z), jnp.asarray(ntd), w1, b1, f1, w2, b2, f2, w3)
    return pl.pallas_call(
        _filter_kernel,
        grid=(1,),
        in_specs=[pl.BlockSpec(a.shape, lambda i: (0, 0)) for a in args],
        out_specs=pl.BlockSpec((n, 512), lambda i: (0, 0)),
        out_shape=jax.ShapeDtypeStruct((n, 512), F32),
        compiler_params=_cparams(1),
        name="hyena_filter",
    )(*args)


def _short_conv_kernel(pb_ref, w_ref, b_ref, x0_ref, u_ref):
    x = pb_ref[0]
    n = x.shape[0]
    row = lax.broadcasted_iota(jnp.int32, x.shape, 0)
    prev = jnp.where(row == 0, 0.0, pltpu.roll(x, 1, axis=0))
    nxt = jnp.where(row == n - 1, 0.0, pltpu.roll(x, n - 1, axis=0))
    y = prev * w_ref[0:1, :] + x * w_ref[1:2, :] + nxt * w_ref[2:3, :] + b_ref[...]
    x0_ref[0] = y[:, 0:256]
    u_ref[0] = y[:, 256:512] * y[:, 512:768]


def _short_conv_call(pb, w, b):
    bsz, n, c3 = pb.shape
    return pl.pallas_call(
        _short_conv_kernel,
        grid=(bsz,),
        in_specs=[pl.BlockSpec((1, n, c3), lambda i: (i, 0, 0)),
                  pl.BlockSpec(w.shape, lambda i: (0, 0)), pl.BlockSpec(b.shape, lambda i: (0, 0))],
        out_specs=[pl.BlockSpec((1, n, 256), lambda i: (i, 0, 0))] * 2,
        out_shape=[jax.ShapeDtypeStruct((bsz, n, 256), F32)] * 2,
        compiler_params=_cparams(1),
        name="hyena_short_conv",
    )(pb, w, b)


def _long_conv_kernel(u_ref, k_ref, o_ref, *, n_blk, bsz, ch_per_step):
    T = CONV_BLOCK
    for c in range(ch_per_step):
        acc = [None] * n_blk
        for d in range(-(n_blk - 1), n_blk):
            seg = k_ref[c, :, (d + n_blk - 1) * T:(d + n_blk + 1) * T]
            rolled = pltpu.roll(jnp.broadcast_to(seg, (T, 2 * T)), 0, axis=1, stride=1, stride_axis=0)
            tile = rolled[:, T:2 * T].astype(BF16)
            j_lo, j_hi = max(0, -d), min(n_blk, n_blk - d)
            lhs = u_ref[c, j_lo * bsz:j_hi * bsz, :].astype(BF16)
            res = jnp.dot(lhs, tile, preferred_element_type=F32)
            for j in range(j_lo, j_hi):
                part = res[(j - j_lo) * bsz:(j - j_lo + 1) * bsz]
                i = j + d
                acc[i] = part if acc[i] is None else acc[i] + part
        o_ref[c] = jnp.concatenate(acc, axis=0)


def _long_conv_call(u_t, k_t, n_blk, bsz, ch_per_step):
    ch, rows, T = u_t.shape
    return pl.pallas_call(
        functools.partial(_long_conv_kernel, n_blk=n_blk, bsz=bsz, ch_per_step=ch_per_step),
        grid=(ch // ch_per_step,),
        in_specs=[pl.BlockSpec((ch_per_step, rows, T), lambda i: (i, 0, 0)),
                  pl.BlockSpec((ch_per_step, 1, k_t.shape[-1]), lambda i: (i, 0, 0))],
        out_specs=pl.BlockSpec((ch_per_step, rows, T), lambda i: (i, 0, 0)),
        out_shape=jax.ShapeDtypeStruct((ch, rows, T), F32),
        compiler_params=_cparams(1),
        name="hyena_long_conv",
    )(u_t, k_t)


def _hyena_conv(u, filt):
    bsz, n, ch = u.shape
    T = CONV_BLOCK
    n_blk = n // T
    k_full = jnp.concatenate([jnp.zeros((1, ch), F32), filt[:0:-1, 256:512], filt[:, 0:256]], axis=0)
    k_t = k_full.T.reshape(ch, 1, 2 * n)
    u_t = u.reshape(bsz, n_blk, T, ch).transpose(3, 1, 0, 2).reshape(ch, n_blk * bsz, T)
    y_t = _long_conv_call(u_t, k_t, n_blk, bsz, 4)
    return y_t.reshape(ch, n_blk, bsz, T).transpose(2, 1, 3, 0).reshape(bsz, n, ch)


def _merge_kernel(x_ref, mod_ref, nw_ref, ya_ref, x0_ref, u_ref, yconv_ref, yc_ref, yd_ref, hyb_ref,
                  wg_ref, wb_ref, wo_ref, o_ref, *, d):
    x = x_ref[...]
    h = _norm_mod(x, nw_ref[...], mod_ref[0, :, 0:d], mod_ref[0, :, d:2 * d]).astype(BF16)
    y_b = x0_ref[...] * (yconv_ref[...] + u_ref[...] * hyb_ref[...])
    ys = (ya_ref[...], y_b, yc_ref[...], yd_ref[...])
    acc = None
    for j in range(N_BRANCH):
        gate = _sigmoid(jnp.dot(h, wg_ref[j], preferred_element_type=F32))
        term = gate * jnp.dot(ys[j].astype(BF16), wb_ref[j], preferred_element_type=F32)
        acc = term if acc is None else acc + term
    y = jnp.dot(acc.astype(BF16), wo_ref[...], preferred_element_type=F32)
    o_ref[...] = x + mod_ref[0, :, 2 * d:3 * d] * y


def _merge_call(x2d, mod, nw, ya, x0, u, yconv, yc, yd, hyb, wg, wb, wo, rows_per_mod, tm):
    r, d = x2d.shape
    blocks_per_mod = rows_per_mod // tm
    c2 = lambda i: (0, 0)
    c3 = lambda i: (0, 0, 0)
    row = lambda w: pl.BlockSpec((tm, w), lambda i: (i, 0))
    return pl.pallas_call(
        functools.partial(_merge_kernel, d=d),
        grid=(r // tm,),
        in_specs=[row(d), pl.BlockSpec((1, 1, mod.shape[-1]), lambda i: (i // blocks_per_mod, 0, 0)),
                  pl.BlockSpec((1, d), c2), row(256), row(256), row(256), row(256), row(256), row(256),
                  pl.BlockSpec((1, 256), c2),
                  pl.BlockSpec(wg.shape, c3), pl.BlockSpec(wb.shape, c3), pl.BlockSpec(wo.shape, c2)],
        out_specs=row(d),
        out_shape=jax.ShapeDtypeStruct((r, d), F32),
        compiler_params=_cparams(1),
        name="merge",
    )(x2d, mod, nw, ya, x0, u, yconv, yc, yd, hyb, wg, wb, wo)


def _ffn_kernel(x_ref, mod_ref, nw_ref, wa_ref, wg_ref, wd_ref, fnw_ref, o_ref, *, d, hid_chunk, final_norm):
    x = x_ref[...]
    h = _norm_mod(x, nw_ref[...], mod_ref[0, :, 3 * d:4 * d], mod_ref[0, :, 4 * d:5 * d]).astype(BF16)
    hidden = wa_ref.shape[1]
    acc = None
    for c in range(hidden // hid_chunk):
        cs = slice(c * hid_chunk, (c + 1) * hid_chunk)
        a = jnp.dot(h, wa_ref[:, cs], preferred_element_type=F32)
        g = jnp.dot(h, wg_ref[:, cs], preferred_element_type=F32)
        term = jnp.dot((_silu(a) * g).astype(BF16), wd_ref[cs, :], preferred_element_type=F32)
        acc = term if acc is None else acc + term
    y = x + mod_ref[0, :, 5 * d:6 * d] * acc
    if final_norm:
        y = y * lax.rsqrt(jnp.mean(y * y, axis=-1, keepdims=True) + NORM_EPS) * fnw_ref[...]
    o_ref[...] = y


def _ffn_call(x2d, mod, nw, wa, wg, wd, fnw, rows_per_mod, tm, final_norm):
    r, d = x2d.shape
    blocks_per_mod = rows_per_mod // tm
    c2 = lambda i: (0, 0)
    return pl.pallas_call(
        functools.partial(_ffn_kernel, d=d, hid_chunk=256, final_norm=final_norm),
        grid=(r // tm,),
        in_specs=[pl.BlockSpec((tm, d), lambda i: (i, 0)),
                  pl.BlockSpec((1, 1, mod.shape[-1]), lambda i: (i // blocks_per_mod, 0, 0)),
                  pl.BlockSpec((1, d), c2),
                  pl.BlockSpec(wa.shape, c2), pl.BlockSpec(wg.shape, c2), pl.BlockSpec(wd.shape, c2),
                  pl.BlockSpec((1, d), c2)],
        out_specs=pl.BlockSpec((tm, d), lambda i: (i, 0)),
        out_shape=jax.ShapeDtypeStruct((r, d), F32),
        compiler_params=_cparams(1),
        name="ffn",
    )(x2d, mod, nw, wa, wg, wd, fnw)


def _grid_transpose(x, rows, cols):
    bsz, n, d = x.shape
    return x.reshape(bsz, rows, cols, d).swapaxes(1, 2).reshape(bsz, n, d)


def _row_tile(rows):
    return 512 if rows % 512 == 0 else 256


def kernel(x, c, ctx, c_ctx, ada_w, ada_b, norm1_w, norm2_w, w_in, b_in, mlstm_norm_w, hy_short_w, hy_short_b, hy_w1, hy_b1, hy_freq1, hy_w2, hy_b2, hy_freq2, hy_w3, hy_bias, gm_norm_w, gm_norm_b, gm_ws, gm_bs, hg_lb_logits, hg_norm_w, w_gate, w_branch, w_out, w_ffn_in, w_ffn_out, final_norm_w):
    bsz, n, d = x.shape
    nc = ctx.shape[1]
    depth = ada_w.shape[0]
    rows = n // GRID_W
    mw = d // N_BRANCH
    hidden = w_ffn_out.shape[1]
    assert mw == 256 and mw == HEADS * HEAD_DIM

    mb = -(-(bsz + 1) // 8) * 8
    cvec = jnp.concatenate([c, c_ctx[None], jnp.zeros((mb - bsz - 1, d), F32)], axis=0)
    mods = _ada_call(cvec, ada_w, ada_b)

    s_a, s_g, s_b, s_c = 4 * mw, 4 * mw + 4 * HEADS, 7 * mw + 4 * HEADS, 9 * mw + 4 * HEADS
    gate_pad = LANE - 4 * HEADS

    xc = ctx
    for l in range(depth):
        need_ctx = l < depth - 1
        col_major = l % 2 == 1
        if col_major:
            x = _grid_transpose(x, rows, GRID_W)
        mod_lat = mods[l, :bsz].reshape(bsz, 1, 6 * d)
        mod_ctx = mods[l, bsz:bsz + 1].reshape(1, 1, 6 * d)
        wl, bl = w_in[l], b_in[l]
        w_segs = [jnp.concatenate([wl[:, :s_g], jnp.zeros((d, gate_pad), F32)], axis=1),
                  wl[:, s_g:s_b], wl[:, s_b:s_c], wl[:, s_c:]]
        b_segs = [jnp.concatenate([bl[:s_g], jnp.zeros((gate_pad,), F32)]), bl[s_g:s_b], bl[s_b:s_c], bl[s_c:]]
        w_segs = [w.astype(BF16) for w in w_segs]
        b_segs = [b.reshape(1, -1) for b in b_segs]
        nw1 = norm1_w[l].reshape(1, d)

        x2d = x.reshape(bsz * n, d)
        xc2d = xc.reshape(bsz * nc, d)
        pa, pb, pc, pd = _in_call(x2d, mod_lat, nw1, w_segs, b_segs, n, _row_tile(n))
        if need_ctx:
            ca, cb, cc, cd = _in_call(xc2d, mod_ctx, nw1, w_segs, b_segs, bsz * nc, _row_tile(bsz * nc))
        else:
            ca, cd = _in_call(xc2d, mod_ctx, nw1, [w_segs[0], w_segs[3]], [b_segs[0], b_segs[3]],
                              bsz * nc, _row_tile(bsz * nc))

        yac, ya = _mlstm_call(ca.reshape(bsz, nc, -1), pa.reshape(bsz, n, -1),
                              mlstm_norm_w[l].reshape(1, mw), need_ctx)
        ydc, yd = _hgrn_call(cd.reshape(bsz, nc, -1), pd.reshape(bsz, n, -1), hg_lb_logits,
                             hg_norm_w[l].reshape(1, mw), l, need_ctx)

        hy_args = (hy_w1[l], hy_b1[l].reshape(1, -1), hy_freq1[l].reshape(1, -1), hy_w2[l],
                   hy_b2[l].reshape(1, -1), hy_freq2[l].reshape(1, -1), hy_w3[l])
        sw, sb = hy_short_w[l], hy_short_b[l].reshape(1, -1)
        gws, gbs_t = gm_ws[l], gm_bs[l].T
        gnw, gnb = gm_norm_w[l].reshape(1, mw), gm_norm_b[l].reshape(1, mw)
        hyb = hy_bias[l].reshape(1, mw)
        wg, wb, wo = w_gate[l].astype(BF16), w_branch[l].astype(BF16), w_out[l].astype(BF16)
        wfa, wfg = w_ffn_in[l][:, :hidden].astype(BF16), w_ffn_in[l][:, hidden:].astype(BF16)
        wfd = w_ffn_out[l].astype(BF16)
        nw2 = norm2_w[l].reshape(1, d)
        fnw = final_norm_w.reshape(1, d)

        x0, u = _short_conv_call(pb.reshape(bsz, n, -1), sw, sb)
        yconv = _hyena_conv(u, _filter_call(n, *hy_args))
        ycm = _gmlp_call(pc, gws, gbs_t, gnw, gnb, _row_tile(n))
        x2d = _merge_call(x2d, mod_lat, nw1, ya.reshape(bsz * n, mw), x0.reshape(bsz * n, mw),
                          u.reshape(bsz * n, mw), yconv.reshape(bsz * n, mw), ycm, yd.reshape(bsz * n, mw),
                          hyb, wg, wb, wo, n, _row_tile(n))
        x2d = _ffn_call(x2d, mod_lat, nw2, wfa, wfg, wfd, fnw, n, _row_tile(n), l == depth - 1)
        x = x2d.reshape(bsz, n, d)

        if need_ctx:
            tmc = _row_tile(bsz * nc)
            x0c, uc = _short_conv_call(cb.reshape(bsz, nc, -1), sw, sb)
            yconv_c = _hyena_conv(uc, _filter_call(nc, *hy_args))
            ycm_c = _gmlp_call(cc, gws, gbs_t, gnw, gnb, tmc)
            xc2d = _merge_call(xc2d, mod_ctx, nw1, yac.reshape(bsz * nc, mw), x0c.reshape(bsz * nc, mw),
                               uc.reshape(bsz * nc, mw), yconv_c.reshape(bsz * nc, mw), ycm_c,
                               ydc.reshape(bsz * nc, mw), hyb, wg, wb, wo, bsz * nc, tmc)
            xc2d = _ffn_call(xc2d, mod_ctx, nw2, wfa, wfg, wfd, fnw, bsz * nc, tmc, False)
            xc = xc2d.reshape(bsz, nc, d)
        if col_major:
            x = _grid_transpose(x, GRID_W, rows)
    return x
```

```python
import functools
import math

import numpy as np
import jax
import jax.numpy as jnp
from jax import lax
from jax.experimental import pallas as pl
from jax.experimental.pallas import tpu as pltpu

F32 = jnp.float32
BF16 = jnp.bfloat16

GRID_W = 64
NORM_EPS = 1e-6
N_BRANCH = 4
NEG_BIG = -1e30
F_EPS = 1e-30
HEAD_DIM = 64
HEADS = 4
LANE = 128
A_CHUNK = 128
D_CHUNK = 128
C_CHUNK = 128
C_GROUPS = 4
B_SHORT = 3
B_BANDS = 8
B_DECAY_TARGET = 1e-2
B_FAST_DECAY = 0.3
B_SLOW_DECAY = 1.5
CONV_BLOCK = 256
VMEM_LIMIT = 56 * 1024 * 1024


def _cparams(n_axes):
    return pltpu.CompilerParams(dimension_semantics=("arbitrary",) * n_axes,
                                vmem_limit_bytes=VMEM_LIMIT)


def _dot(a, b):
    return jnp.dot(a.astype(BF16), b.astype(BF16), preferred_element_type=F32)


def _split3(x):
    hi = x.astype(BF16)
    r1 = x - hi.astype(F32)
    mid = r1.astype(BF16)
    lo = (r1 - mid.astype(F32)).astype(BF16)
    return hi, mid, lo


def _dot_f32(a, b):
    a_hi, a_mid, a_lo = _split3(a)
    b_hi, b_mid, b_lo = _split3(b)
    d = lambda u, v: jnp.dot(u, v, preferred_element_type=F32)
    return (d(a_hi, b_hi) + (d(a_hi, b_mid) + d(a_mid, b_hi))
            + (d(a_hi, b_lo) + d(a_mid, b_mid) + d(a_lo, b_hi)))


def _sigmoid(x):
    return 1.0 / (1.0 + jnp.exp(-x))


def _silu(x):
    return x * _sigmoid(x)


def _log_sigmoid(x):
    return jnp.minimum(x, 0.0) - jnp.log(1.0 + jnp.exp(-jnp.abs(x)))


def _norm_mod(x, nw, shift, scale):
    ms = jnp.mean(x * x, axis=-1, keepdims=True)
    y = x * lax.rsqrt(ms + NORM_EPS) * nw
    return y * (1.0 + scale) + shift


def _ada_kernel(c_ref, w_ref, b_ref, o_ref):
    o_ref[0] = _dot_f32(_silu(c_ref[...]), w_ref[0]) + b_ref[0]


def _ada_call(cvec, ada_w, ada_b):
    depth, d, n6 = ada_w.shape
    mb = cvec.shape[0]
    tn = 512
    return pl.pallas_call(
        _ada_kernel,
        grid=(depth, n6 // tn),
        in_specs=[pl.BlockSpec((mb, d), lambda l, j: (0, 0)),
                  pl.BlockSpec((1, d, tn), lambda l, j: (l, 0, j)),
                  pl.BlockSpec((1, 1, tn), lambda l, j: (l, 0, j))],
        out_specs=pl.BlockSpec((1, mb, tn), lambda l, j: (l, 0, j)),
        out_shape=jax.ShapeDtypeStruct((depth, mb, n6), F32),
        compiler_params=_cparams(2),
        name="ada_mod",
    )(cvec, ada_w, ada_b.reshape(depth, 1, n6))


def _in_kernel(*refs, d, n_seg):
    x_ref, mod_ref, nw_ref = refs[:3]
    w_refs = refs[3:3 + n_seg]
    b_refs = refs[3 + n_seg:3 + 2 * n_seg]
    o_refs = refs[3 + 2 * n_seg:]
    h = _norm_mod(x_ref[...], nw_ref[...], mod_ref[0, :, 0:d], mod_ref[0, :, d:2 * d]).astype(BF16)
    for w_ref, b_ref, o_ref in zip(w_refs, b_refs, o_refs):
        o_ref[...] = jnp.dot(h, w_ref[...], preferred_element_type=F32) + b_ref[...]


def _in_call(x2d, mod, nw, ws, bs, rows_per_mod, tm):
    r, d = x2d.shape
    n_seg = len(ws)
    blocks_per_mod = rows_per_mod // tm
    const = lambda i: (0, 0)
    in_specs = [pl.BlockSpec((tm, d), lambda i: (i, 0)),
                pl.BlockSpec((1, 1, mod.shape[-1]), lambda i: (i // blocks_per_mod, 0, 0)),
                pl.BlockSpec((1, d), const)]
    in_specs += [pl.BlockSpec(w.shape, const) for w in ws]
    in_specs += [pl.BlockSpec(b.shape, const) for b in bs]
    return pl.pallas_call(
        functools.partial(_in_kernel, d=d, n_seg=n_seg),
        grid=(r // tm,),
        in_specs=in_specs,
        out_specs=[pl.BlockSpec((tm, w.shape[1]), lambda i: (i, 0)) for w in ws],
        out_shape=[jax.ShapeDtypeStruct((r, w.shape[1]), F32) for w in ws],
        compiler_params=_cparams(1),
        name="in_proj",
    )(x2d, mod, nw, *ws, *bs)


def _mlstm_tables(L):
    t = np.arange(L)
    tri = (t[:, None] >= t[None, :]).astype(np.float32)
    selg = np.zeros((2, 3 * LANE, HEADS * L), np.float32)
    selp = np.zeros((2, 2 * LANE, HEADS * HEAD_DIM), np.float32)
    for d in range(2):
        for h in range(HEADS):
            c = 4 * d + h
            for part in range(3):
                selg[d, part * LANE + c, h * L:(h + 1) * L] = 1.0
            for part in range(2):
                selp[d, part * LANE + c, h * HEAD_DIM:(h + 1) * HEAD_DIM] = 1.0
    ones_blk = np.zeros((2 * L, LANE), np.float32)
    ones_blk[:L, :HEAD_DIM] = 1.0
    ones_blk[L:, HEAD_DIM:] = 1.0
    tri3 = lambda m: np.concatenate([m, m, m], axis=1)
    return tri3(tri), tri3(tri.T), selg, selp, ones_blk


def _running_max(x, row, reverse):
    L = x.shape[0]
    sh = 1
    while sh < L:
        if reverse:
            shifted, valid = pltpu.roll(x, L - sh, axis=0), row < L - sh
        else:
            shifted, valid = pltpu.roll(x, sh, axis=0), row >= sh
        x = jnp.maximum(x, jnp.where(valid, shifted, NEG_BIG))
        sh *= 2
    return x


def _mlstm_chunk(blk, tri_ref, selg_ref, selp_ref, ones_ref, cn_ref, m_ref, d, mask, row, lane_lo, sub_lo, bd):
    L = blk.shape[0]
    reverse = d == 1
    end = 0 if reverse else L - 1
    cum = jnp.dot(tri_ref[...], jnp.concatenate(_split3(_log_sigmoid(blk[:, 1152:1280])), axis=0),
                  preferred_element_type=F32)
    a = blk[:, 1024:1152] - cum
    m_prev = m_ref[d:d + 1, :]
    g = jnp.maximum(_running_max(a, row, reverse), m_prev)
    g_end = g[end:end + 1, :]
    w_inter = jnp.exp(m_prev - g)
    e_negm = jnp.exp(-(cum + g))
    m_ref[d:d + 1, :] = cum[end:end + 1, :] + g_end
    a_t = a.T
    g_b = jnp.dot(jnp.concatenate(_split3(g), axis=1), selg_ref[d], preferred_element_type=F32)
    wi_b = jnp.dot(jnp.concatenate(_split3(w_inter)[:2], axis=1), selp_ref[d], preferred_element_type=F32)
    em_b = jnp.dot(jnp.concatenate(_split3(e_negm)[:2], axis=1), selp_ref[d], preferred_element_type=F32)
    outs = []
    for pair in range(HEADS // 2):
        cs = slice(128 * pair, 128 * pair + 128)
        q_pair = blk[:, cs].astype(BF16)
        k_pair = blk[:, 256 + 128 * pair:256 + 128 * pair + 128] * (HEAD_DIM ** -0.5)
        v_pair = blk[:, 512 + 128 * pair:512 + 128 * pair + 128]
        k_b = k_pair.astype(BF16)
        zero = jnp.zeros_like(k_b)
        k_rows = jnp.concatenate([jnp.where(lane_lo, k_b, zero), jnp.where(lane_lo, zero, k_b)], axis=0)
        s_pair = lax.dot_general(q_pair, k_rows, (((1,), (1,)), ((), ())), preferred_element_type=F32)
        w_halves = []
        for sub in range(2):
            hd = 2 * pair + sub
            c = 4 * d + hd
            expo = jnp.where(mask, a_t[c:c + 1, :] - g_b[:, hd * L:(hd + 1) * L], NEG_BIG)
            w_halves.append((jnp.exp(expo) * s_pair[:, sub * L:(sub + 1) * L]).astype(BF16))
        w_pair = jnp.concatenate(w_halves, axis=1)
        v_b = v_pair.astype(BF16)
        v_rows = jnp.concatenate([jnp.where(lane_lo, v_b, zero), jnp.where(lane_lo, zero, v_b)], axis=0)
        intra = jnp.dot(w_pair, jnp.concatenate([v_rows, ones_ref[...]], axis=1), preferred_element_type=F32)
        slot = 2 * d + pair
        cn = cn_ref[slot]
        inter = jnp.dot(q_pair, cn.astype(BF16), preferred_element_type=F32)
        wi_p = wi_b[:, cs]
        num = wi_p * inter[:, 0:128] + intra[:, 0:128]
        den = wi_p * inter[:, 128:256] + intra[:, 128:256]
        outs.append(num / jnp.maximum(jnp.abs(den), em_b[:, cs]))
        c_e = 4 * d + 2 * pair
        w_st = jnp.exp(jnp.where(sub_lo, a_t[c_e:c_e + 1, :] - g_end[:, c_e:c_e + 1],
                                 a_t[c_e + 1:c_e + 2, :] - g_end[:, c_e + 1:c_e + 2]))
        ktw = (k_pair.T * w_st).astype(BF16)
        upd = jnp.dot(ktw, jnp.concatenate([v_b, jnp.ones_like(v_b)], axis=1), preferred_element_type=F32)
        decay = wi_p[end:end + 1, :]
        cn_ref[slot] = jnp.concatenate([decay, decay], axis=1) * cn + jnp.where(bd, upd, 0.0)
    return jnp.concatenate(outs, axis=1)


def _mlstm_kernel(*refs, need_ctx):
    ctx_ref, lat_ref, trif_ref, trib_ref, selg_ref, selp_ref, ones_ref, nw_ref = refs[:8]
    if need_ctx:
        yc_ref, yl_ref, ybc_ref, ybl_ref, cn_ref, m_ref = refs[8:]
    else:
        yl_ref, ybc_ref, ybl_ref, cn_ref, m_ref = refs[8:]
        yc_ref = None
    L = A_CHUNK
    lane_lo = lax.broadcasted_iota(jnp.int32, (L, LANE), 1) < HEAD_DIM
    row = lax.broadcasted_iota(jnp.int32, (L, LANE), 0)
    sub_lo = row < HEAD_DIM
    r2 = lax.broadcasted_iota(jnp.int32, (L, L), 0)
    c2 = lax.broadcasted_iota(jnp.int32, (L, L), 1)
    mask_f, mask_b = r2 >= c2, r2 <= c2
    rb = lax.broadcasted_iota(jnp.int32, (LANE, 2 * LANE), 0) < HEAD_DIM
    cb = (lax.broadcasted_iota(jnp.int32, (LANE, 2 * LANE), 1) % LANE) < HEAD_DIM
    bd = rb == cb
    cn_ref[...] = jnp.zeros_like(cn_ref)
    m_ref[...] = jnp.zeros_like(m_ref)

    def scan(src_ref, yf_ref, yb_ref):
        n_chunks = src_ref.shape[1] // L

        def body(j, carry):
            sf = pl.multiple_of(j * L, L)
            sb = pl.multiple_of((n_chunks - 1 - j) * L, L)
            hf = _mlstm_chunk(src_ref[0, pl.ds(sf, L), :], trif_ref, selg_ref, selp_ref, ones_ref, cn_ref, m_ref,
                              0, mask_f, row, lane_lo, sub_lo, bd)
            hb = _mlstm_chunk(src_ref[0, pl.ds(sb, L), :], trib_ref, selg_ref, selp_ref, ones_ref, cn_ref, m_ref,
                              1, mask_b, row, lane_lo, sub_lo, bd)
            if yf_ref is not None:
                yf_ref[0, pl.ds(sf, L), :] = hf
                yb_ref[pl.ds(sb, L), :] = hb
            return carry

        lax.fori_loop(0, n_chunks, body, 0)

    def finalize(src_ref, y_ref, yb_ref):
        n_chunks = src_ref.shape[1] // L

        def body(j, carry):
            s = pl.multiple_of(j * L, L)
            for pair in range(HEADS // 2):
                cs = slice(128 * pair, 128 * pair + 128)
                y = y_ref[0, pl.ds(s, L), cs] + yb_ref[pl.ds(s, L), cs]
                y2 = y * y
                s_lo = jnp.sum(jnp.where(lane_lo, y2, 0.0), axis=-1, keepdims=True)
                s_all = jnp.sum(y2, axis=-1, keepdims=True)
                ms = jnp.where(lane_lo, s_lo, s_all - s_lo) * (1.0 / HEAD_DIM)
                o_gate = src_ref[0, pl.ds(s, L), 768 + 128 * pair:768 + 128 * pair + 128]
                y_ref[0, pl.ds(s, L), cs] = y * lax.rsqrt(ms + NORM_EPS) * nw_ref[:, cs] * _sigmoid(o_gate)
            return carry

        lax.fori_loop(0, n_chunks, body, 0)

    scan(ctx_ref, yc_ref, ybc_ref)
    scan(lat_ref, yl_ref, ybl_ref)
    if need_ctx:
        finalize(ctx_ref, yc_ref, ybc_ref)
    finalize(lat_ref, yl_ref, ybl_ref)


def _mlstm_call(pa_ctx, pa_lat, nw, need_ctx):
    b, nc, wa = pa_ctx.shape
    n = pa_lat.shape[1]
    L = A_CHUNK
    tri_f, tri_b, selg, selp, ones_blk = (jnp.asarray(t, BF16) for t in _mlstm_tables(L))
    c2 = lambda i: (0, 0)
    c3 = lambda i: (0, 0, 0)
    out_specs = [pl.BlockSpec((1, n, 256), lambda i: (i, 0, 0))]
    out_shape = [jax.ShapeDtypeStruct((b, n, 256), F32)]
    if need_ctx:
        out_specs.insert(0, pl.BlockSpec((1, nc, 256), lambda i: (i, 0, 0)))
        out_shape.insert(0, jax.ShapeDtypeStruct((b, nc, 256), F32))
    res = pl.pallas_call(
        functools.partial(_mlstm_kernel, need_ctx=need_ctx),
        grid=(b,),
        in_specs=[pl.BlockSpec((1, nc, wa), lambda i: (i, 0, 0)),
                  pl.BlockSpec((1, n, wa), lambda i: (i, 0, 0)),
                  pl.BlockSpec(tri_f.shape, c2), pl.BlockSpec(tri_b.shape, c2),
                  pl.BlockSpec(selg.shape, c3), pl.BlockSpec(selp.shape, c3), pl.BlockSpec(ones_blk.shape, c2),
                  pl.BlockSpec((1, 256), c2)],
        out_specs=out_specs,
        out_shape=out_shape,
        scratch_shapes=[pltpu.VMEM((nc, 256), F32), pltpu.VMEM((n, 256), F32),
                        pltpu.VMEM((HEADS, LANE, 2 * LANE), F32), pltpu.VMEM((8, LANE), F32)],
        compiler_params=_cparams(1),
        name="mlstm",
    )(pa_ctx, pa_lat, tri_f, tri_b, selg, selp, ones_blk, nw)
    return (res[0], res[1]) if need_ctx else (None, res[0])


def _hgrn_tables(L):
    n_lev = int(math.log2(L))
    t = np.arange(L)
    tri_f = (t[:, None] >= t[None, :]).astype(np.float32)
    tri_b = (t[:, None] <= t[None, :]).astype(np.float32)
    cum_f, cum_b, mask_f, mask_b = [tri_f], [tri_b], [], []
    for lev in range(n_lev):
        half, size = 1 << lev, 2 << lev
        start = (t // size) * size
        cum_f.append(tri_f[start + half - 1])
        cum_b.append(tri_b[start + half])
        same = (t[:, None] // size) == (t[None, :] // size)
        upper = (t // half) % 2 == 1
        mask_f.append((same & upper[:, None] & ~upper[None, :]).astype(np.float32))
        mask_b.append((same & ~upper[:, None] & upper[None, :]).astype(np.float32))
    eye = np.eye(L, dtype=np.float32)
    mask_f.append(eye)
    mask_b.append(eye)
    cum_f, cum_b = np.concatenate(cum_f, 0), np.concatenate(cum_b, 0)
    wide = lambda m: np.concatenate([m, m], axis=-1)
    return (wide(wide(cum_f))[:, :3 * L], wide(wide(cum_b))[:, :3 * L],
            wide(np.stack(mask_f)), wide(np.stack(mask_b)), n_lev)


def _hgrn_chunk(blk, f_pre, lb, cum_ref, mask_ref, s_ref, d, lane_lo, bd, n_lev):
    L = blk.shape[0]
    end = 0 if d == 1 else L - 1
    qs = _silu(blk[:, 0:256])
    v = blk[:, 256:512]
    log_f = jnp.log(jnp.maximum(lb + (1.0 - lb) * _sigmoid(f_pre), F_EPS))
    k = (1.0 - lb) * _sigmoid(-f_pre)
    br = jnp.dot(cum_ref[...], jnp.concatenate(_split3(log_f), axis=0), preferred_element_type=F32)
    b = br[0:L]
    b_end = b[end:end + 1, :]
    q_in = qs * jnp.exp(b)
    k_out = k * jnp.exp(b_end - b)
    decay = jnp.exp(b_end)
    nt = (((1,), (1,)), ((), ()))
    outs = []
    for pair in range(HEADS // 2):
        cs = slice(128 * pair, 128 * pair + 128)
        q_p, k_p, b_p = qs[:, cs], k[:, cs], b[:, cs]
        zero = jnp.zeros((L, LANE), BF16)
        v_b = v[:, cs].astype(BF16)
        v_rows = jnp.concatenate([jnp.where(lane_lo, v_b, zero), jnp.where(lane_lo, zero, v_b)], axis=0)

        def pair_scores(q_l, k_l):
            k_l = k_l.astype(BF16)
            k_rows = jnp.concatenate([jnp.where(lane_lo, k_l, zero), jnp.where(lane_lo, zero, k_l)], axis=0)
            return lax.dot_general(q_l.astype(BF16), k_rows, nt, preferred_element_type=F32)

        scores = mask_ref[n_lev] * pair_scores(q_p, k_p)
        for lev in range(n_lev):
            e = jnp.exp(-jnp.abs(b_p - br[(lev + 1) * L:(lev + 2) * L, cs]))
            scores = scores + mask_ref[lev] * pair_scores(q_p * e, k_p * e)
        slot = 2 * d + pair
        s_t = s_ref[slot]
        outs.append(jnp.dot(scores.astype(BF16), v_rows, preferred_element_type=F32)
                    + lax.dot_general(q_in[:, cs].astype(BF16), s_t.astype(BF16), nt, preferred_element_type=F32))
        upd = lax.dot_general(v_b, k_out[:, cs].astype(BF16), (((0,), (0,)), ((), ())), preferred_element_type=F32)
        s_ref[slot] = decay[:, cs] * s_t + jnp.where(bd, upd, 0.0)
    return jnp.concatenate(outs, axis=1)


def _hgrn_kernel(*refs, need_ctx, layer, n_lev):
    ctx_ref, lat_ref, cumf_ref, cumb_ref, maskf_ref, maskb_ref, lbl_ref, nw_ref = refs[:8]
    if need_ctx:
        yc_ref, yl_ref, ybc_ref, ybl_ref, s_ref = refs[8:]
    else:
        yl_ref, ybc_ref, ybl_ref, s_ref = refs[8:]
        yc_ref = None
    L = D_CHUNK
    lane_lo = lax.broadcasted_iota(jnp.int32, (L, LANE), 1) < HEAD_DIM
    bd = ((lax.broadcasted_iota(jnp.int32, (LANE, LANE), 0) < HEAD_DIM)
          == (lax.broadcasted_iota(jnp.int32, (LANE, LANE), 1) < HEAD_DIM))
    logits = lbl_ref[...]
    e = jnp.exp(logits - jnp.max(logits, axis=0, keepdims=True))
    prob = e / jnp.sum(e, axis=0, keepdims=True)
    lb = jnp.sum(prob[0:layer + 1], axis=0, keepdims=True) - prob[0:1]
    s_ref[...] = jnp.zeros_like(s_ref)

    def scan(src_ref, yf_ref, yb_ref):
        n_chunks = src_ref.shape[1] // L

        def body(j, carry):
            sf = pl.multiple_of(j * L, L)
            sb = pl.multiple_of((n_chunks - 1 - j) * L, L)
            blk_f = src_ref[0, pl.ds(sf, L), :]
            blk_b = src_ref[0, pl.ds(sb, L), :]
            of = _hgrn_chunk(blk_f, blk_f[:, 512:768], lb, cumf_ref, maskf_ref, s_ref, 0, lane_lo, bd, n_lev)
            ob = _hgrn_chunk(blk_b, blk_b[:, 768:1024], lb, cumb_ref, maskb_ref, s_ref, 1, lane_lo, bd, n_lev)
            if yf_ref is not None:
                yf_ref[0, pl.ds(sf, L), :] = of
                yb_ref[pl.ds(sb, L), :] = ob
            return carry

        lax.fori_loop(0, n_chunks, body, 0)

    def finalize(src_ref, y_ref, yb_ref):
        n_chunks = src_ref.shape[1] // L

        def body(j, carry):
            s = pl.multiple_of(j * L, L)
            for pair in range(HEADS // 2):
                cs = slice(128 * pair, 128 * pair + 128)
                y = y_ref[0, pl.ds(s, L), cs] + yb_ref[pl.ds(s, L), cs]
                y2 = y * y
                s_lo = jnp.sum(jnp.where(lane_lo, y2, 0.0), axis=-1, keepdims=True)
                s_all = jnp.sum(y2, axis=-1, keepdims=True)
                ms = jnp.where(lane_lo, s_lo, s_all - s_lo) * (1.0 / HEAD_DIM)
                g = src_ref[0, pl.ds(s, L), 1024 + 128 * pair:1024 + 128 * pair + 128]
                y_ref[0, pl.ds(s, L), cs] = y * lax.rsqrt(ms + NORM_EPS) * nw_ref[:, cs] * _silu(g)
            return carry

        lax.fori_loop(0, n_chunks, body, 0)

    scan(ctx_ref, yc_ref, ybc_ref)
    scan(lat_ref, yl_ref, ybl_ref)
    if need_ctx:
        finalize(ctx_ref, yc_ref, ybc_ref)
    finalize(lat_ref, yl_ref, ybl_ref)


def _hgrn_call(pd_ctx, pd_lat, lb_logits, nw, layer, need_ctx):
    b, nc, wd = pd_ctx.shape
    n = pd_lat.shape[1]
    L = D_CHUNK
    cum_f, cum_b, mask_f, mask_b, n_lev = _hgrn_tables(L)
    cum_f, cum_b = jnp.asarray(cum_f, BF16), jnp.asarray(cum_b, BF16)
    mask_f, mask_b = jnp.asarray(mask_f, F32), jnp.asarray(mask_b, F32)
    c2 = lambda i: (0, 0)
    c3 = lambda i: (0, 0, 0)
    out_specs = [pl.BlockSpec((1, n, 256), lambda i: (i, 0, 0))]
    out_shape = [jax.ShapeDtypeStruct((b, n, 256), F32)]
    if need_ctx:
        out_specs.insert(0, pl.BlockSpec((1, nc, 256), lambda i: (i, 0, 0)))
        out_shape.insert(0, jax.ShapeDtypeStruct((b, nc, 256), F32))
    res = pl.pallas_call(
        functools.partial(_hgrn_kernel, need_ctx=need_ctx, layer=layer, n_lev=n_lev),
        grid=(b,),
        in_specs=[pl.BlockSpec((1, nc, wd), lambda i: (i, 0, 0)),
                  pl.BlockSpec((1, n, wd), lambda i: (i, 0, 0)),
                  pl.BlockSpec(cum_f.shape, c2), pl.BlockSpec(cum_b.shape, c2),
                  pl.BlockSpec(mask_f.shape, c3), pl.BlockSpec(mask_b.shape, c3),
                  pl.BlockSpec(lb_logits.shape, c2), pl.BlockSpec((1, 256), c2)],
        out_specs=out_specs,
        out_shape=out_shape,
        scratch_shapes=[pltpu.VMEM((nc, 256), F32), pltpu.VMEM((n, 256), F32),
                        pltpu.VMEM((HEADS, LANE, LANE), F32)],
        compiler_params=_cparams(1),
        name="hgrn2",
    )(pd_ctx, pd_lat, cum_f, cum_b, mask_f, mask_b, lb_logits, nw)
    return (res[0], res[1]) if need_ctx else (None, res[0])


def _gmlp_kernel(pc_ref, ws_ref, bs_ref, nw_ref, nb_ref, o_ref):
    L = C_CHUNK
    gw = 256 // C_GROUPS
    n_chunks = pc_ref.shape[0] // L
    for c in range(n_chunks):
        x = pc_ref[c * L:(c + 1) * L, :]
        g = 0.5 * x * (1.0 + jnp.tanh(math.sqrt(2.0 / math.pi) * (x + 0.044715 * (x * x * x))))
        u, v = g[:, 0:256], g[:, 256:512]
        mu = jnp.mean(v, axis=-1, keepdims=True)
        vc = v - mu
        var = jnp.mean(vc * vc, axis=-1, keepdims=True)
        vn = vc * lax.rsqrt(var + NORM_EPS) * nw_ref[...] + nb_ref[...]
        mixed = [_dot(ws_ref[gi], vn[:, gi * gw:(gi + 1) * gw]) + bs_ref[:, gi:gi + 1] for gi in range(C_GROUPS)]
        o_ref[c * L:(c + 1) * L, :] = u * jnp.concatenate(mixed, axis=1)


def _gmlp_call(pc2d, ws, bs_t, nw, nb, tm):
    r = pc2d.shape[0]
    c2 = lambda i: (0, 0)
    return pl.pallas_call(
        _gmlp_kernel,
        grid=(r // tm,),
        in_specs=[pl.BlockSpec((tm, 512), lambda i: (i, 0)),
                  pl.BlockSpec(ws.shape, lambda i: (0, 0, 0)),
                  pl.BlockSpec(bs_t.shape, c2), pl.BlockSpec((1, 256), c2), pl.BlockSpec((1, 256), c2)],
        out_specs=pl.BlockSpec((tm, 256), lambda i: (i, 0)),
        out_shape=jax.ShapeDtypeStruct((r, 256), F32),
        compiler_params=_cparams(1),
        name="gmlp",
    )(pc2d, ws, bs_t, nw, nb)


def _filter_feats(n):
    t = np.linspace(0.0, 1.0, n, dtype=np.float32)[:, None]
    bands = np.linspace(1e-4, B_BANDS - 1, B_BANDS, dtype=np.float32)[None]
    ang = (np.float32(2 * math.pi) * bands * np.arange(n, dtype=np.float32)[:, None] / np.float32(n)).astype(np.float32)
    z = np.concatenate([t, np.cos(ang), -np.sin(ang)], axis=-1).astype(np.float32)
    deltas = np.abs(np.linspace(math.log(B_DECAY_TARGET) / B_SLOW_DECAY,
                                math.log(B_DECAY_TARGET) / B_FAST_DECAY, 256, dtype=np.float32))
    neg_t_deltas = (-t * deltas[None]).astype(np.float32)
    return z, neg_t_deltas


def _filter_kernel(z_ref, ntd_ref, w1_ref, b1_ref, f1_ref, w2_ref, b2_ref, f2_ref, w3_ref, o_ref):
    hd = jnp.sin(f1_ref[...] * (_dot_f32(z_ref[...], w1_ref[...]) + b1_ref[...]))
    hd = jnp.sin(f2_ref[...] * (_dot_f32(hd, w2_ref[...]) + b2_ref[...]))
    hk = _dot_f32(hd, w3_ref[...])
    decay = jnp.exp(ntd_ref[...])
    o_ref[...] = hk * jnp.concatenate([decay, decay], axis=1)


def _filter_call(n, w1, b1, f1, w2, b2, f2, w3):
    z, ntd = _filter_feats(n)
    k_pad = 32
    z = np.pad(z, ((0, 0), (0, k_pad - z.shape[1])))
    w1 = jnp.pad(w1, ((0, k_pad - w1.shape[0]), (0, 0)))
    args = (jnp.asarray(z), jnp.asarray(ntd), w1, b1, f1, w2, b2, f2, w3)
    return pl.pallas_call(
        _filter_kernel,
        grid=(1,),
        in_specs=[pl.BlockSpec(a.shape, lambda i: (0, 0)) for a in args],
        out_specs=pl.BlockSpec((n, 512), lambda i: (0, 0)),
        out_shape=jax.ShapeDtypeStruct((n, 512), F32),
        compiler_params=_cparams(1),
        name="hyena_filter",
    )(*args)


def _short_conv_kernel(pb_ref, w_ref, b_ref, x0_ref, u_ref):
    x = pb_ref[0]
    n = x.shape[0]
    row = lax.broadcasted_iota(jnp.int32, x.shape, 0)
    prev = jnp.where(row == 0, 0.0, pltpu.roll(x, 1, axis=0))
    nxt = jnp.where(row == n - 1, 0.0, pltpu.roll(x, n - 1, axis=0))
    y = prev * w_ref[0:1, :] + x * w_ref[1:2, :] + nxt * w_ref[2:3, :] + b_ref[...]
    x0_ref[0] = y[:, 0:256]
    u_ref[0] = y[:, 256:512] * y[:, 512:768]


def _short_conv_call(pb, w, b):
    bsz, n, c3 = pb.shape
    return pl.pallas_call(
        _short_conv_kernel,
        grid=(bsz,),
        in_specs=[pl.BlockSpec((1, n, c3), lambda i: (i, 0, 0)),
                  pl.BlockSpec(w.shape, lambda i: (0, 0)), pl.BlockSpec(b.shape, lambda i: (0, 0))],
        out_specs=[pl.BlockSpec((1, n, 256), lambda i: (i, 0, 0))] * 2,
        out_shape=[jax.ShapeDtypeStruct((bsz, n, 256), F32)] * 2,
        compiler_params=_cparams(1),
        name="hyena_short_conv",
    )(pb, w, b)


def _long_conv_kernel(u_ref, k_ref, o_ref, *, n_blk, bsz, ch_per_step):
    T = CONV_BLOCK
    for c in range(ch_per_step):
        acc = [None] * n_blk
        for d in range(-(n_blk - 1), n_blk):
            seg = k_ref[c, :, (d + n_blk - 1) * T:(d + n_blk + 1) * T]
            rolled = pltpu.roll(jnp.broadcast_to(seg, (T, 2 * T)), 0, axis=1, stride=1, stride_axis=0)
            tile = rolled[:, T:2 * T].astype(BF16)
            j_lo, j_hi = max(0, -d), min(n_blk, n_blk - d)
            lhs = u_ref[c, j_lo * bsz:j_hi * bsz, :].astype(BF16)
            res = jnp.dot(lhs, tile, preferred_element_type=F32)
            for j in range(j_lo, j_hi):
                part = res[(j - j_lo) * bsz:(j - j_lo + 1) * bsz]
                i = j + d
                acc[i] = part if acc[i] is None else acc[i] + part
        o_ref[c] = jnp.concatenate(acc, axis=0)


def _long_conv_call(u_t, k_t, n_blk, bsz, ch_per_step):
    ch, rows, T = u_t.shape
    return pl.pallas_call(
        functools.partial(_long_conv_kernel, n_blk=n_blk, bsz=bsz, ch_per_step=ch_per_step),
        grid=(ch // ch_per_step,),
        in_specs=[pl.BlockSpec((ch_per_step, rows, T), lambda i: (i, 0, 0)),
                  pl.BlockSpec((ch_per_step, 1, k_t.shape[-1]), lambda i: (i, 0, 0))],
        out_specs=pl.BlockSpec((ch_per_step, rows, T), lambda i: (i, 0, 0)),
        out_shape=jax.ShapeDtypeStruct((ch, rows, T), F32),
        compiler_params=_cparams(1),
        name="hyena_long_conv",
    )(u_t, k_t)


def _hyena_conv(u, filt):
    bsz, n, ch = u.shape
    T = CONV_BLOCK
    n_blk = n // T
    k_full = jnp.concatenate([jnp.zeros((1, ch), F32), filt[:0:-1, 256:512], filt[:, 0:256]], axis=0)
    k_t = k_full.T.reshape(ch, 1, 2 * n)
    u_t = u.reshape(bsz, n_blk, T, ch).transpose(3, 1, 0, 2).reshape(ch, n_blk * bsz, T)
    y_t = _long_conv_call(u_t, k_t, n_blk, bsz, 4)
    return y_t.reshape(ch, n_blk, bsz, T).transpose(2, 1, 3, 0).reshape(bsz, n, ch)


def _merge_kernel(x_ref, mod_ref, nw_ref, ya_ref, x0_ref, u_ref, yconv_ref, yc_ref, yd_ref, hyb_ref,
                  wg_ref, wb_ref, wo_ref, o_ref, *, d):
    x = x_ref[...]
    h = _norm_mod(x, nw_ref[...], mod_ref[0, :, 0:d], mod_ref[0, :, d:2 * d]).astype(BF16)
    y_b = x0_ref[...] * (yconv_ref[...] + u_ref[...] * hyb_ref[...])
    ys = (ya_ref[...], y_b, yc_ref[...], yd_ref[...])
    acc = None
    for j in range(N_BRANCH):
        gate = _sigmoid(jnp.dot(h, wg_ref[j], preferred_element_type=F32))
        term = gate * jnp.dot(ys[j].astype(BF16), wb_ref[j], preferred_element_type=F32)
        acc = term if acc is None else acc + term
    y = jnp.dot(acc.astype(BF16), wo_ref[...], preferred_element_type=F32)
    o_ref[...] = x + mod_ref[0, :, 2 * d:3 * d] * y


def _merge_call(x2d, mod, nw, ya, x0, u, yconv, yc, yd, hyb, wg, wb, wo, rows_per_mod, tm):
    r, d = x2d.shape
    blocks_per_mod = rows_per_mod // tm
    c2 = lambda i: (0, 0)
    c3 = lambda i: (0, 0, 0)
    row = lambda w: pl.BlockSpec((tm, w), lambda i: (i, 0))
    return pl.pallas_call(
        functools.partial(_merge_kernel, d=d),
        grid=(r // tm,),
        in_specs=[row(d), pl.BlockSpec((1, 1, mod.shape[-1]), lambda i: (i // blocks_per_mod, 0, 0)),
                  pl.BlockSpec((1, d), c2), row(256), row(256), row(256), row(256), row(256), row(256),
                  pl.BlockSpec((1, 256), c2),
                  pl.BlockSpec(wg.shape, c3), pl.BlockSpec(wb.shape, c3), pl.BlockSpec(wo.shape, c2)],
        out_specs=row(d),
        out_shape=jax.ShapeDtypeStruct((r, d), F32),
        compiler_params=_cparams(1),
        name="merge",
    )(x2d, mod, nw, ya, x0, u, yconv, yc, yd, hyb, wg, wb, wo)


def _ffn_kernel(x_ref, mod_ref, nw_ref, wa_ref, wg_ref, wd_ref, fnw_ref, o_ref, *, d, hid_chunk, final_norm):
    x = x_ref[...]
    h = _norm_mod(x, nw_ref[...], mod_ref[0, :, 3 * d:4 * d], mod_ref[0, :, 4 * d:5 * d]).astype(BF16)
    hidden = wa_ref.shape[1]
    acc = None
    for c in range(hidden // hid_chunk):
        cs = slice(c * hid_chunk, (c + 1) * hid_chunk)
        a = jnp.dot(h, wa_ref[:, cs], preferred_element_type=F32)
        g = jnp.dot(h, wg_ref[:, cs], preferred_element_type=F32)
        term = jnp.dot((_silu(a) * g).astype(BF16), wd_ref[cs, :], preferred_element_type=F32)
        acc = term if acc is None else acc + term
    y = x + mod_ref[0, :, 5 * d:6 * d] * acc
    if final_norm:
        y = y * lax.rsqrt(jnp.mean(y * y, axis=-1, keepdims=True) + NORM_EPS) * fnw_ref[...]
    o_ref[...] = y


def _ffn_call(x2d, mod, nw, wa, wg, wd, fnw, rows_per_mod, tm, final_norm):
    r, d = x2d.shape
    blocks_per_mod = rows_per_mod // tm
    c2 = lambda i: (0, 0)
    return pl.pallas_call(
        functools.partial(_ffn_kernel, d=d, hid_chunk=256, final_norm=final_norm),
        grid=(r // tm,),
        in_specs=[pl.BlockSpec((tm, d), lambda i: (i, 0)),
                  pl.BlockSpec((1, 1, mod.shape[-1]), lambda i: (i // blocks_per_mod, 0, 0)),
                  pl.BlockSpec((1, d), c2),
                  pl.BlockSpec(wa.shape, c2), pl.BlockSpec(wg.shape, c2), pl.BlockSpec(wd.shape, c2),
                  pl.BlockSpec((1, d), c2)],
        out_specs=pl.BlockSpec((tm, d), lambda i: (i, 0)),
        out_shape=jax.ShapeDtypeStruct((r, d), F32),
        compiler_params=_cparams(1),
        name="ffn",
    )(x2d, mod, nw, wa, wg, wd, fnw)


def _grid_transpose(x, rows, cols):
    bsz, n, d = x.shape
    return x.reshape(bsz, rows, cols, d).swapaxes(1, 2).reshape(bsz, n, d)


def _row_tile(rows):
    return 512 if rows % 512 == 0 else 256


def kernel(x, c, ctx, c_ctx, ada_w, ada_b, norm1_w, norm2_w, w_in, b_in, mlstm_norm_w, hy_short_w, hy_short_b, hy_w1, hy_b1, hy_freq1, hy_w2, hy_b2, hy_freq2, hy_w3, hy_bias, gm_norm_w, gm_norm_b, gm_ws, gm_bs, hg_lb_logits, hg_norm_w, w_gate, w_branch, w_out, w_ffn_in, w_ffn_out, final_norm_w):
    bsz, n, d = x.shape
    nc = ctx.shape[1]
    depth = ada_w.shape[0]
    rows = n // GRID_W
    mw = d // N_BRANCH
    hidden = w_ffn_out.shape[1]
    assert mw == 256 and mw == HEADS * HEAD_DIM

    mb = -(-(bsz + 1) // 8) * 8
    cvec = jnp.concatenate([c, c_ctx[None], jnp.zeros((mb - bsz - 1, d), F32)], axis=0)
    mods = _ada_call(cvec, ada_w, ada_b)

    s_a, s_g, s_b, s_c = 4 * mw, 4 * mw + 4 * HEADS, 7 * mw + 4 * HEADS, 9 * mw + 4 * HEADS
    gate_pad = LANE - 2 * HEADS

    xc = ctx
    for l in range(depth):
        need_ctx = l < depth - 1
        col_major = l % 2 == 1
        if col_major:
            x = _grid_transpose(x, rows, GRID_W)
        mod_lat = mods[l, :bsz].reshape(bsz, 1, 6 * d)
        mod_ctx = mods[l, bsz:bsz + 1].reshape(1, 1, 6 * d)
        wl, bl = w_in[l], b_in[l]
        wgt, bgt = wl[:, s_a:s_g].reshape(d, 4, HEADS), bl[s_a:s_g].reshape(4, HEADS)
        zw, zb = jnp.zeros((d, gate_pad), F32), jnp.zeros((gate_pad,), F32)
        w_segs = [jnp.concatenate([wl[:, :s_a], wgt[:, 0], wgt[:, 2], zw, wgt[:, 1], wgt[:, 3], zw], axis=1),
                  wl[:, s_g:s_b], wl[:, s_b:s_c], wl[:, s_c:]]
        b_segs = [jnp.concatenate([bl[:s_a], bgt[0], bgt[2], zb, bgt[1], bgt[3], zb]),
                  bl[s_g:s_b], bl[s_b:s_c], bl[s_c:]]
        w_segs = [w.astype(BF16) for w in w_segs]
        b_segs = [b.reshape(1, -1) for b in b_segs]
        nw1 = norm1_w[l].reshape(1, d)

        x2d = x.reshape(bsz * n, d)
        xc2d = xc.reshape(bsz * nc, d)
        pa, pb, pc, pd = _in_call(x2d, mod_lat, nw1, w_segs, b_segs, n, _row_tile(n))
        if need_ctx:
            ca, cb, cc, cd = _in_call(xc2d, mod_ctx, nw1, w_segs, b_segs, bsz * nc, _row_tile(bsz * nc))
        else:
            ca, cd = _in_call(xc2d, mod_ctx, nw1, [w_segs[0], w_segs[3]], [b_segs[0], b_segs[3]],
                              bsz * nc, _row_tile(bsz * nc))

        yac, ya = _mlstm_call(ca.reshape(bsz, nc, -1), pa.reshape(bsz, n, -1),
                              mlstm_norm_w[l].reshape(1, mw), need_ctx)
        ydc, yd = _hgrn_call(cd.reshape(bsz, nc, -1), pd.reshape(bsz, n, -1), hg_lb_logits,
                             hg_norm_w[l].reshape(1, mw), l, need_ctx)

        hy_args = (hy_w1[l], hy_b1[l].reshape(1, -1), hy_freq1[l].reshape(1, -1), hy_w2[l],
                   hy_b2[l].reshape(1, -1), hy_freq2[l].reshape(1, -1), hy_w3[l])
        sw, sb = hy_short_w[l], hy_short_b[l].reshape(1, -1)
        gws, gbs_t = gm_ws[l], gm_bs[l].T
        gnw, gnb = gm_norm_w[l].reshape(1, mw), gm_norm_b[l].reshape(1, mw)
        hyb = hy_bias[l].reshape(1, mw)
        wg, wb, wo = w_gate[l].astype(BF16), w_branch[l].astype(BF16), w_out[l].astype(BF16)
        wfa, wfg = w_ffn_in[l][:, :hidden].astype(BF16), w_ffn_in[l][:, hidden:].astype(BF16)
        wfd = w_ffn_out[l].astype(BF16)
        nw2 = norm2_w[l].reshape(1, d)
        fnw = final_norm_w.reshape(1, d)

        x0, u = _short_conv_call(pb.reshape(bsz, n, -1), sw, sb)
        yconv = _hyena_conv(u, _filter_call(n, *hy_args))
        ycm = _gmlp_call(pc, gws, gbs_t, gnw, gnb, _row_tile(n))
        x2d = _merge_call(x2d, mod_lat, nw1, ya.reshape(bsz * n, mw), x0.reshape(bsz * n, mw),
                          u.reshape(bsz * n, mw), yconv.reshape(bsz * n, mw), ycm, yd.reshape(bsz * n, mw),
                          hyb, wg, wb, wo, n, _row_tile(n))
        x2d = _ffn_call(x2d, mod_lat, nw2, wfa, wfg, wfd, fnw, n, _row_tile(n), l == depth - 1)
        x = x2d.reshape(bsz, n, d)

        if need_ctx:
            tmc = _row_tile(bsz * nc)
            x0c, uc = _short_conv_call(cb.reshape(bsz, nc, -1), sw, sb)
            yconv_c = _hyena_conv(uc, _filter_call(nc, *hy_args))
            ycm_c = _gmlp_call(cc, gws, gbs_t, gnw, gnb, tmc)
            xc2d = _merge_call(xc2d, mod_ctx, nw1, yac.reshape(bsz * nc, mw), x0c.reshape(bsz * nc, mw),
                               uc.reshape(bsz * nc, mw), yconv_c.reshape(bsz * nc, mw), ycm_c,
                               ydc.reshape(bsz * nc, mw), hyb, wg, wb, wo, bsz * nc, tmc)
            xc2d = _ffn_call(xc2d, mod_ctx, nw2, wfa, wfg, wfd, fnw, bsz * nc, tmc, False)
            xc = xc2d.reshape(bsz, nc, d)
        if col_major:
            x = _grid_transpose(x, GRID_W, rows)
    return x
```

```python
import functools
import math

import numpy as np
import jax
import jax.numpy as jnp
from jax import lax
from jax.experimental import pallas as pl
from jax.experimental.pallas import tpu as pltpu

F32 = jnp.float32
BF16 = jnp.bfloat16

GRID_W = 64
NORM_EPS = 1e-6
N_BRANCH = 4
NEG_BIG = -1e30
F_EPS = 1e-30
HEAD_DIM = 64
HEADS = 4
LANE = 128
A_CHUNK = 128
D_CHUNK = 128
C_CHUNK = 128
C_GROUPS = 4
B_SHORT = 3
B_BANDS = 8
B_DECAY_TARGET = 1e-2
B_FAST_DECAY = 0.3
B_SLOW_DECAY = 1.5
CONV_BLOCK = 256
VMEM_LIMIT = 56 * 1024 * 1024


def _cparams(n_axes):
    return pltpu.CompilerParams(dimension_semantics=("arbitrary",) * n_axes,
                                vmem_limit_bytes=VMEM_LIMIT)


def _dot(a, b):
    return jnp.dot(a.astype(BF16), b.astype(BF16), preferred_element_type=F32)


def _split3(x):
    hi = x.astype(BF16)
    r1 = x - hi.astype(F32)
    mid = r1.astype(BF16)
    lo = (r1 - mid.astype(F32)).astype(BF16)
    return hi, mid, lo


def _dot_f32(a, b):
    a_hi, a_mid, a_lo = _split3(a)
    b_hi, b_mid, b_lo = _split3(b)
    d = lambda u, v: jnp.dot(u, v, preferred_element_type=F32)
    return (d(a_hi, b_hi) + (d(a_hi, b_mid) + d(a_mid, b_hi))
            + (d(a_hi, b_lo) + d(a_mid, b_mid) + d(a_lo, b_hi)))


def _sigmoid(x):
    return 1.0 / (1.0 + jnp.exp(-x))


def _sigmoid_both(x):
    e = jnp.exp(-jnp.abs(x))
    big = 1.0 / (1.0 + e)
    small = e * big
    pos = x >= 0.0
    return jnp.where(pos, big, small), jnp.where(pos, small, big)


def _silu(x):
    return x * _sigmoid(x)


def _log_sigmoid(x):
    return jnp.minimum(x, 0.0) - jnp.log(1.0 + jnp.exp(-jnp.abs(x)))


def _norm_mod(x, nw, shift, scale):
    ms = jnp.mean(x * x, axis=-1, keepdims=True)
    y = x * lax.rsqrt(ms + NORM_EPS) * nw
    return y * (1.0 + scale) + shift


def _ada_kernel(c_ref, w_ref, b_ref, o_ref):
    o_ref[0] = _dot_f32(_silu(c_ref[...]), w_ref[0]) + b_ref[0]


def _ada_call(cvec, ada_w, ada_b):
    depth, d, n6 = ada_w.shape
    mb = cvec.shape[0]
    tn = 512
    return pl.pallas_call(
        _ada_kernel,
        grid=(depth, n6 // tn),
        in_specs=[pl.BlockSpec((mb, d), lambda l, j: (0, 0)),
                  pl.BlockSpec((1, d, tn), lambda l, j: (l, 0, j)),
                  pl.BlockSpec((1, 1, tn), lambda l, j: (l, 0, j))],
        out_specs=pl.BlockSpec((1, mb, tn), lambda l, j: (l, 0, j)),
        out_shape=jax.ShapeDtypeStruct((depth, mb, n6), F32),
        compiler_params=_cparams(2),
        name="ada_mod",
    )(cvec, ada_w, ada_b.reshape(depth, 1, n6))


def _in_kernel(*refs, d, n_seg):
    x_ref, mod_ref, nw_ref = refs[:3]
    w_refs = refs[3:3 + n_seg]
    b_refs = refs[3 + n_seg:3 + 2 * n_seg]
    o_refs = refs[3 + 2 * n_seg:]
    h = _norm_mod(x_ref[...], nw_ref[...], mod_ref[0, :, 0:d], mod_ref[0, :, d:2 * d]).astype(BF16)
    for w_ref, b_ref, o_ref in zip(w_refs, b_refs, o_refs):
        o_ref[...] = jnp.dot(h, w_ref[...], preferred_element_type=F32) + b_ref[...]


def _in_call(x2d, mod, nw, ws, bs, rows_per_mod, tm):
    r, d = x2d.shape
    n_seg = len(ws)
    blocks_per_mod = rows_per_mod // tm
    const = lambda i: (0, 0)
    in_specs = [pl.BlockSpec((tm, d), lambda i: (i, 0)),
                pl.BlockSpec((1, 1, mod.shape[-1]), lambda i: (i // blocks_per_mod, 0, 0)),
                pl.BlockSpec((1, d), const)]
    in_specs += [pl.BlockSpec(w.shape, const) for w in ws]
    in_specs += [pl.BlockSpec(b.shape, const) for b in bs]
    return pl.pallas_call(
        functools.partial(_in_kernel, d=d, n_seg=n_seg),
        grid=(r // tm,),
        in_specs=in_specs,
        out_specs=[pl.BlockSpec((tm, w.shape[1]), lambda i: (i, 0)) for w in ws],
        out_shape=[jax.ShapeDtypeStruct((r, w.shape[1]), F32) for w in ws],
        compiler_params=_cparams(1),
        name="in_proj",
    )(x2d, mod, nw, *ws, *bs)


def _mlstm_ones(L):
    ones_blk = np.zeros((2 * L, LANE), np.float32)
    ones_blk[:L, :HEAD_DIM] = 1.0
    ones_blk[L:, HEAD_DIM:] = 1.0
    return ones_blk


def _running(op, fill, x, row, reverse):
    L = x.shape[0]
    sh = 1
    while sh < L:
        if reverse:
            shifted, valid = pltpu.roll(x, L - sh, axis=0), row < L - sh
        else:
            shifted, valid = pltpu.roll(x, sh, axis=0), row >= sh
        x = op(x, jnp.where(valid, shifted, fill))
        sh *= 2
    return x


def _mlstm_chunk(blk, ones_ref, cn_ref, m_ref, d, mask, row, lane_lo, sub_lo, bd):
    L = blk.shape[0]
    reverse = d == 1
    end = 0 if reverse else L - 1
    cum = _running(jnp.add, 0.0, _log_sigmoid(blk[:, 1152:1280]), row, reverse)
    a = blk[:, 1024:1152] - cum
    m_prev = m_ref[d:d + 1, :]
    g = jnp.maximum(_running(jnp.maximum, NEG_BIG, a, row, reverse), m_prev)
    g_end = g[end:end + 1, :]
    w_inter = jnp.exp(m_prev - g)
    e_negm = jnp.exp(-(cum + g))
    m_ref[d:d + 1, :] = cum[end:end + 1, :] + g_end
    a_t = a.T

    def head_lanes(x, pair):
        c = 4 * d + 2 * pair
        return jnp.where(lane_lo, jnp.broadcast_to(x[:, c:c + 1], (L, LANE)),
                         jnp.broadcast_to(x[:, c + 1:c + 2], (L, LANE)))

    outs = []
    for pair in range(HEADS // 2):
        cs = slice(128 * pair, 128 * pair + 128)
        q_pair = blk[:, cs].astype(BF16)
        k_pair = blk[:, 256 + 128 * pair:256 + 128 * pair + 128] * (HEAD_DIM ** -0.5)
        v_pair = blk[:, 512 + 128 * pair:512 + 128 * pair + 128]
        k_b = k_pair.astype(BF16)
        zero = jnp.zeros_like(k_b)
        k_rows = jnp.concatenate([jnp.where(lane_lo, k_b, zero), jnp.where(lane_lo, zero, k_b)], axis=0)
        s_pair = lax.dot_general(q_pair, k_rows, (((1,), (1,)), ((), ())), preferred_element_type=F32)
        w_halves = []
        for sub in range(2):
            hd = 2 * pair + sub
            c = 4 * d + hd
            expo = jnp.where(mask, a_t[c:c + 1, :] - jnp.broadcast_to(g[:, c:c + 1], (L, L)), NEG_BIG)
            w_halves.append((jnp.exp(expo) * s_pair[:, sub * L:(sub + 1) * L]).astype(BF16))
        w_pair = jnp.concatenate(w_halves, axis=1)
        v_b = v_pair.astype(BF16)
        v_rows = jnp.concatenate([jnp.where(lane_lo, v_b, zero), jnp.where(lane_lo, zero, v_b)], axis=0)
        intra = jnp.dot(w_pair, jnp.concatenate([v_rows, ones_ref[...]], axis=1), preferred_element_type=F32)
        slot = 2 * d + pair
        cn = cn_ref[slot]
        inter = jnp.dot(q_pair, cn.astype(BF16), preferred_element_type=F32)
        wi_p = head_lanes(w_inter, pair)
        num = wi_p * inter[:, 0:128] + intra[:, 0:128]
        den = wi_p * inter[:, 128:256] + intra[:, 128:256]
        outs.append(num / jnp.maximum(jnp.abs(den), head_lanes(e_negm, pair)))
        c_e = 4 * d + 2 * pair
        w_st = jnp.exp(jnp.where(sub_lo, a_t[c_e:c_e + 1, :] - g_end[:, c_e:c_e + 1],
                                 a_t[c_e + 1:c_e + 2, :] - g_end[:, c_e + 1:c_e + 2]))
        ktw = (k_pair.T * w_st).astype(BF16)
        upd = jnp.dot(ktw, jnp.concatenate([v_b, jnp.ones_like(v_b)], axis=1), preferred_element_type=F32)
        decay = wi_p[end:end + 1, :]
        cn_ref[slot] = jnp.concatenate([decay, decay], axis=1) * cn + jnp.where(bd, upd, 0.0)
    return jnp.concatenate(outs, axis=1)


def _mlstm_kernel(*refs, need_ctx):
    ctx_ref, lat_ref, ones_ref, nw_ref = refs[:4]
    if need_ctx:
        yc_ref, yl_ref, ybc_ref, ybl_ref, cn_ref, m_ref = refs[4:]
    else:
        yl_ref, ybc_ref, ybl_ref, cn_ref, m_ref = refs[4:]
        yc_ref = None
    L = A_CHUNK
    lane_lo = lax.broadcasted_iota(jnp.int32, (L, LANE), 1) < HEAD_DIM
    row = lax.broadcasted_iota(jnp.int32, (L, LANE), 0)
    sub_lo = row < HEAD_DIM
    r2 = lax.broadcasted_iota(jnp.int32, (L, L), 0)
    c2 = lax.broadcasted_iota(jnp.int32, (L, L), 1)
    mask_f, mask_b = r2 >= c2, r2 <= c2
    rb = lax.broadcasted_iota(jnp.int32, (LANE, 2 * LANE), 0) < HEAD_DIM
    cb = (lax.broadcasted_iota(jnp.int32, (LANE, 2 * LANE), 1) % LANE) < HEAD_DIM
    bd = rb == cb
    cn_ref[...] = jnp.zeros_like(cn_ref)
    m_ref[...] = jnp.zeros_like(m_ref)

    def scan(src_ref, yf_ref, yb_ref):
        n_chunks = src_ref.shape[1] // L

        def body(j, carry):
            sf = pl.multiple_of(j * L, L)
            sb = pl.multiple_of((n_chunks - 1 - j) * L, L)
            hf = _mlstm_chunk(src_ref[0, pl.ds(sf, L), :], ones_ref, cn_ref, m_ref, 0, mask_f, row, lane_lo, sub_lo, bd)
            hb = _mlstm_chunk(src_ref[0, pl.ds(sb, L), :], ones_ref, cn_ref, m_ref, 1, mask_b, row, lane_lo, sub_lo, bd)
            if yf_ref is not None:
                yf_ref[0, pl.ds(sf, L), :] = hf
                yb_ref[pl.ds(sb, L), :] = hb
            return carry

        lax.fori_loop(0, n_chunks, body, 0)

    def finalize(src_ref, y_ref, yb_ref):
        n_chunks = src_ref.shape[1] // L

        def body(j, carry):
            s = pl.multiple_of(j * L, L)
            for pair in range(HEADS // 2):
                cs = slice(128 * pair, 128 * pair + 128)
                y = y_ref[0, pl.ds(s, L), cs] + yb_ref[pl.ds(s, L), cs]
                y2 = y * y
                s_lo = jnp.sum(jnp.where(lane_lo, y2, 0.0), axis=-1, keepdims=True)
                s_all = jnp.sum(y2, axis=-1, keepdims=True)
                ms = jnp.where(lane_lo, s_lo, s_all - s_lo) * (1.0 / HEAD_DIM)
                o_gate = src_ref[0, pl.ds(s, L), 768 + 128 * pair:768 + 128 * pair + 128]
                y_ref[0, pl.ds(s, L), cs] = y * lax.rsqrt(ms + NORM_EPS) * nw_ref[:, cs] * _sigmoid(o_gate)
            return carry

        lax.fori_loop(0, n_chunks, body, 0)

    scan(ctx_ref, yc_ref, ybc_ref)
    scan(lat_ref, yl_ref, ybl_ref)
    if need_ctx:
        finalize(ctx_ref, yc_ref, ybc_ref)
    finalize(lat_ref, yl_ref, ybl_ref)


def _mlstm_call(pa_ctx, pa_lat, nw, need_ctx):
    b, nc, wa = pa_ctx.shape
    n = pa_lat.shape[1]
    L = A_CHUNK
    ones_blk = jnp.asarray(_mlstm_ones(L), BF16)
    c2 = lambda i: (0, 0)
    c3 = lambda i: (0, 0, 0)
    out_specs = [pl.BlockSpec((1, n, 256), lambda i: (i, 0, 0))]
    out_shape = [jax.ShapeDtypeStruct((b, n, 256), F32)]
    if need_ctx:
        out_specs.insert(0, pl.BlockSpec((1, nc, 256), lambda i: (i, 0, 0)))
        out_shape.insert(0, jax.ShapeDtypeStruct((b, nc, 256), F32))
    res = pl.pallas_call(
        functools.partial(_mlstm_kernel, need_ctx=need_ctx),
        grid=(b,),
        in_specs=[pl.BlockSpec((1, nc, wa), lambda i: (i, 0, 0)),
                  pl.BlockSpec((1, n, wa), lambda i: (i, 0, 0)),
                  pl.BlockSpec(ones_blk.shape, c2),
                  pl.BlockSpec((1, 256), c2)],
        out_specs=out_specs,
        out_shape=out_shape,
        scratch_shapes=[pltpu.VMEM((nc, 256), F32), pltpu.VMEM((n, 256), F32),
                        pltpu.VMEM((HEADS, LANE, 2 * LANE), F32), pltpu.VMEM((8, LANE), F32)],
        compiler_params=_cparams(1),
        name="mlstm",
    )(pa_ctx, pa_lat, ones_blk, nw)
    return (res[0], res[1]) if need_ctx else (None, res[0])


D_FAST_BASE = 32
D_FAST_MAX_EXPONENT = 60.0


def _hgrn_tables(L, base):
    t = np.arange(L)
    lev0 = int(math.log2(base))
    n_split = int(math.log2(L)) - lev0
    same_base = (t[:, None] // base) == (t[None, :] // base)
    lvl_f = np.where(same_base & (t[:, None] >= t[None, :]), 0, -1)
    lvl_b = np.where(same_base & (t[:, None] <= t[None, :]), 0, -1)
    mids_f, mids_b = [], []
    for i in range(n_split):
        half, size = base << i, base << (i + 1)
        same = (t[:, None] // size) == (t[None, :] // size)
        upper = (t // half) % 2 == 1
        lvl_f = np.where(same & upper[:, None] & ~upper[None, :], i + 1, lvl_f)
        lvl_b = np.where(same & ~upper[:, None] & upper[None, :], i + 1, lvl_b)
        starts = np.arange(0, L, size)
        mids_f.append([(int(s), size, int(s) + half - 1) for s in starts])
        mids_b.append([(int(s), size, int(s) + half) for s in starts])
    wide = lambda m: np.concatenate([m, m], axis=-1).astype(np.int32)
    return wide(lvl_f), wide(lvl_b), mids_f, mids_b


def _rows_of(b, spec):
    return jnp.concatenate([jnp.broadcast_to(b[r:r + 1, :], (n, b.shape[1])) for _, n, r in spec], axis=0)


def _hgrn_chunk(blk, f_pre, lb, lvl_ref, mids, s_ref, d, row, lane_lo, bd, base):
    L = blk.shape[0]
    reverse = d == 1
    end = 0 if reverse else L - 1
    qs = _silu(blk[:, 0:256])
    v = blk[:, 256:512]
    sig, sig_neg = _sigmoid_both(f_pre)
    log_f = jnp.log(jnp.maximum(lb + (1.0 - lb) * sig, F_EPS))
    k = (1.0 - lb) * sig_neg
    b = _running(jnp.add, 0.0, log_f, row, reverse)
    b_end = b[end:end + 1, :]
    q_in = qs * jnp.exp(b)
    k_out = k * jnp.exp(b_end - b)
    decay = jnp.exp(b_end)
    factors = []
    if base > 1:
        zero_row = jnp.zeros((1, b.shape[1]), F32)
        if reverse:
            refs = [b[s + base:s + base + 1, :] if s + base < L else zero_row for s in range(0, L, base)]
        else:
            refs = [b[s - 1:s, :] if s > 0 else zero_row for s in range(0, L, base)]
        r0 = jnp.concatenate([jnp.broadcast_to(r, (base, b.shape[1])) for r in refs], axis=0)
        factors.append((jnp.exp(b - r0), jnp.exp(r0 - b)))
    else:
        factors.append((None, None))
    for spec in mids:
        e = jnp.exp(-jnp.abs(b - _rows_of(b, spec)))
        factors.append((e, e))
    nt = (((1,), (1,)), ((), ()))
    lvl = lvl_ref[...]
    outs = []
    for pair in range(HEADS // 2):
        cs = slice(128 * pair, 128 * pair + 128)
        q_p, k_p = qs[:, cs], k[:, cs]
        zero = jnp.zeros((L, LANE), BF16)
        v_b = v[:, cs].astype(BF16)
        v_rows = jnp.concatenate([jnp.where(lane_lo, v_b, zero), jnp.where(lane_lo, zero, v_b)], axis=0)
        scores = jnp.zeros((L, 2 * L), F32)
        for i, (eq, ek) in enumerate(factors):
            q_l = (q_p if eq is None else q_p * eq[:, cs]).astype(BF16)
            k_l = (k_p if ek is None else k_p * ek[:, cs]).astype(BF16)
            k_rows = jnp.concatenate([jnp.where(lane_lo, k_l, zero), jnp.where(lane_lo, zero, k_l)], axis=0)
            scores = jnp.where(lvl == i, lax.dot_general(q_l, k_rows, nt, preferred_element_type=F32), scores)
        slot = 2 * d + pair
        s_t = s_ref[slot]
        outs.append(jnp.dot(scores.astype(BF16), v_rows, preferred_element_type=F32)
                    + lax.dot_general(q_in[:, cs].astype(BF16), s_t.astype(BF16), nt, preferred_element_type=F32))
        upd = lax.dot_general(v_b, k_out[:, cs].astype(BF16), (((0,), (0,)), ((), ())), preferred_element_type=F32)
        s_ref[slot] = decay[:, cs] * s_t + jnp.where(bd, upd, 0.0)
    return jnp.concatenate(outs, axis=1)


def _hgrn_kernel(*refs, need_ctx, layer, mids_fast, mids_safe):
    ctx_ref, lat_ref, lvlf_fast_ref, lvlb_fast_ref, lvlf_safe_ref, lvlb_safe_ref, lbl_ref, nw_ref = refs[:8]
    if need_ctx:
        yc_ref, yl_ref, ybc_ref, ybl_ref, s_ref = refs[8:]
    else:
        yl_ref, ybc_ref, ybl_ref, s_ref = refs[8:]
        yc_ref = None
    L = D_CHUNK
    lane_lo = lax.broadcasted_iota(jnp.int32, (L, LANE), 1) < HEAD_DIM
    row = lax.broadcasted_iota(jnp.int32, (L, 2 * LANE), 0)
    bd = ((lax.broadcasted_iota(jnp.int32, (LANE, LANE), 0) < HEAD_DIM)
          == (lax.broadcasted_iota(jnp.int32, (LANE, LANE), 1) < HEAD_DIM))
    logits = lbl_ref[...]
    e = jnp.exp(logits - jnp.max(logits, axis=0, keepdims=True))
    prob = e / jnp.sum(e, axis=0, keepdims=True)
    lb = jnp.sum(prob[0:layer + 1], axis=0, keepdims=True) - prob[0:1]
    lb2 = jnp.concatenate([lb, lb], axis=1)

    def block_decay(src_ref):
        def body(j, acc):
            s = pl.multiple_of(j * L, L)
            f_pre = src_ref[0, pl.ds(s, L), 512:1024]
            neg_log_f = -jnp.log(jnp.maximum(lb2 + (1.0 - lb2) * _sigmoid(f_pre), F_EPS))
            for blk in range(L // D_FAST_BASE):
                part = neg_log_f[blk * D_FAST_BASE:(blk + 1) * D_FAST_BASE]
                acc = jnp.maximum(acc, jnp.sum(part, axis=0, keepdims=True))
            return acc

        return lax.fori_loop(0, src_ref.shape[1] // L, body, jnp.zeros((1, 512), F32))

    fast_ok = jnp.max(jnp.maximum(block_decay(ctx_ref), block_decay(lat_ref))) <= D_FAST_MAX_EXPONENT

    def scan(src_ref, yf_ref, yb_ref, lvlf_ref, lvlb_ref, mids, base):
        n_chunks = src_ref.shape[1] // L

        def body(j, carry):
            sf = pl.multiple_of(j * L, L)
            sb = pl.multiple_of((n_chunks - 1 - j) * L, L)
            blk_f = src_ref[0, pl.ds(sf, L), :]
            blk_b = src_ref[0, pl.ds(sb, L), :]
            of = _hgrn_chunk(blk_f, blk_f[:, 512:768], lb, lvlf_ref, mids[0], s_ref, 0, row, lane_lo, bd, base)
            ob = _hgrn_chunk(blk_b, blk_b[:, 768:1024], lb, lvlb_ref, mids[1], s_ref, 1, row, lane_lo, bd, base)
            if yf_ref is not None:
                yf_ref[0, pl.ds(sf, L), :] = of
                yb_ref[pl.ds(sb, L), :] = ob
            return carry

        lax.fori_loop(0, n_chunks, body, 0)

    def scans(lvlf_ref, lvlb_ref, mids, base):
        s_ref[...] = jnp.zeros_like(s_ref)
        scan(ctx_ref, yc_ref, ybc_ref, lvlf_ref, lvlb_ref, mids, base)
        scan(lat_ref, yl_ref, ybl_ref, lvlf_ref, lvlb_ref, mids, base)

    @pl.when(fast_ok)
    def _():
        scans(lvlf_fast_ref, lvlb_fast_ref, mids_fast, D_FAST_BASE)

    @pl.when(jnp.logical_not(fast_ok))
    def _():
        scans(lvlf_safe_ref, lvlb_safe_ref, mids_safe, 1)

    def finalize(src_ref, y_ref, yb_ref):
        n_chunks = src_ref.shape[1] // L

        def body(j, carry):
            s = pl.multiple_of(j * L, L)
            for pair in range(HEADS // 2):
                cs = slice(128 * pair, 128 * pair + 128)
                y = y_ref[0, pl.ds(s, L), cs] + yb_ref[pl.ds(s, L), cs]
                y2 = y * y
                s_lo = jnp.sum(jnp.where(lane_lo, y2, 0.0), axis=-1, keepdims=True)
                s_all = jnp.sum(y2, axis=-1, keepdims=True)
                ms = jnp.where(lane_lo, s_lo, s_all - s_lo) * (1.0 / HEAD_DIM)
                g = src_ref[0, pl.ds(s, L), 1024 + 128 * pair:1024 + 128 * pair + 128]
                y_ref[0, pl.ds(s, L), cs] = y * lax.rsqrt(ms + NORM_EPS) * nw_ref[:, cs] * _silu(g)
            return carry

        lax.fori_loop(0, n_chunks, body, 0)

    if need_ctx:
        finalize(ctx_ref, yc_ref, ybc_ref)
    finalize(lat_ref, yl_ref, ybl_ref)


def _hgrn_call(pd_ctx, pd_lat, lb_logits, nw, layer, need_ctx):
    b, nc, wd = pd_ctx.shape
    n = pd_lat.shape[1]
    L = D_CHUNK
    lvlf_fast, lvlb_fast, midsf_fast, midsb_fast = _hgrn_tables(L, D_FAST_BASE)
    lvlf_safe, lvlb_safe, midsf_safe, midsb_safe = _hgrn_tables(L, 1)
    tables = [jnp.asarray(a) for a in (lvlf_fast, lvlb_fast, lvlf_safe, lvlb_safe)]
    c2 = lambda i: (0, 0)
    out_specs = [pl.BlockSpec((1, n, 256), lambda i: (i, 0, 0))]
    out_shape = [jax.ShapeDtypeStruct((b, n, 256), F32)]
    if need_ctx:
        out_specs.insert(0, pl.BlockSpec((1, nc, 256), lambda i: (i, 0, 0)))
        out_shape.insert(0, jax.ShapeDtypeStruct((b, nc, 256), F32))
    res = pl.pallas_call(
        functools.partial(_hgrn_kernel, need_ctx=need_ctx, layer=layer,
                          mids_fast=(midsf_fast, midsb_fast), mids_safe=(midsf_safe, midsb_safe)),
        grid=(b,),
        in_specs=[pl.BlockSpec((1, nc, wd), lambda i: (i, 0, 0)),
                  pl.BlockSpec((1, n, wd), lambda i: (i, 0, 0))]
                 + [pl.BlockSpec(a.shape, c2) for a in tables]
                 + [pl.BlockSpec(lb_logits.shape, c2), pl.BlockSpec((1, 256), c2)],
        out_specs=out_specs,
        out_shape=out_shape,
        scratch_shapes=[pltpu.VMEM((nc, 256), F32), pltpu.VMEM((n, 256), F32),
                        pltpu.VMEM((HEADS, LANE, LANE), F32)],
        compiler_params=_cparams(1),
        name="hgrn2",
    )(pd_ctx, pd_lat, *tables, lb_logits, nw)
    return (res[0], res[1]) if need_ctx else (None, res[0])


def _gmlp_kernel(pc_ref, ws_ref, bs_ref, nw_ref, nb_ref, o_ref):
    L = C_CHUNK
    gw = 256 // C_GROUPS
    n_chunks = pc_ref.shape[0] // L
    for c in range(n_chunks):
        x = pc_ref[c * L:(c + 1) * L, :]
        g = 0.5 * x * (1.0 + jnp.tanh(math.sqrt(2.0 / math.pi) * (x + 0.044715 * (x * x * x))))
        u, v = g[:, 0:256], g[:, 256:512]
        mu = jnp.mean(v, axis=-1, keepdims=True)
        vc = v - mu
        var = jnp.mean(vc * vc, axis=-1, keepdims=True)
        vn = vc * lax.rsqrt(var + NORM_EPS) * nw_ref[...] + nb_ref[...]
        mixed = [_dot(ws_ref[gi], vn[:, gi * gw:(gi + 1) * gw]) + bs_ref[:, gi:gi + 1] for gi in range(C_GROUPS)]
        o_ref[c * L:(c + 1) * L, :] = u * jnp.concatenate(mixed, axis=1)


def _gmlp_call(pc2d, ws, bs_t, nw, nb, tm):
    r = pc2d.shape[0]
    c2 = lambda i: (0, 0)
    return pl.pallas_call(
        _gmlp_kernel,
        grid=(r // tm,),
        in_specs=[pl.BlockSpec((tm, 512), lambda i: (i, 0)),
                  pl.BlockSpec(ws.shape, lambda i: (0, 0, 0)),
                  pl.BlockSpec(bs_t.shape, c2), pl.BlockSpec((1, 256), c2), pl.BlockSpec((1, 256), c2)],
        out_specs=pl.BlockSpec((tm, 256), lambda i: (i, 0)),
        out_shape=jax.ShapeDtypeStruct((r, 256), F32),
        compiler_params=_cparams(1),
        name="gmlp",
    )(pc2d, ws, bs_t, nw, nb)


def _filter_feats(n):
    t = np.linspace(0.0, 1.0, n, dtype=np.float32)[:, None]
    bands = np.linspace(1e-4, B_BANDS - 1, B_BANDS, dtype=np.float32)[None]
    ang = (np.float32(2 * math.pi) * bands * np.arange(n, dtype=np.float32)[:, None] / np.float32(n)).astype(np.float32)
    z = np.concatenate([t, np.cos(ang), -np.sin(ang)], axis=-1).astype(np.float32)
    deltas = np.abs(np.linspace(math.log(B_DECAY_TARGET) / B_SLOW_DECAY,
                                math.log(B_DECAY_TARGET) / B_FAST_DECAY, 256, dtype=np.float32))
    neg_t_deltas = (-t * deltas[None]).astype(np.float32)
    return z, neg_t_deltas


def _filter_kernel(z_ref, ntd_ref, w1_ref, b1_ref, f1_ref, w2_ref, b2_ref, f2_ref, w3_ref, o_ref):
    hd = jnp.sin(f1_ref[...] * (_dot_f32(z_ref[...], w1_ref[...]) + b1_ref[...]))
    hd = jnp.sin(f2_ref[...] * (_dot_f32(hd, w2_ref[...]) + b2_ref[...]))
    hk = _dot_f32(hd, w3_ref[...])
    decay = jnp.exp(ntd_ref[...])
    o_ref[...] = hk * jnp.concatenate([decay, decay], axis=1)


def _filter_call(n, w1, b1, f1, w2, b2, f2, w3):
    z, ntd = _filter_feats(n)
    k_pad = 32
    z = np.pad(z, ((0, 0), (0, k_pad - z.shape[1])))
    w1 = jnp.pad(w1, ((0, k_pad - w1.shape[0]), (0, 0)))
    args = (jnp.asarray(z), jnp.asarray(ntd), w1, b1, f1, w2, b2, f2, w3)
    return pl.pallas_call(
        _filter_kernel,
        grid=(1,),
        in_specs=[pl.BlockSpec(a.shape, lambda i: (0, 0)) for a in args],
        out_specs=pl.BlockSpec((n, 512), lambda i: (0, 0)),
        out_shape=jax.ShapeDtypeStruct((n, 512), F32),
        compiler_params=_cparams(1),
        name="hyena_filter",
    )(*args)


def _short_conv_kernel(pb_ref, w_ref, b_ref, x0_ref, u_ref):
    x = pb_ref[0]
    n = x.shape[0]
    row = lax.broadcasted_iota(jnp.int32, x.shape, 0)
    prev = jnp.where(row == 0, 0.0, pltpu.roll(x, 1, axis=0))
    nxt = jnp.where(row == n - 1, 0.0, pltpu.roll(x, n - 1, axis=0))
    y = prev * w_ref[0:1, :] + x * w_ref[1:2, :] + nxt * w_ref[2:3, :] + b_ref[...]
    x0_ref[0] = y[:, 0:256]
    u_ref[0] = y[:, 256:512] * y[:, 512:768]


def _short_conv_call(pb, w, b):
    bsz, n, c3 = pb.shape
    return pl.pallas_call(
        _short_conv_kernel,
        grid=(bsz,),
        in_specs=[pl.BlockSpec((1, n, c3), lambda i: (i, 0, 0)),
                  pl.BlockSpec(w.shape, lambda i: (0, 0)), pl.BlockSpec(b.shape, lambda i: (0, 0))],
        out_specs=[pl.BlockSpec((1, n, 256), lambda i: (i, 0, 0))] * 2,
        out_shape=[jax.ShapeDtypeStruct((bsz, n, 256), F32)] * 2,
        compiler_params=_cparams(1),
        name="hyena_short_conv",
    )(pb, w, b)


def _long_conv_kernel(u_ref, k_ref, o_ref, *, n_blk, bsz, ch_per_step):
    T = CONV_BLOCK
    half = T // 2
    for c in range(ch_per_step):
        taps = k_ref[c]
        rolled = pltpu.roll(jnp.broadcast_to(taps, (half, taps.shape[1])), 0, axis=1, stride=1, stride_axis=0)
        acc = [None] * n_blk
        for d in range(-(n_blk - 1), n_blk):
            base = (d + n_blk) * T
            tile = jnp.concatenate([rolled[:, base:base + T], rolled[:, base - half:base - half + T]],
                                   axis=0).astype(BF16)
            j_lo, j_hi = max(0, -d), min(n_blk, n_blk - d)
            lhs = u_ref[c, j_lo * bsz:j_hi * bsz, :].astype(BF16)
            res = jnp.dot(lhs, tile, preferred_element_type=F32)
            for j in range(j_lo, j_hi):
                part = res[(j - j_lo) * bsz:(j - j_lo + 1) * bsz]
                i = j + d
                acc[i] = part if acc[i] is None else acc[i] + part
        o_ref[c] = jnp.concatenate(acc, axis=0)


def _long_conv_call(u_t, k_t, n_blk, bsz, ch_per_step):
    ch, rows, T = u_t.shape
    return pl.pallas_call(
        functools.partial(_long_conv_kernel, n_blk=n_blk, bsz=bsz, ch_per_step=ch_per_step),
        grid=(ch // ch_per_step,),
        in_specs=[pl.BlockSpec((ch_per_step, rows, T), lambda i: (i, 0, 0)),
                  pl.BlockSpec((ch_per_step, 1, k_t.shape[-1]), lambda i: (i, 0, 0))],
        out_specs=pl.BlockSpec((ch_per_step, rows, T), lambda i: (i, 0, 0)),
        out_shape=jax.ShapeDtypeStruct((ch, rows, T), F32),
        compiler_params=_cparams(1),
        name="hyena_long_conv",
    )(u_t, k_t)


def _hyena_conv(u, filt):
    bsz, n, ch = u.shape
    T = CONV_BLOCK
    n_blk = n // T
    k_full = jnp.concatenate([jnp.zeros((1, ch), F32), filt[:0:-1, 256:512], filt[:, 0:256]], axis=0)
    k_t = k_full.T.reshape(ch, 1, 2 * n)
    u_t = u.reshape(bsz, n_blk, T, ch).transpose(3, 1, 0, 2).reshape(ch, n_blk * bsz, T)
    y_t = _long_conv_call(u_t, k_t, n_blk, bsz, 4)
    return y_t.reshape(ch, n_blk, bsz, T).transpose(2, 1, 3, 0).reshape(bsz, n, ch)


def _merge_kernel(x_ref, mod_ref, nw_ref, ya_ref, x0_ref, u_ref, yconv_ref, yc_ref, yd_ref, hyb_ref,
                  wg_ref, wb_ref, wo_ref, o_ref, *, d):
    x = x_ref[...]
    h = _norm_mod(x, nw_ref[...], mod_ref[0, :, 0:d], mod_ref[0, :, d:2 * d]).astype(BF16)
    y_b = x0_ref[...] * (yconv_ref[...] + u_ref[...] * hyb_ref[...])
    ys = (ya_ref[...], y_b, yc_ref[...], yd_ref[...])
    acc = None
    for j in range(N_BRANCH):
        gate = _sigmoid(jnp.dot(h, wg_ref[j], preferred_element_type=F32))
        term = gate * jnp.dot(ys[j].astype(BF16), wb_ref[j], preferred_element_type=F32)
        acc = term if acc is None else acc + term
    y = jnp.dot(acc.astype(BF16), wo_ref[...], preferred_element_type=F32)
    o_ref[...] = x + mod_ref[0, :, 2 * d:3 * d] * y


def _merge_call(x2d, mod, nw, ya, x0, u, yconv, yc, yd, hyb, wg, wb, wo, rows_per_mod, tm):
    r, d = x2d.shape
    blocks_per_mod = rows_per_mod // tm
    c2 = lambda i: (0, 0)
    c3 = lambda i: (0, 0, 0)
    row = lambda w: pl.BlockSpec((tm, w), lambda i: (i, 0))
    return pl.pallas_call(
        functools.partial(_merge_kernel, d=d),
        grid=(r // tm,),
        in_specs=[row(d), pl.BlockSpec((1, 1, mod.shape[-1]), lambda i: (i // blocks_per_mod, 0, 0)),
                  pl.BlockSpec((1, d), c2), row(256), row(256), row(256), row(256), row(256), row(256),
                  pl.BlockSpec((1, 256), c2),
                  pl.BlockSpec(wg.shape, c3), pl.BlockSpec(wb.shape, c3), pl.BlockSpec(wo.shape, c2)],
        out_specs=row(d),
        out_shape=jax.ShapeDtypeStruct((r, d), F32),
        compiler_params=_cparams(1),
        name="merge",
    )(x2d, mod, nw, ya, x0, u, yconv, yc, yd, hyb, wg, wb, wo)


def _ffn_kernel(x_ref, mod_ref, nw_ref, wa_ref, wg_ref, wd_ref, fnw_ref, o_ref, *, d, hid_chunk, final_norm):
    x = x_ref[...]
    h = _norm_mod(x, nw_ref[...], mod_ref[0, :, 3 * d:4 * d], mod_ref[0, :, 4 * d:5 * d]).astype(BF16)
    hidden = wa_ref.shape[1]
    acc = None
    for c in range(hidden // hid_chunk):
        cs = slice(c * hid_chunk, (c + 1) * hid_chunk)
        a = jnp.dot(h, wa_ref[:, cs], preferred_element_type=F32)
        g = jnp.dot(h, wg_ref[:, cs], preferred_element_type=F32)
        term = jnp.dot((_silu(a) * g).astype(BF16), wd_ref[cs, :], preferred_element_type=F32)
        acc = term if acc is None else acc + term
    y = x + mod_ref[0, :, 5 * d:6 * d] * acc
    if final_norm:
        y = y * lax.rsqrt(jnp.mean(y * y, axis=-1, keepdims=True) + NORM_EPS) * fnw_ref[...]
    o_ref[...] = y


def _ffn_call(x2d, mod, nw, wa, wg, wd, fnw, rows_per_mod, tm, final_norm):
    r, d = x2d.shape
    blocks_per_mod = rows_per_mod // tm
    c2 = lambda i: (0, 0)
    return pl.pallas_call(
        functools.partial(_ffn_kernel, d=d, hid_chunk=256, final_norm=final_norm),
        grid=(r // tm,),
        in_specs=[pl.BlockSpec((tm, d), lambda i: (i, 0)),
                  pl.BlockSpec((1, 1, mod.shape[-1]), lambda i: (i // blocks_per_mod, 0, 0)),
                  pl.BlockSpec((1, d), c2),
                  pl.BlockSpec(wa.shape, c2), pl.BlockSpec(wg.shape, c2), pl.BlockSpec(wd.shape, c2),
                  pl.BlockSpec((1, d), c2)],
        out_specs=pl.BlockSpec((tm, d), lambda i: (i, 0)),
        out_shape=jax.ShapeDtypeStruct((r, d), F32),
        compiler_params=_cparams(1),
        name="ffn",
    )(x2d, mod, nw, wa, wg, wd, fnw)


def _grid_transpose(x, rows, cols):
    bsz, n, d = x.shape
    return x.reshape(bsz, rows, cols, d).swapaxes(1, 2).reshape(bsz, n, d)


def _row_tile(rows):
    return 512 if rows % 512 == 0 else 256


def kernel(x, c, ctx, c_ctx, ada_w, ada_b, norm1_w, norm2_w, w_in, b_in, mlstm_norm_w, hy_short_w, hy_short_b, hy_w1, hy_b1, hy_freq1, hy_w2, hy_b2, hy_freq2, hy_w3, hy_bias, gm_norm_w, gm_norm_b, gm_ws, gm_bs, hg_lb_logits, hg_norm_w, w_gate, w_branch, w_out, w_ffn_in, w_ffn_out, final_norm_w):
    bsz, n, d = x.shape
    nc = ctx.shape[1]
    depth = ada_w.shape[0]
    rows = n // GRID_W
    mw = d // N_BRANCH
    hidden = w_ffn_out.shape[1]
    assert mw == 256 and mw == HEADS * HEAD_DIM

    mb = -(-(bsz + 1) // 8) * 8
    cvec = jnp.concatenate([c, c_ctx[None], jnp.zeros((mb - bsz - 1, d), F32)], axis=0)
    mods = _ada_call(cvec, ada_w, ada_b)

    s_a, s_g, s_b, s_c = 4 * mw, 4 * mw + 4 * HEADS, 7 * mw + 4 * HEADS, 9 * mw + 4 * HEADS
    gate_pad = LANE - 2 * HEADS

    xc = ctx
    for l in range(depth):
        need_ctx = l < depth - 1
        col_major = l % 2 == 1
        if col_major:
            x = _grid_transpose(x, rows, GRID_W)
        mod_lat = mods[l, :bsz].reshape(bsz, 1, 6 * d)
        mod_ctx = mods[l, bsz:bsz + 1].reshape(1, 1, 6 * d)
        wl, bl = w_in[l], b_in[l]
        wgt, bgt = wl[:, s_a:s_g].reshape(d, 4, HEADS), bl[s_a:s_g].reshape(4, HEADS)
        zw, zb = jnp.zeros((d, gate_pad), F32), jnp.zeros((gate_pad,), F32)
        w_segs = [jnp.concatenate([wl[:, :s_a], wgt[:, 0], wgt[:, 2], zw, wgt[:, 1], wgt[:, 3], zw], axis=1),
                  wl[:, s_g:s_b], wl[:, s_b:s_c], wl[:, s_c:]]
        b_segs = [jnp.concatenate([bl[:s_a], bgt[0], bgt[2], zb, bgt[1], bgt[3], zb]),
                  bl[s_g:s_b], bl[s_b:s_c], bl[s_c:]]
        w_segs = [w.astype(BF16) for w in w_segs]
        b_segs = [b.reshape(1, -1) for b in b_segs]
        nw1 = norm1_w[l].reshape(1, d)

        x2d = x.reshape(bsz * n, d)
        xc2d = xc.reshape(bsz * nc, d)
        pa, pb, pc, pd = _in_call(x2d, mod_lat, nw1, w_segs, b_segs, n, _row_tile(n))
        if need_ctx:
            ca, cb, cc, cd = _in_call(xc2d, mod_ctx, nw1, w_segs, b_segs, bsz * nc, _row_tile(bsz * nc))
        else:
            ca, cd = _in_call(xc2d, mod_ctx, nw1, [w_segs[0], w_segs[3]], [b_segs[0], b_segs[3]],
                              bsz * nc, _row_tile(bsz * nc))

        yac, ya = _mlstm_call(ca.reshape(bsz, nc, -1), pa.reshape(bsz, n, -1),
                              mlstm_norm_w[l].reshape(1, mw), need_ctx)
        ydc, yd = _hgrn_call(cd.reshape(bsz, nc, -1), pd.reshape(bsz, n, -1), hg_lb_logits,
                             hg_norm_w[l].reshape(1, mw), l, need_ctx)

        hy_args = (hy_w1[l], hy_b1[l].reshape(1, -1), hy_freq1[l].reshape(1, -1), hy_w2[l],
                   hy_b2[l].reshape(1, -1), hy_freq2[l].reshape(1, -1), hy_w3[l])
        sw, sb = hy_short_w[l], hy_short_b[l].reshape(1, -1)
        gws, gbs_t = gm_ws[l], gm_bs[l].T
        gnw, gnb = gm_norm_w[l].reshape(1, mw), gm_norm_b[l].reshape(1, mw)
        hyb = hy_bias[l].reshape(1, mw)
        wg, wb, wo = w_gate[l].astype(BF16), w_branch[l].astype(BF16), w_out[l].astype(BF16)
        wfa, wfg = w_ffn_in[l][:, :hidden].astype(BF16), w_ffn_in[l][:, hidden:].astype(BF16)
        wfd = w_ffn_out[l].astype(BF16)
        nw2 = norm2_w[l].reshape(1, d)
        fnw = final_norm_w.reshape(1, d)

        x0, u = _short_conv_call(pb.reshape(bsz, n, -1), sw, sb)
        yconv = _hyena_conv(u, _filter_call(n, *hy_args))
        ycm = _gmlp_call(pc, gws, gbs_t, gnw, gnb, _row_tile(n))
        x2d = _merge_call(x2d, mod_lat, nw1, ya.reshape(bsz * n, mw), x0.reshape(bsz * n, mw),
                          u.reshape(bsz * n, mw), yconv.reshape(bsz * n, mw), ycm, yd.reshape(bsz * n, mw),
                          hyb, wg, wb, wo, n, _row_tile(n))
        x2d = _ffn_call(x2d, mod_lat, nw2, wfa, wfg, wfd, fnw, n, _row_tile(n), l == depth - 1)
        x = x2d.reshape(bsz, n, d)

        if need_ctx:
            tmc = _row_tile(bsz * nc)
            x0c, uc = _short_conv_call(cb.reshape(bsz, nc, -1), sw, sb)
            yconv_c = _hyena_conv(uc, _filter_call(nc, *hy_args))
            ycm_c = _gmlp_call(cc, gws, gbs_t, gnw, gnb, tmc)
            xc2d = _merge_call(xc2d, mod_ctx, nw1, yac.reshape(bsz * nc, mw), x0c.reshape(bsz * nc, mw),
                               uc.reshape(bsz * nc, mw), yconv_c.reshape(bsz * nc, mw), ycm_c,
                               ydc.reshape(bsz * nc, mw), hyb, wg, wb, wo, bsz * nc, tmc)
            xc2d = _ffn_call(xc2d, mod_ctx, nw2, wfa, wfg, wfd, fnw, bsz * nc, tmc, False)
            xc = xc2d.reshape(bsz, nc, d)
        if col_major:
            x = _grid_transpose(x, GRID_W, rows)
    return x
```

```python
import functools
import math

import numpy as np
import jax
import jax.numpy as jnp
from jax import lax
from jax.experimental import pallas as pl
from jax.experimental.pallas import tpu as pltpu

F32 = jnp.float32
BF16 = jnp.bfloat16

GRID_W = 64
NORM_EPS = 1e-6
N_BRANCH = 4
NEG_BIG = -1e30
F_EPS = 1e-30
HEAD_DIM = 64
HEADS = 4
LANE = 128
A_CHUNK = 128
D_CHUNK = 128
C_CHUNK = 128
C_GROUPS = 4
B_SHORT = 3
B_BANDS = 8
B_DECAY_TARGET = 1e-2
B_FAST_DECAY = 0.3
B_SLOW_DECAY = 1.5
CONV_BLOCK = 256
VMEM_LIMIT = 56 * 1024 * 1024


def _cparams(n_axes):
    return pltpu.CompilerParams(dimension_semantics=("arbitrary",) * n_axes,
                                vmem_limit_bytes=VMEM_LIMIT)


def _dot(a, b):
    return jnp.dot(a.astype(BF16), b.astype(BF16), preferred_element_type=F32)


def _split3(x):
    hi = x.astype(BF16)
    r1 = x - hi.astype(F32)
    mid = r1.astype(BF16)
    lo = (r1 - mid.astype(F32)).astype(BF16)
    return hi, mid, lo


def _dot_f32(a, b):
    a_hi, a_mid, a_lo = _split3(a)
    b_hi, b_mid, b_lo = _split3(b)
    d = lambda u, v: jnp.dot(u, v, preferred_element_type=F32)
    return (d(a_hi, b_hi) + (d(a_hi, b_mid) + d(a_mid, b_hi))
            + (d(a_hi, b_lo) + d(a_mid, b_mid) + d(a_lo, b_hi)))


def _sigmoid(x):
    return 1.0 / (1.0 + jnp.exp(-x))


def _sigmoid_both(x):
    e = jnp.exp(-jnp.abs(x))
    big = 1.0 / (1.0 + e)
    small = e * big
    pos = x >= 0.0
    return jnp.where(pos, big, small), jnp.where(pos, small, big)


def _silu(x):
    return x * _sigmoid(x)


def _log_sigmoid(x):
    return jnp.minimum(x, 0.0) - jnp.log(1.0 + jnp.exp(-jnp.abs(x)))


def _norm_mod(x, nw, shift, scale):
    ms = jnp.mean(x * x, axis=-1, keepdims=True)
    y = x * lax.rsqrt(ms + NORM_EPS) * nw
    return y * (1.0 + scale) + shift


def _ada_kernel(c_ref, w_ref, b_ref, o_ref):
    o_ref[0] = _dot_f32(_silu(c_ref[...]), w_ref[0]) + b_ref[0]


def _ada_call(cvec, ada_w, ada_b):
    depth, d, n6 = ada_w.shape
    mb = cvec.shape[0]
    tn = 512
    return pl.pallas_call(
        _ada_kernel,
        grid=(depth, n6 // tn),
        in_specs=[pl.BlockSpec((mb, d), lambda l, j: (0, 0)),
                  pl.BlockSpec((1, d, tn), lambda l, j: (l, 0, j)),
                  pl.BlockSpec((1, 1, tn), lambda l, j: (l, 0, j))],
        out_specs=pl.BlockSpec((1, mb, tn), lambda l, j: (l, 0, j)),
        out_shape=jax.ShapeDtypeStruct((depth, mb, n6), F32),
        compiler_params=_cparams(2),
        name="ada_mod",
    )(cvec, ada_w, ada_b.reshape(depth, 1, n6))


def _in_kernel(*refs, d, n_seg):
    x_ref, mod_ref, nw_ref = refs[:3]
    w_refs = refs[3:3 + n_seg]
    b_refs = refs[3 + n_seg:3 + 2 * n_seg]
    o_refs = refs[3 + 2 * n_seg:]
    h = _norm_mod(x_ref[...], nw_ref[...], mod_ref[0, :, 0:d], mod_ref[0, :, d:2 * d]).astype(BF16)
    for w_ref, b_ref, o_ref in zip(w_refs, b_refs, o_refs):
        o_ref[...] = jnp.dot(h, w_ref[...], preferred_element_type=F32) + b_ref[...]


def _in_call(x2d, mod, nw, ws, bs, rows_per_mod, tm):
    r, d = x2d.shape
    n_seg = len(ws)
    blocks_per_mod = rows_per_mod // tm
    const = lambda i: (0, 0)
    in_specs = [pl.BlockSpec((tm, d), lambda i: (i, 0)),
                pl.BlockSpec((1, 1, mod.shape[-1]), lambda i: (i // blocks_per_mod, 0, 0)),
                pl.BlockSpec((1, d), const)]
    in_specs += [pl.BlockSpec(w.shape, const) for w in ws]
    in_specs += [pl.BlockSpec(b.shape, const) for b in bs]
    return pl.pallas_call(
        functools.partial(_in_kernel, d=d, n_seg=n_seg),
        grid=(r // tm,),
        in_specs=in_specs,
        out_specs=[pl.BlockSpec((tm, w.shape[1]), lambda i: (i, 0)) for w in ws],
        out_shape=[jax.ShapeDtypeStruct((r, w.shape[1]), F32) for w in ws],
        compiler_params=_cparams(1),
        name="in_proj",
    )(x2d, mod, nw, *ws, *bs)


def _mlstm_ones(L):
    ones_blk = np.zeros((2 * L, LANE), np.float32)
    ones_blk[:L, :HEAD_DIM] = 1.0
    ones_blk[L:, HEAD_DIM:] = 1.0
    return ones_blk


def _running(op, fill, x, row, reverse):
    L = x.shape[0]
    sh = 1
    while sh < L:
        if reverse:
            shifted, valid = pltpu.roll(x, L - sh, axis=0), row < L - sh
        else:
            shifted, valid = pltpu.roll(x, sh, axis=0), row >= sh
        x = op(x, jnp.where(valid, shifted, fill))
        sh *= 2
    return x


def _mlstm_chunk(blk, ones_ref, cn_ref, m_ref, d, mask, row, lane_lo, sub_lo, bd):
    L = blk.shape[0]
    reverse = d == 1
    end = 0 if reverse else L - 1
    cum = _running(jnp.add, 0.0, _log_sigmoid(blk[:, 1152:1280]), row, reverse)
    a = blk[:, 1024:1152] - cum
    m_prev = m_ref[d:d + 1, :]
    g = jnp.maximum(_running(jnp.maximum, NEG_BIG, a, row, reverse), m_prev)
    g_end = g[end:end + 1, :]
    w_inter = jnp.exp(m_prev - g)
    e_negm = jnp.exp(-(cum + g))
    m_ref[d:d + 1, :] = cum[end:end + 1, :] + g_end
    a_t = a.T

    def head_lanes(x, pair):
        c = 4 * d + 2 * pair
        return jnp.where(lane_lo, jnp.broadcast_to(x[:, c:c + 1], (L, LANE)),
                         jnp.broadcast_to(x[:, c + 1:c + 2], (L, LANE)))

    outs = []
    for pair in range(HEADS // 2):
        cs = slice(128 * pair, 128 * pair + 128)
        q_pair = blk[:, cs].astype(BF16)
        k_pair = blk[:, 256 + 128 * pair:256 + 128 * pair + 128] * (HEAD_DIM ** -0.5)
        v_pair = blk[:, 512 + 128 * pair:512 + 128 * pair + 128]
        k_b = k_pair.astype(BF16)
        zero = jnp.zeros_like(k_b)
        k_rows = jnp.concatenate([jnp.where(lane_lo, k_b, zero), jnp.where(lane_lo, zero, k_b)], axis=0)
        s_pair = lax.dot_general(q_pair, k_rows, (((1,), (1,)), ((), ())), preferred_element_type=F32)
        w_halves = []
        for sub in range(2):
            hd = 2 * pair + sub
            c = 4 * d + hd
            expo = jnp.where(mask, a_t[c:c + 1, :] - jnp.broadcast_to(g[:, c:c + 1], (L, L)), NEG_BIG)
            w_halves.append((jnp.exp(expo) * s_pair[:, sub * L:(sub + 1) * L]).astype(BF16))
        w_pair = jnp.concatenate(w_halves, axis=1)
        v_b = v_pair.astype(BF16)
        v_rows = jnp.concatenate([jnp.where(lane_lo, v_b, zero), jnp.where(lane_lo, zero, v_b)], axis=0)
        intra = jnp.dot(w_pair, jnp.concatenate([v_rows, ones_ref[...]], axis=1), preferred_element_type=F32)
        slot = 2 * d + pair
        cn = cn_ref[slot]
        inter = jnp.dot(q_pair, cn.astype(BF16), preferred_element_type=F32)
        wi_p = head_lanes(w_inter, pair)
        num = wi_p * inter[:, 0:128] + intra[:, 0:128]
        den = wi_p * inter[:, 128:256] + intra[:, 128:256]
        outs.append(num / jnp.maximum(jnp.abs(den), head_lanes(e_negm, pair)))
        c_e = 4 * d + 2 * pair
        w_st = jnp.exp(jnp.where(sub_lo, a_t[c_e:c_e + 1, :] - g_end[:, c_e:c_e + 1],
                                 a_t[c_e + 1:c_e + 2, :] - g_end[:, c_e + 1:c_e + 2]))
        ktw = (k_pair.T * w_st).astype(BF16)
        upd = jnp.dot(ktw, jnp.concatenate([v_b, jnp.ones_like(v_b)], axis=1), preferred_element_type=F32)
        decay = wi_p[end:end + 1, :]
        cn_ref[slot] = jnp.concatenate([decay, decay], axis=1) * cn + jnp.where(bd, upd, 0.0)
    return jnp.concatenate(outs, axis=1)


def _mlstm_kernel(*refs, need_ctx):
    ctx_ref, lat_ref, ones_ref, nw_ref = refs[:4]
    if need_ctx:
        yc_ref, yl_ref, ybc_ref, ybl_ref, cn_ref, m_ref = refs[4:]
    else:
        yl_ref, ybc_ref, ybl_ref, cn_ref, m_ref = refs[4:]
        yc_ref = None
    L = A_CHUNK
    lane_lo = lax.broadcasted_iota(jnp.int32, (L, LANE), 1) < HEAD_DIM
    row = lax.broadcasted_iota(jnp.int32, (L, LANE), 0)
    sub_lo = row < HEAD_DIM
    r2 = lax.broadcasted_iota(jnp.int32, (L, L), 0)
    c2 = lax.broadcasted_iota(jnp.int32, (L, L), 1)
    mask_f, mask_b = r2 >= c2, r2 <= c2
    rb = lax.broadcasted_iota(jnp.int32, (LANE, 2 * LANE), 0) < HEAD_DIM
    cb = (lax.broadcasted_iota(jnp.int32, (LANE, 2 * LANE), 1) % LANE) < HEAD_DIM
    bd = rb == cb
    cn_ref[...] = jnp.zeros_like(cn_ref)
    m_ref[...] = jnp.zeros_like(m_ref)

    def scan(src_ref, yf_ref, yb_ref):
        n_chunks = src_ref.shape[1] // L

        def body(j, carry):
            sf = pl.multiple_of(j * L, L)
            sb = pl.multiple_of((n_chunks - 1 - j) * L, L)
            hf = _mlstm_chunk(src_ref[0, pl.ds(sf, L), :], ones_ref, cn_ref, m_ref, 0, mask_f, row, lane_lo, sub_lo, bd)
            hb = _mlstm_chunk(src_ref[0, pl.ds(sb, L), :], ones_ref, cn_ref, m_ref, 1, mask_b, row, lane_lo, sub_lo, bd)
            if yf_ref is not None:
                yf_ref[0, pl.ds(sf, L), :] = hf
                yb_ref[pl.ds(sb, L), :] = hb
            return carry

        lax.fori_loop(0, n_chunks, body, 0)

    def finalize(src_ref, y_ref, yb_ref):
        n_chunks = src_ref.shape[1] // L

        def body(j, carry):
            s = pl.multiple_of(j * L, L)
            for pair in range(HEADS // 2):
                cs = slice(128 * pair, 128 * pair + 128)
                y = y_ref[0, pl.ds(s, L), cs] + yb_ref[pl.ds(s, L), cs]
                y2 = y * y
                s_lo = jnp.sum(jnp.where(lane_lo, y2, 0.0), axis=-1, keepdims=True)
                s_all = jnp.sum(y2, axis=-1, keepdims=True)
                ms = jnp.where(lane_lo, s_lo, s_all - s_lo) * (1.0 / HEAD_DIM)
                o_gate = src_ref[0, pl.ds(s, L), 768 + 128 * pair:768 + 128 * pair + 128]
                y_ref[0, pl.ds(s, L), cs] = y * lax.rsqrt(ms + NORM_EPS) * nw_ref[:, cs] * _sigmoid(o_gate)
            return carry

        lax.fori_loop(0, n_chunks, body, 0)

    scan(ctx_ref, yc_ref, ybc_ref)
    scan(lat_ref, yl_ref, ybl_ref)
    if need_ctx:
        finalize(ctx_ref, yc_ref, ybc_ref)
    finalize(lat_ref, yl_ref, ybl_ref)


def _mlstm_call(pa_ctx, pa_lat, nw, need_ctx):
    b, nc, wa = pa_ctx.shape
    n = pa_lat.shape[1]
    L = A_CHUNK
    ones_blk = jnp.asarray(_mlstm_ones(L), BF16)
    c2 = lambda i: (0, 0)
    c3 = lambda i: (0, 0, 0)
    out_specs = [pl.BlockSpec((1, n, 256), lambda i: (i, 0, 0))]
    out_shape = [jax.ShapeDtypeStruct((b, n, 256), F32)]
    if need_ctx:
        out_specs.insert(0, pl.BlockSpec((1, nc, 256), lambda i: (i, 0, 0)))
        out_shape.insert(0, jax.ShapeDtypeStruct((b, nc, 256), F32))
    res = pl.pallas_call(
        functools.partial(_mlstm_kernel, need_ctx=need_ctx),
        grid=(b,),
        in_specs=[pl.BlockSpec((1, nc, wa), lambda i: (i, 0, 0)),
                  pl.BlockSpec((1, n, wa), lambda i: (i, 0, 0)),
                  pl.BlockSpec(ones_blk.shape, c2),
                  pl.BlockSpec((1, 256), c2)],
        out_specs=out_specs,
        out_shape=out_shape,
        scratch_shapes=[pltpu.VMEM((nc, 256), F32), pltpu.VMEM((n, 256), F32),
                        pltpu.VMEM((HEADS, LANE, 2 * LANE), F32), pltpu.VMEM((8, LANE), F32)],
        compiler_params=_cparams(1),
        name="mlstm",
    )(pa_ctx, pa_lat, ones_blk, nw)
    return (res[0], res[1]) if need_ctx else (None, res[0])


D_FAST_BASE = 16
D_FAST_MAX_EXPONENT = 60.0


def _hgrn_tables(L, base):
    t = np.arange(L)
    lev0 = int(math.log2(base))
    n_split = int(math.log2(L)) - lev0
    same_base = (t[:, None] // base) == (t[None, :] // base)
    lvl_f = np.where(same_base & (t[:, None] >= t[None, :]), 0, -1)
    lvl_b = np.where(same_base & (t[:, None] <= t[None, :]), 0, -1)
    mids_f, mids_b = [], []
    for i in range(n_split):
        half, size = base << i, base << (i + 1)
        same = (t[:, None] // size) == (t[None, :] // size)
        upper = (t // half) % 2 == 1
        lvl_f = np.where(same & upper[:, None] & ~upper[None, :], i + 1, lvl_f)
        lvl_b = np.where(same & ~upper[:, None] & upper[None, :], i + 1, lvl_b)
        starts = np.arange(0, L, size)
        mids_f.append([(int(s), size, int(s) + half - 1) for s in starts])
        mids_b.append([(int(s), size, int(s) + half) for s in starts])
    wide = lambda m: np.concatenate([m, m], axis=-1).astype(np.int32)
    return wide(lvl_f), wide(lvl_b), mids_f, mids_b


def _rows_of(b, spec):
    return jnp.concatenate([jnp.broadcast_to(b[r:r + 1, :], (n, b.shape[1])) for _, n, r in spec], axis=0)


def _hgrn_chunk(blk, f_pre, lb, lvl_ref, mids, s_ref, x_ref, d, row, lane_lo, bd, base):
    L = blk.shape[0]
    reverse = d == 1
    end = 0 if reverse else L - 1
    qs = _silu(blk[:, 0:256])
    v = blk[:, 256:512]
    sig, sig_neg = _sigmoid_both(f_pre)
    log_f = jnp.log(jnp.maximum(lb + (1.0 - lb) * sig, F_EPS))
    k = (1.0 - lb) * sig_neg
    b = _running(jnp.add, 0.0, log_f, row, reverse)
    b_end = b[end:end + 1, :]
    q_in = qs * jnp.exp(b)
    k_out = k * jnp.exp(b_end - b)
    decay = jnp.exp(b_end)
    factors = []
    if base > 1:
        zero_row = jnp.zeros((1, b.shape[1]), F32)
        if reverse:
            refs = [b[s + base:s + base + 1, :] if s + base < L else zero_row for s in range(0, L, base)]
        else:
            refs = [b[s - 1:s, :] if s > 0 else zero_row for s in range(0, L, base)]
        r0 = jnp.concatenate([jnp.broadcast_to(r, (base, b.shape[1])) for r in refs], axis=0)
        factors.append((jnp.exp(b - r0), jnp.exp(r0 - b)))
        last = [b[s:s + 1, :] if reverse else b[s + base - 1:s + base, :] for s in range(0, L, base)]
        worst = functools.reduce(jnp.maximum, [r - e for r, e in zip(refs, last)])
        x_ref[d:d + 1, :] = jnp.maximum(x_ref[d:d + 1, :], worst)
    else:
        factors.append((None, None))
    for spec in mids:
        e = jnp.exp(-jnp.abs(b - _rows_of(b, spec)))
        factors.append((e, e))
    nt = (((1,), (1,)), ((), ()))
    lvl = lvl_ref[...]
    outs = []
    for pair in range(HEADS // 2):
        cs = slice(128 * pair, 128 * pair + 128)
        q_p, k_p = qs[:, cs], k[:, cs]
        zero = jnp.zeros((L, LANE), BF16)
        v_b = v[:, cs].astype(BF16)
        v_rows = jnp.concatenate([jnp.where(lane_lo, v_b, zero), jnp.where(lane_lo, zero, v_b)], axis=0)
        scores = jnp.zeros((L, 2 * L), F32)
        for i, (eq, ek) in enumerate(factors):
            q_l = (q_p if eq is None else q_p * eq[:, cs]).astype(BF16)
            k_l = (k_p if ek is None else k_p * ek[:, cs]).astype(BF16)
            k_rows = jnp.concatenate([jnp.where(lane_lo, k_l, zero), jnp.where(lane_lo, zero, k_l)], axis=0)
            scores = jnp.where(lvl == i, lax.dot_general(q_l, k_rows, nt, preferred_element_type=F32), scores)
        slot = 2 * d + pair
        s_t = s_ref[slot]
        outs.append(jnp.dot(scores.astype(BF16), v_rows, preferred_element_type=F32)
                    + lax.dot_general(q_in[:, cs].astype(BF16), s_t.astype(BF16), nt, preferred_element_type=F32))
        upd = lax.dot_general(v_b, k_out[:, cs].astype(BF16), (((0,), (0,)), ((), ())), preferred_element_type=F32)
        s_ref[slot] = decay[:, cs] * s_t + jnp.where(bd, upd, 0.0)
    return jnp.concatenate(outs, axis=1)


def _hgrn_kernel(*refs, need_ctx, layer, mids_fast, mids_safe):
    ctx_ref, lat_ref, lvlf_fast_ref, lvlb_fast_ref, lvlf_safe_ref, lvlb_safe_ref, lbl_ref, nw_ref = refs[:8]
    if need_ctx:
        yc_ref, yl_ref, ybc_ref, ybl_ref, s_ref, x_ref = refs[8:]
    else:
        yl_ref, ybc_ref, ybl_ref, s_ref, x_ref = refs[8:]
        yc_ref = None
    L = D_CHUNK
    lane_lo = lax.broadcasted_iota(jnp.int32, (L, LANE), 1) < HEAD_DIM
    row = lax.broadcasted_iota(jnp.int32, (L, 2 * LANE), 0)
    bd = ((lax.broadcasted_iota(jnp.int32, (LANE, LANE), 0) < HEAD_DIM)
          == (lax.broadcasted_iota(jnp.int32, (LANE, LANE), 1) < HEAD_DIM))
    logits = lbl_ref[...]
    e = jnp.exp(logits - jnp.max(logits, axis=0, keepdims=True))
    prob = e / jnp.sum(e, axis=0, keepdims=True)
    lb = jnp.sum(prob[0:layer + 1], axis=0, keepdims=True) - prob[0:1]
    def scan(src_ref, yf_ref, yb_ref, lvlf_ref, lvlb_ref, mids, base):
        n_chunks = src_ref.shape[1] // L

        def body(j, carry):
            sf = pl.multiple_of(j * L, L)
            sb = pl.multiple_of((n_chunks - 1 - j) * L, L)
            blk_f = src_ref[0, pl.ds(sf, L), :]
            blk_b = src_ref[0, pl.ds(sb, L), :]
            of = _hgrn_chunk(blk_f, blk_f[:, 512:768], lb, lvlf_ref, mids[0], s_ref, x_ref, 0, row, lane_lo, bd, base)
            ob = _hgrn_chunk(blk_b, blk_b[:, 768:1024], lb, lvlb_ref, mids[1], s_ref, x_ref, 1, row, lane_lo, bd, base)
            if yf_ref is not None:
                yf_ref[0, pl.ds(sf, L), :] = of
                yb_ref[pl.ds(sb, L), :] = ob
            return carry

        lax.fori_loop(0, n_chunks, body, 0)

    def scans(lvlf_ref, lvlb_ref, mids, base):
        s_ref[...] = jnp.zeros_like(s_ref)
        scan(ctx_ref, yc_ref, ybc_ref, lvlf_ref, lvlb_ref, mids, base)
        scan(lat_ref, yl_ref, ybl_ref, lvlf_ref, lvlb_ref, mids, base)

    x_ref[...] = jnp.zeros_like(x_ref)
    scans(lvlf_fast_ref, lvlb_fast_ref, mids_fast, D_FAST_BASE)

    @pl.when(jnp.logical_not(jnp.max(x_ref[...]) <= D_FAST_MAX_EXPONENT))
    def _():
        scans(lvlf_safe_ref, lvlb_safe_ref, mids_safe, 1)

    def finalize(src_ref, y_ref, yb_ref):
        n_chunks = src_ref.shape[1] // L

        def body(j, carry):
            s = pl.multiple_of(j * L, L)
            for pair in range(HEADS // 2):
                cs = slice(128 * pair, 128 * pair + 128)
                y = y_ref[0, pl.ds(s, L), cs] + yb_ref[pl.ds(s, L), cs]
                y2 = y * y
                s_lo = jnp.sum(jnp.where(lane_lo, y2, 0.0), axis=-1, keepdims=True)
                s_all = jnp.sum(y2, axis=-1, keepdims=True)
                ms = jnp.where(lane_lo, s_lo, s_all - s_lo) * (1.0 / HEAD_DIM)
                g = src_ref[0, pl.ds(s, L), 1024 + 128 * pair:1024 + 128 * pair + 128]
                y_ref[0, pl.ds(s, L), cs] = y * lax.rsqrt(ms + NORM_EPS) * nw_ref[:, cs] * _silu(g)
            return carry

        lax.fori_loop(0, n_chunks, body, 0)

    if need_ctx:
        finalize(ctx_ref, yc_ref, ybc_ref)
    finalize(lat_ref, yl_ref, ybl_ref)


def _hgrn_call(pd_ctx, pd_lat, lb_logits, nw, layer, need_ctx):
    b, nc, wd = pd_ctx.shape
    n = pd_lat.shape[1]
    L = D_CHUNK
    lvlf_fast, lvlb_fast, midsf_fast, midsb_fast = _hgrn_tables(L, D_FAST_BASE)
    lvlf_safe, lvlb_safe, midsf_safe, midsb_safe = _hgrn_tables(L, 1)
    tables = [jnp.asarray(a) for a in (lvlf_fast, lvlb_fast, lvlf_safe, lvlb_safe)]
    c2 = lambda i: (0, 0)
    out_specs = [pl.BlockSpec((1, n, 256), lambda i: (i, 0, 0))]
    out_shape = [jax.ShapeDtypeStruct((b, n, 256), F32)]
    if need_ctx:
        out_specs.insert(0, pl.BlockSpec((1, nc, 256), lambda i: (i, 0, 0)))
        out_shape.insert(0, jax.ShapeDtypeStruct((b, nc, 256), F32))
    res = pl.pallas_call(
        functools.partial(_hgrn_kernel, need_ctx=need_ctx, layer=layer,
                          mids_fast=(midsf_fast, midsb_fast), mids_safe=(midsf_safe, midsb_safe)),
        grid=(b,),
        in_specs=[pl.BlockSpec((1, nc, wd), lambda i: (i, 0, 0)),
                  pl.BlockSpec((1, n, wd), lambda i: (i, 0, 0))]
                 + [pl.BlockSpec(a.shape, c2) for a in tables]
                 + [pl.BlockSpec(lb_logits.shape, c2), pl.BlockSpec((1, 256), c2)],
        out_specs=out_specs,
        out_shape=out_shape,
        scratch_shapes=[pltpu.VMEM((nc, 256), F32), pltpu.VMEM((n, 256), F32),
                        pltpu.VMEM((HEADS, LANE, LANE), F32), pltpu.VMEM((8, 256), F32)],
        compiler_params=_cparams(1),
        name="hgrn2",
    )(pd_ctx, pd_lat, *tables, lb_logits, nw)
    return (res[0], res[1]) if need_ctx else (None, res[0])


def _gmlp_kernel(pc_ref, ws_ref, bs_ref, nw_ref, nb_ref, o_ref):
    L = C_CHUNK
    n_chunks = pc_ref.shape[0] // L
    lane_lo = lax.broadcasted_iota(jnp.int32, (L, LANE), 1) < 256 // C_GROUPS
    for c in range(n_chunks):
        x = pc_ref[c * L:(c + 1) * L, :]
        g = 0.5 * x * (1.0 + jnp.tanh(math.sqrt(2.0 / math.pi) * (x + 0.044715 * (x * x * x))))
        u, v = g[:, 0:256], g[:, 256:512]
        mu = jnp.mean(v, axis=-1, keepdims=True)
        vc = v - mu
        var = jnp.mean(vc * vc, axis=-1, keepdims=True)
        vn = (vc * lax.rsqrt(var + NORM_EPS) * nw_ref[...] + nb_ref[...]).astype(BF16)
        mixed = []
        for pair in range(C_GROUPS // 2):
            v_p = vn[:, pair * LANE:(pair + 1) * LANE]
            mixed.append(jnp.where(lane_lo, jnp.dot(ws_ref[2 * pair], v_p, preferred_element_type=F32),
                                   jnp.dot(ws_ref[2 * pair + 1], v_p, preferred_element_type=F32)))
        o_ref[c * L:(c + 1) * L, :] = u * (jnp.concatenate(mixed, axis=1) + bs_ref[...])


def _gmlp_call(pc2d, ws, bs_t, nw, nb, tm):
    r = pc2d.shape[0]
    c2 = lambda i: (0, 0)
    return pl.pallas_call(
        _gmlp_kernel,
        grid=(r // tm,),
        in_specs=[pl.BlockSpec((tm, 512), lambda i: (i, 0)),
                  pl.BlockSpec(ws.shape, lambda i: (0, 0, 0)),
                  pl.BlockSpec(bs_t.shape, c2), pl.BlockSpec((1, 256), c2), pl.BlockSpec((1, 256), c2)],
        out_specs=pl.BlockSpec((tm, 256), lambda i: (i, 0)),
        out_shape=jax.ShapeDtypeStruct((r, 256), F32),
        compiler_params=_cparams(1),
        name="gmlp",
    )(pc2d, ws, bs_t, nw, nb)


def _filter_feats(n):
    t = np.linspace(0.0, 1.0, n, dtype=np.float32)[:, None]
    bands = np.linspace(1e-4, B_BANDS - 1, B_BANDS, dtype=np.float32)[None]
    ang = (np.float32(2 * math.pi) * bands * np.arange(n, dtype=np.float32)[:, None] / np.float32(n)).astype(np.float32)
    z = np.concatenate([t, np.cos(ang), -np.sin(ang)], axis=-1).astype(np.float32)
    deltas = np.abs(np.linspace(math.log(B_DECAY_TARGET) / B_SLOW_DECAY,
                                math.log(B_DECAY_TARGET) / B_FAST_DECAY, 256, dtype=np.float32))
    neg_t_deltas = (-t * deltas[None]).astype(np.float32)
    return z, neg_t_deltas


def _filter_kernel(z_ref, ntd_ref, w1_ref, b1_ref, f1_ref, w2_ref, b2_ref, f2_ref, w3_ref, flip_ref, o_ref):
    n = z_ref.shape[0]
    T = flip_ref.shape[0]
    hd = jnp.sin(f1_ref[...] * (_dot_f32(z_ref[...], w1_ref[...]) + b1_ref[...]))
    hd = jnp.sin(f2_ref[...] * (_dot_f32(hd, w2_ref[...]) + b2_ref[...]))
    decay = jnp.exp(ntd_ref[...])
    fwd = _dot_f32(hd, w3_ref[:, 0:256]) * decay
    bwd = _dot_f32(hd, w3_ref[:, 256:512]) * decay
    flip = flip_ref[...]
    rev = []
    for i in range(n // T):
        parts = _split3(bwd[n - (i + 1) * T:n - i * T])
        rev.append(sum(jnp.dot(flip, p, preferred_element_type=F32) for p in parts))
    rev = jnp.concatenate(rev, axis=0)
    row = lax.broadcasted_iota(jnp.int32, rev.shape, 0)
    o_ref[0:n, :] = jnp.where(row == 0, 0.0, pltpu.roll(rev, 1, axis=0))
    o_ref[n:2 * n, :] = fwd


def _filter_call(n, w1, b1, f1, w2, b2, f2, w3):
    z, ntd = _filter_feats(n)
    k_pad = 32
    z = np.pad(z, ((0, 0), (0, k_pad - z.shape[1])))
    w1 = jnp.pad(w1, ((0, k_pad - w1.shape[0]), (0, 0)))
    flip = jnp.asarray(np.eye(CONV_BLOCK, dtype=np.float32)[::-1], BF16)
    args = (jnp.asarray(z), jnp.asarray(ntd), w1, b1, f1, w2, b2, f2, w3, flip)
    return pl.pallas_call(
        _filter_kernel,
        grid=(1,),
        in_specs=[pl.BlockSpec(a.shape, lambda i: (0, 0)) for a in args],
        out_specs=pl.BlockSpec((2 * n, 256), lambda i: (0, 0)),
        out_shape=jax.ShapeDtypeStruct((2 * n, 256), F32),
        compiler_params=_cparams(1),
        name="hyena_filter",
    )(*args)


def _short_conv_kernel(pb_ref, w_ref, b_ref, x0_ref, u_ref):
    x = pb_ref[0]
    n = x.shape[0]
    row = lax.broadcasted_iota(jnp.int32, x.shape, 0)
    prev = jnp.where(row == 0, 0.0, pltpu.roll(x, 1, axis=0))
    nxt = jnp.where(row == n - 1, 0.0, pltpu.roll(x, n - 1, axis=0))
    y = prev * w_ref[0:1, :] + x * w_ref[1:2, :] + nxt * w_ref[2:3, :] + b_ref[...]
    x0_ref[0] = y[:, 0:256]
    u_ref[0] = y[:, 256:512] * y[:, 512:768]


def _short_conv_call(pb, w, b):
    bsz, n, c3 = pb.shape
    return pl.pallas_call(
        _short_conv_kernel,
        grid=(bsz,),
        in_specs=[pl.BlockSpec((1, n, c3), lambda i: (i, 0, 0)),
                  pl.BlockSpec(w.shape, lambda i: (0, 0)), pl.BlockSpec(b.shape, lambda i: (0, 0))],
        out_specs=[pl.BlockSpec((1, n, 256), lambda i: (i, 0, 0))] * 2,
        out_shape=[jax.ShapeDtypeStruct((bsz, n, 256), F32)] * 2,
        compiler_params=_cparams(1),
        name="hyena_short_conv",
    )(pb, w, b)


def _long_conv_kernel(u_ref, k_ref, o_ref, *, n_blk, bsz, ch_per_step):
    T = CONV_BLOCK
    half = T // 2
    for c in range(ch_per_step):
        taps = k_ref[c]
        rolled = pltpu.roll(jnp.broadcast_to(taps, (half, taps.shape[1])), 0, axis=1, stride=1, stride_axis=0)
        acc = [None] * n_blk
        for d in range(-(n_blk - 1), n_blk):
            base = (d + n_blk) * T
            tile = jnp.concatenate([rolled[:, base:base + T], rolled[:, base - half:base - half + T]],
                                   axis=0).astype(BF16)
            j_lo, j_hi = max(0, -d), min(n_blk, n_blk - d)
            lhs = u_ref[c, j_lo * bsz:j_hi * bsz, :].astype(BF16)
            res = jnp.dot(lhs, tile, preferred_element_type=F32)
            for j in range(j_lo, j_hi):
                part = res[(j - j_lo) * bsz:(j - j_lo + 1) * bsz]
                i = j + d
                acc[i] = part if acc[i] is None else acc[i] + part
        o_ref[c] = jnp.concatenate(acc, axis=0)


def _long_conv_call(u_t, k_t, n_blk, bsz, ch_per_step):
    ch, rows, T = u_t.shape
    return pl.pallas_call(
        functools.partial(_long_conv_kernel, n_blk=n_blk, bsz=bsz, ch_per_step=ch_per_step),
        grid=(ch // ch_per_step,),
        in_specs=[pl.BlockSpec((ch_per_step, rows, T), lambda i: (i, 0, 0)),
                  pl.BlockSpec((ch_per_step, 1, k_t.shape[-1]), lambda i: (i, 0, 0))],
        out_specs=pl.BlockSpec((ch_per_step, rows, T), lambda i: (i, 0, 0)),
        out_shape=jax.ShapeDtypeStruct((ch, rows, T), F32),
        compiler_params=_cparams(1),
        name="hyena_long_conv",
    )(u_t, k_t)


def _hyena_conv(u, filt):
    bsz, n, ch = u.shape
    T = CONV_BLOCK
    n_blk = n // T
    k_t = filt.T.reshape(ch, 1, 2 * n)
    u_t = u.reshape(bsz, n_blk, T, ch).transpose(3, 1, 0, 2).reshape(ch, n_blk * bsz, T)
    y_t = _long_conv_call(u_t, k_t, n_blk, bsz, 4)
    return y_t.reshape(ch, n_blk, bsz, T).transpose(2, 1, 3, 0).reshape(bsz, n, ch)


def _merge_kernel(x_ref, mod_ref, nw_ref, ya_ref, x0_ref, u_ref, yconv_ref, yc_ref, yd_ref, hyb_ref,
                  wg_ref, wb_ref, wo_ref, o_ref, *, d):
    x = x_ref[...]
    h = _norm_mod(x, nw_ref[...], mod_ref[0, :, 0:d], mod_ref[0, :, d:2 * d]).astype(BF16)
    y_b = x0_ref[...] * (yconv_ref[...] + u_ref[...] * hyb_ref[...])
    ys = (ya_ref[...], y_b, yc_ref[...], yd_ref[...])
    acc = None
    for j in range(N_BRANCH):
        gate = _sigmoid(jnp.dot(h, wg_ref[j], preferred_element_type=F32))
        term = gate * jnp.dot(ys[j].astype(BF16), wb_ref[j], preferred_element_type=F32)
        acc = term if acc is None else acc + term
    y = jnp.dot(acc.astype(BF16), wo_ref[...], preferred_element_type=F32)
    o_ref[...] = x + mod_ref[0, :, 2 * d:3 * d] * y


def _merge_call(x2d, mod, nw, ya, x0, u, yconv, yc, yd, hyb, wg, wb, wo, rows_per_mod, tm):
    r, d = x2d.shape
    blocks_per_mod = rows_per_mod // tm
    c2 = lambda i: (0, 0)
    c3 = lambda i: (0, 0, 0)
    row = lambda w: pl.BlockSpec((tm, w), lambda i: (i, 0))
    return pl.pallas_call(
        functools.partial(_merge_kernel, d=d),
        grid=(r // tm,),
        in_specs=[row(d), pl.BlockSpec((1, 1, mod.shape[-1]), lambda i: (i // blocks_per_mod, 0, 0)),
                  pl.BlockSpec((1, d), c2), row(256), row(256), row(256), row(256), row(256), row(256),
                  pl.BlockSpec((1, 256), c2),
                  pl.BlockSpec(wg.shape, c3), pl.BlockSpec(wb.shape, c3), pl.BlockSpec(wo.shape, c2)],
        out_specs=row(d),
        out_shape=jax.ShapeDtypeStruct((r, d), F32),
        compiler_params=_cparams(1),
        name="merge",
    )(x2d, mod, nw, ya, x0, u, yconv, yc, yd, hyb, wg, wb, wo)


def _ffn_kernel(x_ref, mod_ref, nw_ref, wa_ref, wg_ref, wd_ref, fnw_ref, o_ref, *, d, hid_chunk, final_norm):
    x = x_ref[...]
    h = _norm_mod(x, nw_ref[...], mod_ref[0, :, 3 * d:4 * d], mod_ref[0, :, 4 * d:5 * d]).astype(BF16)
    hidden = wa_ref.shape[1]
    acc = None
    for c in range(hidden // hid_chunk):
        cs = slice(c * hid_chunk, (c + 1) * hid_chunk)
        a = jnp.dot(h, wa_ref[:, cs], preferred_element_type=F32)
        g = jnp.dot(h, wg_ref[:, cs], preferred_element_type=F32)
        term = jnp.dot((_silu(a) * g).astype(BF16), wd_ref[cs, :], preferred_element_type=F32)
        acc = term if acc is None else acc + term
    y = x + mod_ref[0, :, 5 * d:6 * d] * acc
    if final_norm:
        y = y * lax.rsqrt(jnp.mean(y * y, axis=-1, keepdims=True) + NORM_EPS) * fnw_ref[...]
    o_ref[...] = y


def _ffn_call(x2d, mod, nw, wa, wg, wd, fnw, rows_per_mod, tm, final_norm):
    r, d = x2d.shape
    blocks_per_mod = rows_per_mod // tm
    c2 = lambda i: (0, 0)
    return pl.pallas_call(
        functools.partial(_ffn_kernel, d=d, hid_chunk=256, final_norm=final_norm),
        grid=(r // tm,),
        in_specs=[pl.BlockSpec((tm, d), lambda i: (i, 0)),
                  pl.BlockSpec((1, 1, mod.shape[-1]), lambda i: (i // blocks_per_mod, 0, 0)),
                  pl.BlockSpec((1, d), c2),
                  pl.BlockSpec(wa.shape, c2), pl.BlockSpec(wg.shape, c2), pl.BlockSpec(wd.shape, c2),
                  pl.BlockSpec((1, d), c2)],
        out_specs=pl.BlockSpec((tm, d), lambda i: (i, 0)),
        out_shape=jax.ShapeDtypeStruct((r, d), F32),
        compiler_params=_cparams(1),
        name="ffn",
    )(x2d, mod, nw, wa, wg, wd, fnw)


def _grid_transpose(x, rows, cols):
    bsz, n, d = x.shape
    return x.reshape(bsz, rows, cols, d).swapaxes(1, 2).reshape(bsz, n, d)


def _row_tile(rows):
    return 512 if rows % 512 == 0 else 256


def kernel(x, c, ctx, c_ctx, ada_w, ada_b, norm1_w, norm2_w, w_in, b_in, mlstm_norm_w, hy_short_w, hy_short_b, hy_w1, hy_b1, hy_freq1, hy_w2, hy_b2, hy_freq2, hy_w3, hy_bias, gm_norm_w, gm_norm_b, gm_ws, gm_bs, hg_lb_logits, hg_norm_w, w_gate, w_branch, w_out, w_ffn_in, w_ffn_out, final_norm_w):
    bsz, n, d = x.shape
    nc = ctx.shape[1]
    depth = ada_w.shape[0]
    rows = n // GRID_W
    mw = d // N_BRANCH
    hidden = w_ffn_out.shape[1]
    assert mw == 256 and mw == HEADS * HEAD_DIM

    mb = -(-(bsz + 1) // 8) * 8
    cvec = jnp.concatenate([c, c_ctx[None], jnp.zeros((mb - bsz - 1, d), F32)], axis=0)
    mods = _ada_call(cvec, ada_w, ada_b)

    s_a, s_g, s_b, s_c = 4 * mw, 4 * mw + 4 * HEADS, 7 * mw + 4 * HEADS, 9 * mw + 4 * HEADS
    gate_pad = LANE - 2 * HEADS

    xc = ctx
    for l in range(depth):
        need_ctx = l < depth - 1
        col_major = l % 2 == 1
        if col_major:
            x = _grid_transpose(x, rows, GRID_W)
        mod_lat = mods[l, :bsz].reshape(bsz, 1, 6 * d)
        mod_ctx = mods[l, bsz:bsz + 1].reshape(1, 1, 6 * d)
        wl, bl = w_in[l], b_in[l]
        wgt, bgt = wl[:, s_a:s_g].reshape(d, 4, HEADS), bl[s_a:s_g].reshape(4, HEADS)
        zw, zb = jnp.zeros((d, gate_pad), F32), jnp.zeros((gate_pad,), F32)
        w_segs = [jnp.concatenate([wl[:, :s_a], wgt[:, 0], wgt[:, 2], zw, wgt[:, 1], wgt[:, 3], zw], axis=1),
                  wl[:, s_g:s_b], wl[:, s_b:s_c], wl[:, s_c:]]
        b_segs = [jnp.concatenate([bl[:s_a], bgt[0], bgt[2], zb, bgt[1], bgt[3], zb]),
                  bl[s_g:s_b], bl[s_b:s_c], bl[s_c:]]
        w_segs = [w.astype(BF16) for w in w_segs]
        b_segs = [b.reshape(1, -1) for b in b_segs]
        nw1 = norm1_w[l].reshape(1, d)

        x2d = x.reshape(bsz * n, d)
        xc2d = xc.reshape(bsz * nc, d)
        pa, pb, pc, pd = _in_call(x2d, mod_lat, nw1, w_segs, b_segs, n, _row_tile(n))
        if need_ctx:
            ca, cb, cc, cd = _in_call(xc2d, mod_ctx, nw1, w_segs, b_segs, bsz * nc, _row_tile(bsz * nc))
        else:
            ca, cd = _in_call(xc2d, mod_ctx, nw1, [w_segs[0], w_segs[3]], [b_segs[0], b_segs[3]],
                              bsz * nc, _row_tile(bsz * nc))

        yac, ya = _mlstm_call(ca.reshape(bsz, nc, -1), pa.reshape(bsz, n, -1),
                              mlstm_norm_w[l].reshape(1, mw), need_ctx)
        ydc, yd = _hgrn_call(cd.reshape(bsz, nc, -1), pd.reshape(bsz, n, -1), hg_lb_logits,
                             hg_norm_w[l].reshape(1, mw), l, need_ctx)

        hy_args = (hy_w1[l], hy_b1[l].reshape(1, -1), hy_freq1[l].reshape(1, -1), hy_w2[l],
                   hy_b2[l].reshape(1, -1), hy_freq2[l].reshape(1, -1), hy_w3[l])
        sw, sb = hy_short_w[l], hy_short_b[l].reshape(1, -1)
        gws, gbs_t = gm_ws[l].astype(BF16), jnp.repeat(gm_bs[l].T, mw // C_GROUPS, axis=1)
        gnw, gnb = gm_norm_w[l].reshape(1, mw), gm_norm_b[l].reshape(1, mw)
        hyb = hy_bias[l].reshape(1, mw)
        wg, wb, wo = w_gate[l].astype(BF16), w_branch[l].astype(BF16), w_out[l].astype(BF16)
        wfa, wfg = w_ffn_in[l][:, :hidden].astype(BF16), w_ffn_in[l][:, hidden:].astype(BF16)
        wfd = w_ffn_out[l].astype(BF16)
        nw2 = norm2_w[l].reshape(1, d)
        fnw = final_norm_w.reshape(1, d)

        x0, u = _short_conv_call(pb.reshape(bsz, n, -1), sw, sb)
        yconv = _hyena_conv(u, _filter_call(n, *hy_args))
        ycm = _gmlp_call(pc, gws, gbs_t, gnw, gnb, _row_tile(n))
        x2d = _merge_call(x2d, mod_lat, nw1, ya.reshape(bsz * n, mw), x0.reshape(bsz * n, mw),
                          u.reshape(bsz * n, mw), yconv.reshape(bsz * n, mw), ycm, yd.reshape(bsz * n, mw),
                          hyb, wg, wb, wo, n, _row_tile(n))
        x2d = _ffn_call(x2d, mod_lat, nw2, wfa, wfg, wfd, fnw, n, _row_tile(n), l == depth - 1)
        x = x2d.reshape(bsz, n, d)

        if need_ctx:
            tmc = _row_tile(bsz * nc)
            x0c, uc = _short_conv_call(cb.reshape(bsz, nc, -1), sw, sb)
            yconv_c = _hyena_conv(uc, _filter_call(nc, *hy_args))
            ycm_c = _gmlp_call(cc, gws, gbs_t, gnw, gnb, tmc)
            xc2d = _merge_call(xc2d, mod_ctx, nw1, yac.reshape(bsz * nc, mw), x0c.reshape(bsz * nc, mw),
                               uc.reshape(bsz * nc, mw), yconv_c.reshape(bsz * nc, mw), ycm_c,
                               ydc.reshape(bsz * nc, mw), hyb, wg, wb, wo, bsz * nc, tmc)
            xc2d = _ffn_call(xc2d, mod_ctx, nw2, wfa, wfg, wfd, fnw, bsz * nc, tmc, False)
            xc = xc2d.reshape(bsz, nc, d)
        if col_major:
            x = _grid_transpose(x, GRID_W, rows)
    return x
```

```python
import functools
import math

import numpy as np
import jax
import jax.numpy as jnp
from jax import lax
from jax.experimental import pallas as pl
from jax.experimental.pallas import tpu as pltpu

F32 = jnp.float32
BF16 = jnp.bfloat16

GRID_W = 64
NORM_EPS = 1e-6
N_BRANCH = 4
NEG_BIG = -1e30
F_EPS = 1e-30
HEAD_DIM = 64
HEADS = 4
LANE = 128
A_CHUNK = 128
D_CHUNK = 128
C_CHUNK = 128
C_GROUPS = 4
B_SHORT = 3
B_BANDS = 8
B_DECAY_TARGET = 1e-2
B_FAST_DECAY = 0.3
B_SLOW_DECAY = 1.5
CONV_BLOCK = 256
VMEM_LIMIT = 56 * 1024 * 1024


def _cparams(n_axes):
    return pltpu.CompilerParams(dimension_semantics=("arbitrary",) * n_axes,
                                vmem_limit_bytes=VMEM_LIMIT)


def _dot(a, b):
    return jnp.dot(a.astype(BF16), b.astype(BF16), preferred_element_type=F32)


def _split3(x):
    hi = x.astype(BF16)
    r1 = x - hi.astype(F32)
    mid = r1.astype(BF16)
    lo = (r1 - mid.astype(F32)).astype(BF16)
    return hi, mid, lo


def _dot_f32(a, b):
    a_hi, a_mid, a_lo = _split3(a)
    b_hi, b_mid, b_lo = _split3(b)
    d = lambda u, v: jnp.dot(u, v, preferred_element_type=F32)
    return (d(a_hi, b_hi) + (d(a_hi, b_mid) + d(a_mid, b_hi))
            + (d(a_hi, b_lo) + d(a_mid, b_mid) + d(a_lo, b_hi)))


def _sigmoid(x):
    return 1.0 / (1.0 + jnp.exp(-x))


def _sigmoid_both(x):
    e = jnp.exp(-jnp.abs(x))
    big = 1.0 / (1.0 + e)
    small = e * big
    pos = x >= 0.0
    return jnp.where(pos, big, small), jnp.where(pos, small, big)


def _silu(x):
    return x * _sigmoid(x)


def _log_sigmoid(x):
    return jnp.minimum(x, 0.0) - jnp.log(1.0 + jnp.exp(-jnp.abs(x)))


def _norm_mod(x, nw, shift, scale):
    ms = jnp.mean(x * x, axis=-1, keepdims=True)
    y = x * lax.rsqrt(ms + NORM_EPS) * nw
    return y * (1.0 + scale) + shift


def _ada_kernel(c_ref, w_ref, b_ref, o_ref):
    o_ref[0] = _dot_f32(_silu(c_ref[...]), w_ref[0]) + b_ref[0]


def _ada_call(cvec, ada_w, ada_b):
    depth, d, n6 = ada_w.shape
    mb = cvec.shape[0]
    tn = 512
    return pl.pallas_call(
        _ada_kernel,
        grid=(depth, n6 // tn),
        in_specs=[pl.BlockSpec((mb, d), lambda l, j: (0, 0)),
                  pl.BlockSpec((1, d, tn), lambda l, j: (l, 0, j)),
                  pl.BlockSpec((1, 1, tn), lambda l, j: (l, 0, j))],
        out_specs=pl.BlockSpec((1, mb, tn), lambda l, j: (l, 0, j)),
        out_shape=jax.ShapeDtypeStruct((depth, mb, n6), F32),
        compiler_params=_cparams(2),
        name="ada_mod",
    )(cvec, ada_w, ada_b.reshape(depth, 1, n6))


def _in_kernel(*refs, d, n_seg):
    x_ref, mod_ref, nw_ref = refs[:3]
    w_refs = refs[3:3 + n_seg]
    b_refs = refs[3 + n_seg:3 + 2 * n_seg]
    o_refs = refs[3 + 2 * n_seg:]
    h = _norm_mod(x_ref[...], nw_ref[...], mod_ref[0, :, 0:d], mod_ref[0, :, d:2 * d]).astype(BF16)
    for w_ref, b_ref, o_ref in zip(w_refs, b_refs, o_refs):
        o_ref[...] = jnp.dot(h, w_ref[...], preferred_element_type=F32) + b_ref[...]


def _in_call(x2d, mod, nw, ws, bs, rows_per_mod, tm):
    r, d = x2d.shape
    n_seg = len(ws)
    blocks_per_mod = rows_per_mod // tm
    const = lambda i: (0, 0)
    in_specs = [pl.BlockSpec((tm, d), lambda i: (i, 0)),
                pl.BlockSpec((1, 1, mod.shape[-1]), lambda i: (i // blocks_per_mod, 0, 0)),
                pl.BlockSpec((1, d), const)]
    in_specs += [pl.BlockSpec(w.shape, const) for w in ws]
    in_specs += [pl.BlockSpec(b.shape, const) for b in bs]
    return pl.pallas_call(
        functools.partial(_in_kernel, d=d, n_seg=n_seg),
        grid=(r // tm,),
        in_specs=in_specs,
        out_specs=[pl.BlockSpec((tm, w.shape[1]), lambda i: (i, 0)) for w in ws],
        out_shape=[jax.ShapeDtypeStruct((r, w.shape[1]), F32) for w in ws],
        compiler_params=_cparams(1),
        name="in_proj",
    )(x2d, mod, nw, *ws, *bs)


def _load_tile(x_ref, d):
    if len(x_ref.shape) == 2:
        return x_ref[...]
    ncol = x_ref.shape[2] // d
    return jnp.concatenate([x_ref[0, :, c * d:(c + 1) * d] for c in range(ncol)], axis=0)


def _store_tile(o_ref, y, d):
    if len(o_ref.shape) == 2:
        o_ref[...] = y
        return
    rows = o_ref.shape[1]
    for c in range(o_ref.shape[2] // d):
        o_ref[0, :, c * d:(c + 1) * d] = y[c * rows:(c + 1) * rows, :]


def _tile_specs(tm, d, col_view):
    if col_view is None:
        hb = tm // 8
        return (pl.BlockSpec((tm, d), lambda i: (i, 0)),
                lambda n_rows: pl.BlockSpec((8, d), lambda i: (jnp.maximum(i * hb - 1, 0), 0)),
                lambda n_rows: pl.BlockSpec((8, d), lambda i: (jnp.minimum((i + 1) * hb, n_rows // 8 - 1), 0)),
                lambda x2d: x2d)
    rows, cols = col_view
    ncol = tm // rows
    tps = cols // ncol
    return (pl.BlockSpec((1, rows, ncol * d), lambda i: (i // tps, 0, i % tps)),
            lambda n_rows: pl.BlockSpec((1, rows, d), lambda i: (i // tps, 0, jnp.maximum((i % tps) * ncol - 1, 0))),
            lambda n_rows: pl.BlockSpec((1, rows, d),
                                        lambda i: (i // tps, 0, jnp.minimum((i % tps + 1) * ncol, cols - 1))),
            lambda x2d: x2d.reshape(-1, rows, cols * d))


def _gmlp_chunk(x, ws_ref, bs_ref, nw_ref, nb_ref, lane_lo):
    g = 0.5 * x * (1.0 + jnp.tanh(math.sqrt(2.0 / math.pi) * (x + 0.044715 * (x * x * x))))
    u, v = g[:, 0:256], g[:, 256:512]
    mu = jnp.mean(v, axis=-1, keepdims=True)
    vc = v - mu
    var = jnp.mean(vc * vc, axis=-1, keepdims=True)
    vn = (vc * lax.rsqrt(var + NORM_EPS) * nw_ref[...] + nb_ref[...]).astype(BF16)
    mixed = []
    for pair in range(C_GROUPS // 2):
        v_p = vn[:, pair * LANE:(pair + 1) * LANE]
        mixed.append(jnp.where(lane_lo, jnp.dot(ws_ref[2 * pair], v_p, preferred_element_type=F32),
                               jnp.dot(ws_ref[2 * pair + 1], v_p, preferred_element_type=F32)))
    return u * (jnp.concatenate(mixed, axis=1) + bs_ref[...])


def _in_full_kernel(x_ref, xp_ref, xn_ref, mod_ref, nw_ref, wa_ref, wb_ref, wc_ref, wd_ref,
                    ba_ref, bb_ref, bc_ref, bd_ref, sw_ref, sb_ref, gws_ref, gbs_ref, gnw_ref, gnb_ref,
                    oa_ref, ox0_ref, ou_ref, oc_ref, od_ref, *, d, seq_len):
    shift, scale, nw = mod_ref[0, :, 0:d], mod_ref[0, :, d:2 * d], nw_ref[...]
    x = _load_tile(x_ref, d)
    tm = x.shape[0]
    h = _norm_mod(x, nw, shift, scale).astype(BF16)
    pc = jnp.dot(h, wc_ref[...], preferred_element_type=F32) + bc_ref[...]
    pb = jnp.dot(h, wb_ref[...], preferred_element_type=F32) + bb_ref[...]
    oa_ref[...] = jnp.dot(h, wa_ref[...], preferred_element_type=F32) + ba_ref[...]
    od_ref[...] = jnp.dot(h, wd_ref[...], preferred_element_type=F32) + bd_ref[...]
    lane_lo = lax.broadcasted_iota(jnp.int32, (C_CHUNK, LANE), 1) < 256 // C_GROUPS
    for c in range(tm // C_CHUNK):
        oc_ref[c * C_CHUNK:(c + 1) * C_CHUNK, :] = _gmlp_chunk(pc[c * C_CHUNK:(c + 1) * C_CHUNK], gws_ref, gbs_ref,
                                                                gnw_ref, gnb_ref, lane_lo)
    if len(xp_ref.shape) == 2:
        halo = jnp.concatenate([xp_ref[...], xn_ref[...]], axis=0)
    else:
        hr = xp_ref.shape[1]
        halo = jnp.concatenate([xp_ref[0, hr - 8:hr, :], xn_ref[0, 0:8, :]], axis=0)
    pb_halo = jnp.dot(_norm_mod(halo, nw, shift, scale).astype(BF16), wb_ref[...],
                      preferred_element_type=F32) + bb_ref[...]
    row = lax.broadcasted_iota(jnp.int32, pb.shape, 0)
    pos = (pl.program_id(0) * tm + row) & (seq_len - 1)
    prev = jnp.where(row == 0, pb_halo[7:8, :], pltpu.roll(pb, 1, axis=0))
    nxt = jnp.where(row == tm - 1, pb_halo[8:9, :], pltpu.roll(pb, tm - 1, axis=0))
    prev = jnp.where(pos == 0, 0.0, prev)
    nxt = jnp.where(pos == seq_len - 1, 0.0, nxt)
    y = prev * sw_ref[0:1, :] + pb * sw_ref[1:2, :] + nxt * sw_ref[2:3, :] + sb_ref[...]
    ox0_ref[...] = y[:, 0:256]
    ou_ref[...] = y[:, 256:512] * y[:, 512:768]


def _in_full_call(x2d, mod, nw, ws, bs, conv_w, conv_b, gws, gbs, gnw, gnb, rows_per_mod, seq_len, tm,
                  col_view=None):
    r, d = x2d.shape
    assert seq_len & (seq_len - 1) == 0 and tm % C_CHUNK == 0
    blocks_per_mod = rows_per_mod // tm
    c2 = lambda i: (0, 0)
    full = lambda a: pl.BlockSpec(a.shape, (lambda i: (0,) * a.ndim))
    row = lambda w: pl.BlockSpec((tm, w), lambda i: (i, 0))
    x_spec, prev_spec, next_spec, view = _tile_specs(tm, d, col_view)
    in_specs = [x_spec, prev_spec(r), next_spec(r),
                pl.BlockSpec((1, 1, mod.shape[-1]), lambda i: (i // blocks_per_mod, 0, 0)),
                pl.BlockSpec((1, d), c2)]
    consts = list(ws) + list(bs) + [conv_w, conv_b, gws, gbs, gnw, gnb]
    in_specs += [full(a) for a in consts]
    widths = (ws[0].shape[1], 256, 256, 256, ws[3].shape[1])
    return pl.pallas_call(
        functools.partial(_in_full_kernel, d=d, seq_len=seq_len),
        grid=(r // tm,),
        in_specs=in_specs,
        out_specs=[row(w) for w in widths],
        out_shape=[jax.ShapeDtypeStruct((r, w), F32) for w in widths],
        compiler_params=_cparams(1),
        name="in_proj_full",
    )(view(x2d), view(x2d), view(x2d), mod, nw, *consts)


def _mlstm_ones(L):
    ones_blk = np.zeros((2 * L, LANE), np.float32)
    ones_blk[:L, :HEAD_DIM] = 1.0
    ones_blk[L:, HEAD_DIM:] = 1.0
    return ones_blk


def _running(op, fill, x, row, reverse):
    L = x.shape[0]
    sh = 1
    while sh < L:
        if reverse:
            shifted, valid = pltpu.roll(x, L - sh, axis=0), row < L - sh
        else:
            shifted, valid = pltpu.roll(x, sh, axis=0), row >= sh
        x = op(x, jnp.where(valid, shifted, fill))
        sh *= 2
    return x


def _mlstm_chunk(blk, ones_ref, cn_ref, m_ref, d, mask, row, lane_lo, sub_lo, bd):
    L = blk.shape[0]
    reverse = d == 1
    end = 0 if reverse else L - 1
    cum = _running(jnp.add, 0.0, _log_sigmoid(blk[:, 1152:1280]), row, reverse)
    a = blk[:, 1024:1152] - cum
    m_prev = m_ref[d:d + 1, :]
    g = jnp.maximum(_running(jnp.maximum, NEG_BIG, a, row, reverse), m_prev)
    g_end = g[end:end + 1, :]
    w_inter = jnp.exp(m_prev - g)
    e_negm = jnp.exp(-(cum + g))
    m_ref[d:d + 1, :] = cum[end:end + 1, :] + g_end
    a_t = a.T

    def head_lanes(x, pair):
        c = 4 * d + 2 * pair
        return jnp.where(lane_lo, jnp.broadcast_to(x[:, c:c + 1], (L, LANE)),
                         jnp.broadcast_to(x[:, c + 1:c + 2], (L, LANE)))

    outs = []
    for pair in range(HEADS // 2):
        cs = slice(128 * pair, 128 * pair + 128)
        q_pair = blk[:, cs].astype(BF16)
        k_pair = blk[:, 256 + 128 * pair:256 + 128 * pair + 128] * (HEAD_DIM ** -0.5)
        v_pair = blk[:, 512 + 128 * pair:512 + 128 * pair + 128]
        k_b = k_pair.astype(BF16)
        zero = jnp.zeros_like(k_b)
        k_rows = jnp.concatenate([jnp.where(lane_lo, k_b, zero), jnp.where(lane_lo, zero, k_b)], axis=0)
        s_pair = lax.dot_general(q_pair, k_rows, (((1,), (1,)), ((), ())), preferred_element_type=F32)
        w_halves = []
        for sub in range(2):
            hd = 2 * pair + sub
            c = 4 * d + hd
            expo = jnp.where(mask, a_t[c:c + 1, :] - jnp.broadcast_to(g[:, c:c + 1], (L, L)), NEG_BIG)
            w_halves.append((jnp.exp(expo) * s_pair[:, sub * L:(sub + 1) * L]).astype(BF16))
        w_pair = jnp.concatenate(w_halves, axis=1)
        v_b = v_pair.astype(BF16)
        v_rows = jnp.concatenate([jnp.where(lane_lo, v_b, zero), jnp.where(lane_lo, zero, v_b)], axis=0)
        intra = jnp.dot(w_pair, jnp.concatenate([v_rows, ones_ref[...]], axis=1), preferred_element_type=F32)
        slot = 2 * d + pair
        cn = cn_ref[slot]
        inter = jnp.dot(q_pair, cn.astype(BF16), preferred_element_type=F32)
        wi_p = head_lanes(w_inter, pair)
        num = wi_p * inter[:, 0:128] + intra[:, 0:128]
        den = wi_p * inter[:, 128:256] + intra[:, 128:256]
        outs.append(num / jnp.maximum(jnp.abs(den), head_lanes(e_negm, pair)))
        c_e = 4 * d + 2 * pair
        w_st = jnp.exp(jnp.where(sub_lo, a_t[c_e:c_e + 1, :] - g_end[:, c_e:c_e + 1],
                                 a_t[c_e + 1:c_e + 2, :] - g_end[:, c_e + 1:c_e + 2]))
        ktw = (k_pair.T * w_st).astype(BF16)
        upd = jnp.dot(ktw, jnp.concatenate([v_b, jnp.ones_like(v_b)], axis=1), preferred_element_type=F32)
        decay = wi_p[end:end + 1, :]
        cn_ref[slot] = jnp.concatenate([decay, decay], axis=1) * cn + jnp.where(bd, upd, 0.0)
    return jnp.concatenate(outs, axis=1)


def _mlstm_kernel(*refs, need_ctx):
    ctx_ref, lat_ref, ones_ref, nw_ref = refs[:4]
    if need_ctx:
        yc_ref, yl_ref, ybc_ref, ybl_ref, cn_ref, m_ref = refs[4:]
    else:
        yl_ref, ybc_ref, ybl_ref, cn_ref, m_ref = refs[4:]
        yc_ref = None
    L = A_CHUNK
    lane_lo = lax.broadcasted_iota(jnp.int32, (L, LANE), 1) < HEAD_DIM
    row = lax.broadcasted_iota(jnp.int32, (L, LANE), 0)
    sub_lo = row < HEAD_DIM
    r2 = lax.broadcasted_iota(jnp.int32, (L, L), 0)
    c2 = lax.broadcasted_iota(jnp.int32, (L, L), 1)
    mask_f, mask_b = r2 >= c2, r2 <= c2
    rb = lax.broadcasted_iota(jnp.int32, (LANE, 2 * LANE), 0) < HEAD_DIM
    cb = (lax.broadcasted_iota(jnp.int32, (LANE, 2 * LANE), 1) % LANE) < HEAD_DIM
    bd = rb == cb
    cn_ref[...] = jnp.zeros_like(cn_ref)
    m_ref[...] = jnp.zeros_like(m_ref)

    def scan(src_ref, yf_ref, yb_ref):
        n_chunks = src_ref.shape[1] // L

        def body(j, carry):
            sf = pl.multiple_of(j * L, L)
            sb = pl.multiple_of((n_chunks - 1 - j) * L, L)
            hf = _mlstm_chunk(src_ref[0, pl.ds(sf, L), :], ones_ref, cn_ref, m_ref, 0, mask_f, row, lane_lo, sub_lo, bd)
            hb = _mlstm_chunk(src_ref[0, pl.ds(sb, L), :], ones_ref, cn_ref, m_ref, 1, mask_b, row, lane_lo, sub_lo, bd)
            if yf_ref is not None:
                yf_ref[0, pl.ds(sf, L), :] = hf
                yb_ref[pl.ds(sb, L), :] = hb
            return carry

        lax.fori_loop(0, n_chunks, body, 0)

    def finalize(src_ref, y_ref, yb_ref):
        n_chunks = src_ref.shape[1] // L

        def body(j, carry):
            s = pl.multiple_of(j * L, L)
            for pair in range(HEADS // 2):
                cs = slice(128 * pair, 128 * pair + 128)
                y = y_ref[0, pl.ds(s, L), cs] + yb_ref[pl.ds(s, L), cs]
                y2 = y * y
                s_lo = jnp.sum(jnp.where(lane_lo, y2, 0.0), axis=-1, keepdims=True)
                s_all = jnp.sum(y2, axis=-1, keepdims=True)
                ms = jnp.where(lane_lo, s_lo, s_all - s_lo) * (1.0 / HEAD_DIM)
                o_gate = src_ref[0, pl.ds(s, L), 768 + 128 * pair:768 + 128 * pair + 128]
                y_ref[0, pl.ds(s, L), cs] = y * lax.rsqrt(ms + NORM_EPS) * nw_ref[:, cs] * _sigmoid(o_gate)
            return carry

        lax.fori_loop(0, n_chunks, body, 0)

    scan(ctx_ref, yc_ref, ybc_ref)
    scan(lat_ref, yl_ref, ybl_ref)
    if need_ctx:
        finalize(ctx_ref, yc_ref, ybc_ref)
    finalize(lat_ref, yl_ref, ybl_ref)


def _mlstm_call(pa_ctx, pa_lat, nw, need_ctx):
    b, nc, wa = pa_ctx.shape
    n = pa_lat.shape[1]
    L = A_CHUNK
    ones_blk = jnp.asarray(_mlstm_ones(L), BF16)
    c2 = lambda i: (0, 0)
    c3 = lambda i: (0, 0, 0)
    out_specs = [pl.BlockSpec((1, n, 256), lambda i: (i, 0, 0))]
    out_shape = [jax.ShapeDtypeStruct((b, n, 256), F32)]
    if need_ctx:
        out_specs.insert(0, pl.BlockSpec((1, nc, 256), lambda i: (i, 0, 0)))
        out_shape.insert(0, jax.ShapeDtypeStruct((b, nc, 256), F32))
    res = pl.pallas_call(
        functools.partial(_mlstm_kernel, need_ctx=need_ctx),
        grid=(b,),
        in_specs=[pl.BlockSpec((1, nc, wa), lambda i: (i, 0, 0)),
                  pl.BlockSpec((1, n, wa), lambda i: (i, 0, 0)),
                  pl.BlockSpec(ones_blk.shape, c2),
                  pl.BlockSpec((1, 256), c2)],
        out_specs=out_specs,
        out_shape=out_shape,
        scratch_shapes=[pltpu.VMEM((nc, 256), F32), pltpu.VMEM((n, 256), F32),
                        pltpu.VMEM((HEADS, LANE, 2 * LANE), F32), pltpu.VMEM((8, LANE), F32)],
        compiler_params=_cparams(1),
        name="mlstm",
    )(pa_ctx, pa_lat, ones_blk, nw)
    return (res[0], res[1]) if need_ctx else (None, res[0])


D_FAST_BASE = 16
D_FAST_MAX_EXPONENT = 60.0


def _hgrn_tables(L, base):
    t = np.arange(L)
    lev0 = int(math.log2(base))
    n_split = int(math.log2(L)) - lev0
    same_base = (t[:, None] // base) == (t[None, :] // base)
    lvl_f = np.where(same_base & (t[:, None] >= t[None, :]), 0, -1)
    lvl_b = np.where(same_base & (t[:, None] <= t[None, :]), 0, -1)
    mids_f, mids_b = [], []
    for i in range(n_split):
        half, size = base << i, base << (i + 1)
        same = (t[:, None] // size) == (t[None, :] // size)
        upper = (t // half) % 2 == 1
        lvl_f = np.where(same & upper[:, None] & ~upper[None, :], i + 1, lvl_f)
        lvl_b = np.where(same & ~upper[:, None] & upper[None, :], i + 1, lvl_b)
        starts = np.arange(0, L, size)
        mids_f.append([(int(s), size, int(s) + half - 1) for s in starts])
        mids_b.append([(int(s), size, int(s) + half) for s in starts])
    wide = lambda m: np.concatenate([m, m], axis=-1).astype(np.int32)
    return wide(lvl_f), wide(lvl_b), mids_f, mids_b


def _rows_of(b, spec):
    return jnp.concatenate([jnp.broadcast_to(b[r:r + 1, :], (n, b.shape[1])) for _, n, r in spec], axis=0)


def _hgrn_chunk(blk, f_pre, lb, lvl_ref, mids, s_ref, x_ref, d, row, lane_lo, bd, base):
    L = blk.shape[0]
    reverse = d == 1
    end = 0 if reverse else L - 1
    qs = _silu(blk[:, 0:256])
    v = blk[:, 256:512]
    sig, sig_neg = _sigmoid_both(f_pre)
    log_f = jnp.log(jnp.maximum(lb + (1.0 - lb) * sig, F_EPS))
    k = (1.0 - lb) * sig_neg
    b = _running(jnp.add, 0.0, log_f, row, reverse)
    b_end = b[end:end + 1, :]
    q_in = qs * jnp.exp(b)
    k_out = k * jnp.exp(b_end - b)
    decay = jnp.exp(b_end)
    factors = []
    if base > 1:
        zero_row = jnp.zeros((1, b.shape[1]), F32)
        if reverse:
            refs = [b[s + base:s + base + 1, :] if s + base < L else zero_row for s in range(0, L, base)]
        else:
            refs = [b[s - 1:s, :] if s > 0 else zero_row for s in range(0, L, base)]
        r0 = jnp.concatenate([jnp.broadcast_to(r, (base, b.shape[1])) for r in refs], axis=0)
        factors.append((jnp.exp(b - r0), jnp.exp(r0 - b)))
        last = [b[s:s + 1, :] if reverse else b[s + base - 1:s + base, :] for s in range(0, L, base)]
        worst = functools.reduce(jnp.maximum, [r - e for r, e in zip(refs, last)])
        x_ref[d:d + 1, :] = jnp.maximum(x_ref[d:d + 1, :], worst)
    else:
        factors.append((None, None))
    for spec in mids:
        e = jnp.exp(-jnp.abs(b - _rows_of(b, spec)))
        factors.append((e, e))
    nt = (((1,), (1,)), ((), ()))
    lvl = lvl_ref[...]
    outs = []
    for pair in range(HEADS // 2):
        cs = slice(128 * pair, 128 * pair + 128)
        q_p, k_p = qs[:, cs], k[:, cs]
        zero = jnp.zeros((L, LANE), BF16)
        v_b = v[:, cs].astype(BF16)
        v_rows = jnp.concatenate([jnp.where(lane_lo, v_b, zero), jnp.where(lane_lo, zero, v_b)], axis=0)
        scores = jnp.zeros((L, 2 * L), F32)
        for i, (eq, ek) in enumerate(factors):
            q_l = (q_p if eq is None else q_p * eq[:, cs]).astype(BF16)
            k_l = (k_p if ek is None else k_p * ek[:, cs]).astype(BF16)
            k_rows = jnp.concatenate([jnp.where(lane_lo, k_l, zero), jnp.where(lane_lo, zero, k_l)], axis=0)
            scores = jnp.where(lvl == i, lax.dot_general(q_l, k_rows, nt, preferred_element_type=F32), scores)
        slot = 2 * d + pair
        s_t = s_ref[slot]
        outs.append(jnp.dot(scores.astype(BF16), v_rows, preferred_element_type=F32)
                    + lax.dot_general(q_in[:, cs].astype(BF16), s_t.astype(BF16), nt, preferred_element_type=F32))
        upd = lax.dot_general(v_b, k_out[:, cs].astype(BF16), (((0,), (0,)), ((), ())), preferred_element_type=F32)
        s_ref[slot] = decay[:, cs] * s_t + jnp.where(bd, upd, 0.0)
    return jnp.concatenate(outs, axis=1)


def _hgrn_kernel(*refs, need_ctx, layer, mids_fast, mids_safe):
    ctx_ref, lat_ref, lvlf_fast_ref, lvlb_fast_ref, lvlf_safe_ref, lvlb_safe_ref, lbl_ref, nw_ref = refs[:8]
    if need_ctx:
        yc_ref, yl_ref, ybc_ref, ybl_ref, s_ref, x_ref = refs[8:]
    else:
        yl_ref, ybc_ref, ybl_ref, s_ref, x_ref = refs[8:]
        yc_ref = None
    L = D_CHUNK
    lane_lo = lax.broadcasted_iota(jnp.int32, (L, LANE), 1) < HEAD_DIM
    row = lax.broadcasted_iota(jnp.int32, (L, 2 * LANE), 0)
    bd = ((lax.broadcasted_iota(jnp.int32, (LANE, LANE), 0) < HEAD_DIM)
          == (lax.broadcasted_iota(jnp.int32, (LANE, LANE), 1) < HEAD_DIM))
    logits = lbl_ref[...]
    e = jnp.exp(logits - jnp.max(logits, axis=0, keepdims=True))
    prob = e / jnp.sum(e, axis=0, keepdims=True)
    lb = jnp.sum(prob[0:layer + 1], axis=0, keepdims=True) - prob[0:1]
    def scan(src_ref, yf_ref, yb_ref, lvlf_ref, lvlb_ref, mids, base):
        n_chunks = src_ref.shape[1] // L

        def body(j, carry):
            sf = pl.multiple_of(j * L, L)
            sb = pl.multiple_of((n_chunks - 1 - j) * L, L)
            blk_f = src_ref[0, pl.ds(sf, L), :]
            blk_b = src_ref[0, pl.ds(sb, L), :]
            of = _hgrn_chunk(blk_f, blk_f[:, 512:768], lb, lvlf_ref, mids[0], s_ref, x_ref, 0, row, lane_lo, bd, base)
            ob = _hgrn_chunk(blk_b, blk_b[:, 768:1024], lb, lvlb_ref, mids[1], s_ref, x_ref, 1, row, lane_lo, bd, base)
            if yf_ref is not None:
                yf_ref[0, pl.ds(sf, L), :] = of
                yb_ref[pl.ds(sb, L), :] = ob
            return carry

        lax.fori_loop(0, n_chunks, body, 0)

    def scans(lvlf_ref, lvlb_ref, mids, base):
        s_ref[...] = jnp.zeros_like(s_ref)
        scan(ctx_ref, yc_ref, ybc_ref, lvlf_ref, lvlb_ref, mids, base)
        scan(lat_ref, yl_ref, ybl_ref, lvlf_ref, lvlb_ref, mids, base)

    x_ref[...] = jnp.zeros_like(x_ref)
    scans(lvlf_fast_ref, lvlb_fast_ref, mids_fast, D_FAST_BASE)

    @pl.when(jnp.logical_not(jnp.max(x_ref[...]) <= D_FAST_MAX_EXPONENT))
    def _():
        scans(lvlf_safe_ref, lvlb_safe_ref, mids_safe, 1)

    def finalize(src_ref, y_ref, yb_ref):
        n_chunks = src_ref.shape[1] // L

        def body(j, carry):
            s = pl.multiple_of(j * L, L)
            for pair in range(HEADS // 2):
                cs = slice(128 * pair, 128 * pair + 128)
                y = y_ref[0, pl.ds(s, L), cs] + yb_ref[pl.ds(s, L), cs]
                y2 = y * y
                s_lo = jnp.sum(jnp.where(lane_lo, y2, 0.0), axis=-1, keepdims=True)
                s_all = jnp.sum(y2, axis=-1, keepdims=True)
                ms = jnp.where(lane_lo, s_lo, s_all - s_lo) * (1.0 / HEAD_DIM)
                g = src_ref[0, pl.ds(s, L), 1024 + 128 * pair:1024 + 128 * pair + 128]
                y_ref[0, pl.ds(s, L), cs] = y * lax.rsqrt(ms + NORM_EPS) * nw_ref[:, cs] * _silu(g)
            return carry

        lax.fori_loop(0, n_chunks, body, 0)

    if need_ctx:
        finalize(ctx_ref, yc_ref, ybc_ref)
    finalize(lat_ref, yl_ref, ybl_ref)


def _hgrn_call(pd_ctx, pd_lat, lb_logits, nw, layer, need_ctx):
    b, nc, wd = pd_ctx.shape
    n = pd_lat.shape[1]
    L = D_CHUNK
    lvlf_fast, lvlb_fast, midsf_fast, midsb_fast = _hgrn_tables(L, D_FAST_BASE)
    lvlf_safe, lvlb_safe, midsf_safe, midsb_safe = _hgrn_tables(L, 1)
    tables = [jnp.asarray(a) for a in (lvlf_fast, lvlb_fast, lvlf_safe, lvlb_safe)]
    c2 = lambda i: (0, 0)
    out_specs = [pl.BlockSpec((1, n, 256), lambda i: (i, 0, 0))]
    out_shape = [jax.ShapeDtypeStruct((b, n, 256), F32)]
    if need_ctx:
        out_specs.insert(0, pl.BlockSpec((1, nc, 256), lambda i: (i, 0, 0)))
        out_shape.insert(0, jax.ShapeDtypeStruct((b, nc, 256), F32))
    res = pl.pallas_call(
        functools.partial(_hgrn_kernel, need_ctx=need_ctx, layer=layer,
                          mids_fast=(midsf_fast, midsb_fast), mids_safe=(midsf_safe, midsb_safe)),
        grid=(b,),
        in_specs=[pl.BlockSpec((1, nc, wd), lambda i: (i, 0, 0)),
                  pl.BlockSpec((1, n, wd), lambda i: (i, 0, 0))]
                 + [pl.BlockSpec(a.shape, c2) for a in tables]
                 + [pl.BlockSpec(lb_logits.shape, c2), pl.BlockSpec((1, 256), c2)],
        out_specs=out_specs,
        out_shape=out_shape,
        scratch_shapes=[pltpu.VMEM((nc, 256), F32), pltpu.VMEM((n, 256), F32),
                        pltpu.VMEM((HEADS, LANE, LANE), F32), pltpu.VMEM((8, 256), F32)],
        compiler_params=_cparams(1),
        name="hgrn2",
    )(pd_ctx, pd_lat, *tables, lb_logits, nw)
    return (res[0], res[1]) if need_ctx else (None, res[0])


def _filter_feats(n):
    t = np.linspace(0.0, 1.0, n, dtype=np.float32)[:, None]
    bands = np.linspace(1e-4, B_BANDS - 1, B_BANDS, dtype=np.float32)[None]
    ang = (np.float32(2 * math.pi) * bands * np.arange(n, dtype=np.float32)[:, None] / np.float32(n)).astype(np.float32)
    z = np.concatenate([t, np.cos(ang), -np.sin(ang)], axis=-1).astype(np.float32)
    deltas = np.abs(np.linspace(math.log(B_DECAY_TARGET) / B_SLOW_DECAY,
                                math.log(B_DECAY_TARGET) / B_FAST_DECAY, 256, dtype=np.float32))
    neg_t_deltas = (-t * deltas[None]).astype(np.float32)
    return z, neg_t_deltas


def _filter_kernel(z_ref, ntd_ref, w1_ref, b1_ref, f1_ref, w2_ref, b2_ref, f2_ref, w3_ref, flip_ref, o_ref):
    n = z_ref.shape[0]
    T = flip_ref.shape[0]
    hd = jnp.sin(f1_ref[...] * (_dot_f32(z_ref[...], w1_ref[...]) + b1_ref[...]))
    hd = jnp.sin(f2_ref[...] * (_dot_f32(hd, w2_ref[...]) + b2_ref[...]))
    decay = jnp.exp(ntd_ref[...])
    fwd = _dot_f32(hd, w3_ref[:, 0:256]) * decay
    bwd = _dot_f32(hd, w3_ref[:, 256:512]) * decay
    flip = flip_ref[...]
    rev = []
    for i in range(n // T):
        parts = _split3(bwd[n - (i + 1) * T:n - i * T])
        rev.append(sum(jnp.dot(flip, p, preferred_element_type=F32) for p in parts))
    rev = jnp.concatenate(rev, axis=0)
    row = lax.broadcasted_iota(jnp.int32, rev.shape, 0)
    o_ref[0:n, :] = jnp.where(row == 0, 0.0, pltpu.roll(rev, 1, axis=0))
    o_ref[n:2 * n, :] = fwd


def _filter_call(n, w1, b1, f1, w2, b2, f2, w3):
    z, ntd = _filter_feats(n)
    k_pad = 32
    z = np.pad(z, ((0, 0), (0, k_pad - z.shape[1])))
    w1 = jnp.pad(w1, ((0, k_pad - w1.shape[0]), (0, 0)))
    flip = jnp.asarray(np.eye(CONV_BLOCK, dtype=np.float32)[::-1], BF16)
    args = (jnp.asarray(z), jnp.asarray(ntd), w1, b1, f1, w2, b2, f2, w3, flip)
    return pl.pallas_call(
        _filter_kernel,
        grid=(1,),
        in_specs=[pl.BlockSpec(a.shape, lambda i: (0, 0)) for a in args],
        out_specs=pl.BlockSpec((2 * n, 256), lambda i: (0, 0)),
        out_shape=jax.ShapeDtypeStruct((2 * n, 256), F32),
        compiler_params=_cparams(1),
        name="hyena_filter",
    )(*args)


def _long_conv_kernel(u_ref, k_ref, o_ref, *, n_blk, bsz, ch_per_step):
    T = CONV_BLOCK
    half = T // 2
    for c in range(ch_per_step):
        taps = k_ref[c]
        rolled = pltpu.roll(jnp.broadcast_to(taps, (half, taps.shape[1])), 0, axis=1, stride=1, stride_axis=0)
        acc = [None] * n_blk
        for d in range(-(n_blk - 1), n_blk):
            base = (d + n_blk) * T
            tile = jnp.concatenate([rolled[:, base:base + T], rolled[:, base - half:base - half + T]],
                                   axis=0).astype(BF16)
            j_lo, j_hi = max(0, -d), min(n_blk, n_blk - d)
            lhs = u_ref[c, j_lo * bsz:j_hi * bsz, :].astype(BF16)
            res = jnp.dot(lhs, tile, preferred_element_type=F32)
            for j in range(j_lo, j_hi):
                part = res[(j - j_lo) * bsz:(j - j_lo + 1) * bsz]
                i = j + d
                acc[i] = part if acc[i] is None else acc[i] + part
        o_ref[c] = jnp.concatenate(acc, axis=0)


def _long_conv_call(u_t, k_t, n_blk, bsz, ch_per_step):
    ch, rows, T = u_t.shape
    return pl.pallas_call(
        functools.partial(_long_conv_kernel, n_blk=n_blk, bsz=bsz, ch_per_step=ch_per_step),
        grid=(ch // ch_per_step,),
        in_specs=[pl.BlockSpec((ch_per_step, rows, T), lambda i: (i, 0, 0)),
                  pl.BlockSpec((ch_per_step, 1, k_t.shape[-1]), lambda i: (i, 0, 0))],
        out_specs=pl.BlockSpec((ch_per_step, rows, T), lambda i: (i, 0, 0)),
        out_shape=jax.ShapeDtypeStruct((ch, rows, T), F32),
        compiler_params=_cparams(1),
        name="hyena_long_conv",
    )(u_t, k_t)


def _hyena_conv(u, filt):
    bsz, n, ch = u.shape
    T = CONV_BLOCK
    n_blk = n // T
    k_t = filt.T.reshape(ch, 1, 2 * n)
    u_t = u.reshape(bsz, n_blk, T, ch).transpose(3, 1, 0, 2).reshape(ch, n_blk * bsz, T)
    y_t = _long_conv_call(u_t, k_t, n_blk, bsz, 4)
    return y_t.reshape(ch, n_blk, bsz, T).transpose(2, 1, 3, 0).reshape(bsz, n, ch)


def _merge_kernel(x_ref, mod_ref, nw_ref, ya_ref, x0_ref, u_ref, yconv_ref, yc_ref, yd_ref, hyb_ref,
                  wg_ref, wb_ref, wo_ref, o_ref, *, d):
    x = _load_tile(x_ref, d)
    h = _norm_mod(x, nw_ref[...], mod_ref[0, :, 0:d], mod_ref[0, :, d:2 * d]).astype(BF16)
    y_b = x0_ref[...] * (yconv_ref[...] + u_ref[...] * hyb_ref[...])
    ys = (ya_ref[...], y_b, yc_ref[...], yd_ref[...])
    acc = None
    for j in range(N_BRANCH):
        gate = _sigmoid(jnp.dot(h, wg_ref[j], preferred_element_type=F32))
        term = gate * jnp.dot(ys[j].astype(BF16), wb_ref[j], preferred_element_type=F32)
        acc = term if acc is None else acc + term
    y = jnp.dot(acc.astype(BF16), wo_ref[...], preferred_element_type=F32)
    _store_tile(o_ref, x + mod_ref[0, :, 2 * d:3 * d] * y, d)


def _merge_call(x2d, mod, nw, ya, x0, u, yconv, yc, yd, hyb, wg, wb, wo, rows_per_mod, tm, col_view=None):
    r, d = x2d.shape
    blocks_per_mod = rows_per_mod // tm
    c2 = lambda i: (0, 0)
    c3 = lambda i: (0, 0, 0)
    row = lambda w: pl.BlockSpec((tm, w), lambda i: (i, 0))
    x_spec, _, _, view = _tile_specs(tm, d, col_view)
    xv = view(x2d)
    return pl.pallas_call(
        functools.partial(_merge_kernel, d=d),
        grid=(r // tm,),
        in_specs=[x_spec, pl.BlockSpec((1, 1, mod.shape[-1]), lambda i: (i // blocks_per_mod, 0, 0)),
                  pl.BlockSpec((1, d), c2), row(256), row(256), row(256), row(256), row(256), row(256),
                  pl.BlockSpec((1, 256), c2),
                  pl.BlockSpec(wg.shape, c3), pl.BlockSpec(wb.shape, c3), pl.BlockSpec(wo.shape, c2)],
        out_specs=x_spec,
        out_shape=jax.ShapeDtypeStruct(xv.shape, F32),
        compiler_params=_cparams(1),
        name="merge",
    )(xv, mod, nw, ya, x0, u, yconv, yc, yd, hyb, wg, wb, wo).reshape(r, d)


def _ffn_kernel(x_ref, mod_ref, nw_ref, wa_ref, wg_ref, wd_ref, fnw_ref, o_ref, *, d, hid_chunk, final_norm):
    x = x_ref[...]
    h = _norm_mod(x, nw_ref[...], mod_ref[0, :, 3 * d:4 * d], mod_ref[0, :, 4 * d:5 * d]).astype(BF16)
    hidden = wa_ref.shape[1]
    acc = None
    for c in range(hidden // hid_chunk):
        cs = slice(c * hid_chunk, (c + 1) * hid_chunk)
        a = jnp.dot(h, wa_ref[:, cs], preferred_element_type=F32)
        g = jnp.dot(h, wg_ref[:, cs], preferred_element_type=F32)
        term = jnp.dot((_silu(a) * g).astype(BF16), wd_ref[cs, :], preferred_element_type=F32)
        acc = term if acc is None else acc + term
    y = x + mod_ref[0, :, 5 * d:6 * d] * acc
    if final_norm:
        y = y * lax.rsqrt(jnp.mean(y * y, axis=-1, keepdims=True) + NORM_EPS) * fnw_ref[...]
    o_ref[...] = y


def _ffn_call(x2d, mod, nw, wa, wg, wd, fnw, rows_per_mod, tm, final_norm):
    r, d = x2d.shape
    blocks_per_mod = rows_per_mod // tm
    c2 = lambda i: (0, 0)
    return pl.pallas_call(
        functools.partial(_ffn_kernel, d=d, hid_chunk=256, final_norm=final_norm),
        grid=(r // tm,),
        in_specs=[pl.BlockSpec((tm, d), lambda i: (i, 0)),
                  pl.BlockSpec((1, 1, mod.shape[-1]), lambda i: (i // blocks_per_mod, 0, 0)),
                  pl.BlockSpec((1, d), c2),
                  pl.BlockSpec(wa.shape, c2), pl.BlockSpec(wg.shape, c2), pl.BlockSpec(wd.shape, c2),
                  pl.BlockSpec((1, d), c2)],
        out_specs=pl.BlockSpec((tm, d), lambda i: (i, 0)),
        out_shape=jax.ShapeDtypeStruct((r, d), F32),
        compiler_params=_cparams(1),
        name="ffn",
    )(x2d, mod, nw, wa, wg, wd, fnw)


def _row_tile(rows):
    return 512 if rows % 512 == 0 else 256


def kernel(x, c, ctx, c_ctx, ada_w, ada_b, norm1_w, norm2_w, w_in, b_in, mlstm_norm_w, hy_short_w, hy_short_b, hy_w1, hy_b1, hy_freq1, hy_w2, hy_b2, hy_freq2, hy_w3, hy_bias, gm_norm_w, gm_norm_b, gm_ws, gm_bs, hg_lb_logits, hg_norm_w, w_gate, w_branch, w_out, w_ffn_in, w_ffn_out, final_norm_w):
    bsz, n, d = x.shape
    nc = ctx.shape[1]
    depth = ada_w.shape[0]
    rows = n // GRID_W
    mw = d // N_BRANCH
    hidden = w_ffn_out.shape[1]
    assert mw == 256 and mw == HEADS * HEAD_DIM

    mb = -(-(bsz + 1) // 8) * 8
    cvec = jnp.concatenate([c, c_ctx[None], jnp.zeros((mb - bsz - 1, d), F32)], axis=0)
    mods = _ada_call(cvec, ada_w, ada_b)

    s_a, s_g, s_b, s_c = 4 * mw, 4 * mw + 4 * HEADS, 7 * mw + 4 * HEADS, 9 * mw + 4 * HEADS
    gate_pad = LANE - 2 * HEADS

    xc = ctx
    for l in range(depth):
        need_ctx = l < depth - 1
        col_major = l % 2 == 1
        col_view = (rows, GRID_W) if col_major else None
        mod_lat = mods[l, :bsz].reshape(bsz, 1, 6 * d)
        mod_ctx = mods[l, bsz:bsz + 1].reshape(1, 1, 6 * d)
        wl, bl = w_in[l], b_in[l]
        wgt, bgt = wl[:, s_a:s_g].reshape(d, 4, HEADS), bl[s_a:s_g].reshape(4, HEADS)
        zw, zb = jnp.zeros((d, gate_pad), F32), jnp.zeros((gate_pad,), F32)
        w_segs = [jnp.concatenate([wl[:, :s_a], wgt[:, 0], wgt[:, 2], zw, wgt[:, 1], wgt[:, 3], zw], axis=1),
                  wl[:, s_g:s_b], wl[:, s_b:s_c], wl[:, s_c:]]
        b_segs = [jnp.concatenate([bl[:s_a], bgt[0], bgt[2], zb, bgt[1], bgt[3], zb]),
                  bl[s_g:s_b], bl[s_b:s_c], bl[s_c:]]
        w_segs = [w.astype(BF16) for w in w_segs]
        b_segs = [b.reshape(1, -1) for b in b_segs]
        nw1 = norm1_w[l].reshape(1, d)

        x2d = x.reshape(bsz * n, d)
        xc2d = xc.reshape(bsz * nc, d)
        hy_args = (hy_w1[l], hy_b1[l].reshape(1, -1), hy_freq1[l].reshape(1, -1), hy_w2[l],
                   hy_b2[l].reshape(1, -1), hy_freq2[l].reshape(1, -1), hy_w3[l])
        front = (hy_short_w[l], hy_short_b[l].reshape(1, -1), gm_ws[l].astype(BF16),
                 jnp.repeat(gm_bs[l].T, mw // C_GROUPS, axis=1), gm_norm_w[l].reshape(1, mw), gm_norm_b[l].reshape(1, mw))
        pa, x0, u, ycm, pd = _in_full_call(x2d, mod_lat, nw1, w_segs, b_segs, *front, n, n, _row_tile(n), col_view)
        if need_ctx:
            ca, x0c, uc, ycm_c, cd = _in_full_call(xc2d, mod_ctx, nw1, w_segs, b_segs, *front, bsz * nc, nc,
                                                   _row_tile(bsz * nc))
        else:
            ca, cd = _in_call(xc2d, mod_ctx, nw1, [w_segs[0], w_segs[3]], [b_segs[0], b_segs[3]],
                              bsz * nc, _row_tile(bsz * nc))

        yac, ya = _mlstm_call(ca.reshape(bsz, nc, -1), pa.reshape(bsz, n, -1),
                              mlstm_norm_w[l].reshape(1, mw), need_ctx)
        ydc, yd = _hgrn_call(cd.reshape(bsz, nc, -1), pd.reshape(bsz, n, -1), hg_lb_logits,
                             hg_norm_w[l].reshape(1, mw), l, need_ctx)

        hyb = hy_bias[l].reshape(1, mw)
        wg, wb, wo = w_gate[l].astype(BF16), w_branch[l].astype(BF16), w_out[l].astype(BF16)
        wfa, wfg = w_ffn_in[l][:, :hidden].astype(BF16), w_ffn_in[l][:, hidden:].astype(BF16)
        wfd = w_ffn_out[l].astype(BF16)
        nw2 = norm2_w[l].reshape(1, d)
        fnw = final_norm_w.reshape(1, d)

        yconv = _hyena_conv(u.reshape(bsz, n, mw), _filter_call(n, *hy_args))
        x2d = _merge_call(x2d, mod_lat, nw1, ya.reshape(bsz * n, mw), x0, u, yconv.reshape(bsz * n, mw), ycm,
                          yd.reshape(bsz * n, mw), hyb, wg, wb, wo, n, _row_tile(n), col_view)
        x2d = _ffn_call(x2d, mod_lat, nw2, wfa, wfg, wfd, fnw, n, _row_tile(n), l == depth - 1)
        x = x2d.reshape(bsz, n, d)

        if need_ctx:
            tmc = _row_tile(bsz * nc)
            yconv_c = _hyena_conv(uc.reshape(bsz, nc, mw), _filter_call(nc, *hy_args))
            xc2d = _merge_call(xc2d, mod_ctx, nw1, yac.reshape(bsz * nc, mw), x0c, uc,
                               yconv_c.reshape(bsz * nc, mw), ycm_c, ydc.reshape(bsz * nc, mw),
                               hyb, wg, wb, wo, bsz * nc, tmc)
            xc2d = _ffn_call(xc2d, mod_ctx, nw2, wfa, wfg, wfd, fnw, bsz * nc, tmc, False)
            xc = xc2d.reshape(bsz, nc, d)
    return x
```

```python
import functools
import math

import numpy as np
import jax
import jax.numpy as jnp
from jax import lax
from jax.experimental import pallas as pl
from jax.experimental.pallas import tpu as pltpu

F32 = jnp.float32
BF16 = jnp.bfloat16

GRID_W = 64
NORM_EPS = 1e-6
N_BRANCH = 4
NEG_BIG = -1e30
F_EPS = 1e-30
HEAD_DIM = 64
HEADS = 4
LANE = 128
A_CHUNK = 128
D_CHUNK = 128
C_CHUNK = 128
C_GROUPS = 4
B_SHORT = 3
B_BANDS = 8
B_DECAY_TARGET = 1e-2
B_FAST_DECAY = 0.3
B_SLOW_DECAY = 1.5
CONV_BLOCK = 256
VMEM_LIMIT = 56 * 1024 * 1024


def _cparams(n_axes):
    return pltpu.CompilerParams(dimension_semantics=("arbitrary",) * n_axes,
                                vmem_limit_bytes=VMEM_LIMIT)


def _dot(a, b):
    return jnp.dot(a.astype(BF16), b.astype(BF16), preferred_element_type=F32)


def _split3(x):
    hi = x.astype(BF16)
    r1 = x - hi.astype(F32)
    mid = r1.astype(BF16)
    lo = (r1 - mid.astype(F32)).astype(BF16)
    return hi, mid, lo


def _dot_f32(a, b):
    a_hi, a_mid, a_lo = _split3(a)
    b_hi, b_mid, b_lo = _split3(b)
    d = lambda u, v: jnp.dot(u, v, preferred_element_type=F32)
    return (d(a_hi, b_hi) + (d(a_hi, b_mid) + d(a_mid, b_hi))
            + (d(a_hi, b_lo) + d(a_mid, b_mid) + d(a_lo, b_hi)))


def _sigmoid(x):
    return 1.0 / (1.0 + jnp.exp(-x))


def _sigmoid_both(x):
    e = jnp.exp(-jnp.abs(x))
    big = 1.0 / (1.0 + e)
    small = e * big
    pos = x >= 0.0
    return jnp.where(pos, big, small), jnp.where(pos, small, big)


def _silu(x):
    return x * _sigmoid(x)


def _log_sigmoid(x):
    return jnp.minimum(x, 0.0) - jnp.log(1.0 + jnp.exp(-jnp.abs(x)))


def _norm_mod(x, nw, shift, scale):
    ms = jnp.mean(x * x, axis=-1, keepdims=True)
    y = x * lax.rsqrt(ms + NORM_EPS) * nw
    return y * (1.0 + scale) + shift


def _ada_kernel(c_ref, w_ref, b_ref, o_ref):
    o_ref[0] = _dot_f32(_silu(c_ref[...]), w_ref[0]) + b_ref[0]


def _ada_call(cvec, ada_w, ada_b):
    depth, d, n6 = ada_w.shape
    mb = cvec.shape[0]
    tn = 512
    return pl.pallas_call(
        _ada_kernel,
        grid=(depth, n6 // tn),
        in_specs=[pl.BlockSpec((mb, d), lambda l, j: (0, 0)),
                  pl.BlockSpec((1, d, tn), lambda l, j: (l, 0, j)),
                  pl.BlockSpec((1, 1, tn), lambda l, j: (l, 0, j))],
        out_specs=pl.BlockSpec((1, mb, tn), lambda l, j: (l, 0, j)),
        out_shape=jax.ShapeDtypeStruct((depth, mb, n6), F32),
        compiler_params=_cparams(2),
        name="ada_mod",
    )(cvec, ada_w, ada_b.reshape(depth, 1, n6))


def _in_kernel(*refs, d, n_seg):
    x_ref, mod_ref, nw_ref = refs[:3]
    w_refs = refs[3:3 + n_seg]
    b_refs = refs[3 + n_seg:3 + 2 * n_seg]
    o_refs = refs[3 + 2 * n_seg:]
    h = _norm_mod(x_ref[...], nw_ref[...], mod_ref[0, :, 0:d], mod_ref[0, :, d:2 * d]).astype(BF16)
    for w_ref, b_ref, o_ref in zip(w_refs, b_refs, o_refs):
        o_ref[...] = jnp.dot(h, w_ref[...], preferred_element_type=F32) + b_ref[...]


def _in_call(x2d, mod, nw, ws, bs, rows_per_mod, tm):
    r, d = x2d.shape
    n_seg = len(ws)
    blocks_per_mod = rows_per_mod // tm
    const = lambda i: (0, 0)
    in_specs = [pl.BlockSpec((tm, d), lambda i: (i, 0)),
                pl.BlockSpec((1, 1, mod.shape[-1]), lambda i: (i // blocks_per_mod, 0, 0)),
                pl.BlockSpec((1, d), const)]
    in_specs += [pl.BlockSpec(w.shape, const) for w in ws]
    in_specs += [pl.BlockSpec(b.shape, const) for b in bs]
    return pl.pallas_call(
        functools.partial(_in_kernel, d=d, n_seg=n_seg),
        grid=(r // tm,),
        in_specs=in_specs,
        out_specs=[pl.BlockSpec((tm, w.shape[1]), lambda i: (i, 0)) for w in ws],
        out_shape=[jax.ShapeDtypeStruct((r, w.shape[1]), F32) for w in ws],
        compiler_params=_cparams(1),
        name="in_proj",
    )(x2d, mod, nw, *ws, *bs)


def _load_tile(x_ref):
    if len(x_ref.shape) == 2:
        return x_ref[...]
    return jnp.concatenate([x_ref[:, c, :] for c in range(x_ref.shape[1])], axis=0)


def _tile_specs(tm, d, col_view):
    if col_view is None:
        hb = tm // 8
        return (pl.BlockSpec((tm, d), lambda i: (i, 0)),
                lambda n_rows: pl.BlockSpec((8, d), lambda i: (jnp.maximum(i * hb - 1, 0), 0)),
                lambda n_rows: pl.BlockSpec((8, d), lambda i: (jnp.minimum((i + 1) * hb, n_rows // 8 - 1), 0)),
                lambda x2d: x2d)
    rows, cols = col_view
    ncol = tm // rows
    tps = cols // ncol
    hb = ncol // 8
    return (pl.BlockSpec((rows, ncol, d), lambda i: (i // tps, i % tps, 0)),
            lambda n_rows: pl.BlockSpec((rows, 8, d), lambda i: (i // tps, jnp.maximum((i % tps) * hb - 1, 0), 0)),
            lambda n_rows: pl.BlockSpec((rows, 8, d),
                                        lambda i: (i // tps, jnp.minimum((i % tps + 1) * hb, cols // 8 - 1), 0)),
            lambda x2d: x2d.reshape(-1, cols, d))


def _gmlp_chunk(x, ws_ref, bs_ref, nw_ref, nb_ref, lane_lo):
    g = 0.5 * x * (1.0 + jnp.tanh(math.sqrt(2.0 / math.pi) * (x + 0.044715 * (x * x * x))))
    u, v = g[:, 0:256], g[:, 256:512]
    mu = jnp.mean(v, axis=-1, keepdims=True)
    vc = v - mu
    var = jnp.mean(vc * vc, axis=-1, keepdims=True)
    vn = (vc * lax.rsqrt(var + NORM_EPS) * nw_ref[...] + nb_ref[...]).astype(BF16)
    mixed = []
    for pair in range(C_GROUPS // 2):
        v_p = vn[:, pair * LANE:(pair + 1) * LANE]
        mixed.append(jnp.where(lane_lo, jnp.dot(ws_ref[2 * pair], v_p, preferred_element_type=F32),
                               jnp.dot(ws_ref[2 * pair + 1], v_p, preferred_element_type=F32)))
    return u * (jnp.concatenate(mixed, axis=1) + bs_ref[...])


def _in_full_kernel(x_ref, xp_ref, xn_ref, mod_ref, nw_ref, wa_ref, wb_ref, wc_ref, wd_ref,
                    ba_ref, bb_ref, bc_ref, bd_ref, sw_ref, sb_ref, gws_ref, gbs_ref, gnw_ref, gnb_ref,
                    oa_ref, ox0_ref, ou_ref, oc_ref, od_ref, *, d, seq_len):
    shift, scale, nw = mod_ref[0, :, 0:d], mod_ref[0, :, d:2 * d], nw_ref[...]
    x = _load_tile(x_ref)
    tm = x.shape[0]
    h = _norm_mod(x, nw, shift, scale).astype(BF16)
    pc = jnp.dot(h, wc_ref[...], preferred_element_type=F32) + bc_ref[...]
    pb = jnp.dot(h, wb_ref[...], preferred_element_type=F32) + bb_ref[...]
    oa_ref[...] = jnp.dot(h, wa_ref[...], preferred_element_type=F32) + ba_ref[...]
    od_ref[...] = jnp.dot(h, wd_ref[...], preferred_element_type=F32) + bd_ref[...]
    lane_lo = lax.broadcasted_iota(jnp.int32, (C_CHUNK, LANE), 1) < 256 // C_GROUPS
    for c in range(tm // C_CHUNK):
        oc_ref[c * C_CHUNK:(c + 1) * C_CHUNK, :] = _gmlp_chunk(pc[c * C_CHUNK:(c + 1) * C_CHUNK], gws_ref, gbs_ref,
                                                                gnw_ref, gnb_ref, lane_lo)
    if len(xp_ref.shape) == 2:
        halo = jnp.concatenate([xp_ref[...], xn_ref[...]], axis=0)
    else:
        halo = jnp.concatenate([xp_ref[xp_ref.shape[0] - 1], xn_ref[0]], axis=0)
    pb_halo = jnp.dot(_norm_mod(halo, nw, shift, scale).astype(BF16), wb_ref[...],
                      preferred_element_type=F32) + bb_ref[...]
    row = lax.broadcasted_iota(jnp.int32, pb.shape, 0)
    pos = (pl.program_id(0) * tm + row) & (seq_len - 1)
    prev = jnp.where(row == 0, pb_halo[7:8, :], pltpu.roll(pb, 1, axis=0))
    nxt = jnp.where(row == tm - 1, pb_halo[8:9, :], pltpu.roll(pb, tm - 1, axis=0))
    prev = jnp.where(pos == 0, 0.0, prev)
    nxt = jnp.where(pos == seq_len - 1, 0.0, nxt)
    y = prev * sw_ref[0:1, :] + pb * sw_ref[1:2, :] + nxt * sw_ref[2:3, :] + sb_ref[...]
    ox0_ref[...] = y[:, 0:256]
    ou_ref[...] = y[:, 256:512] * y[:, 512:768]


def _in_full_call(x2d, mod, nw, ws, bs, conv_w, conv_b, gws, gbs, gnw, gnb, rows_per_mod, seq_len, tm,
                  col_view=None):
    r, d = x2d.shape
    assert seq_len & (seq_len - 1) == 0 and tm % C_CHUNK == 0
    blocks_per_mod = rows_per_mod // tm
    c2 = lambda i: (0, 0)
    full = lambda a: pl.BlockSpec(a.shape, (lambda i: (0,) * a.ndim))
    row = lambda w: pl.BlockSpec((tm, w), lambda i: (i, 0))
    x_spec, prev_spec, next_spec, view = _tile_specs(tm, d, col_view)
    in_specs = [x_spec, prev_spec(r), next_spec(r),
                pl.BlockSpec((1, 1, mod.shape[-1]), lambda i: (i // blocks_per_mod, 0, 0)),
                pl.BlockSpec((1, d), c2)]
    consts = list(ws) + list(bs) + [conv_w, conv_b, gws, gbs, gnw, gnb]
    in_specs += [full(a) for a in consts]
    widths = (ws[0].shape[1], 256, 256, 256, ws[3].shape[1])
    return pl.pallas_call(
        functools.partial(_in_full_kernel, d=d, seq_len=seq_len),
        grid=(r // tm,),
        in_specs=in_specs,
        out_specs=[row(w) for w in widths],
        out_shape=[jax.ShapeDtypeStruct((r, w), F32) for w in widths],
        compiler_params=_cparams(1),
        name="in_proj_full",
    )(view(x2d), view(x2d), view(x2d), mod, nw, *consts)


def _mlstm_ones(L):
    ones_blk = np.zeros((2 * L, LANE), np.float32)
    ones_blk[:L, :HEAD_DIM] = 1.0
    ones_blk[L:, HEAD_DIM:] = 1.0
    return ones_blk


def _running(op, fill, x, row, reverse):
    L = x.shape[0]
    sh = 1
    while sh < L:
        if reverse:
            shifted, valid = pltpu.roll(x, L - sh, axis=0), row < L - sh
        else:
            shifted, valid = pltpu.roll(x, sh, axis=0), row >= sh
        x = op(x, jnp.where(valid, shifted, fill))
        sh *= 2
    return x


def _mlstm_chunk(blk, ones_ref, cn_ref, m_ref, d, mask, row, lane_lo, sub_lo, bd):
    L = blk.shape[0]
    reverse = d == 1
    end = 0 if reverse else L - 1
    cum = _running(jnp.add, 0.0, _log_sigmoid(blk[:, 1152:1280]), row, reverse)
    a = blk[:, 1024:1152] - cum
    m_prev = m_ref[d:d + 1, :]
    g = jnp.maximum(_running(jnp.maximum, NEG_BIG, a, row, reverse), m_prev)
    g_end = g[end:end + 1, :]
    w_inter = jnp.exp(m_prev - g)
    e_negm = jnp.exp(-(cum + g))
    m_ref[d:d + 1, :] = cum[end:end + 1, :] + g_end
    a_t = a.T

    def head_lanes(x, pair):
        c = 4 * d + 2 * pair
        return jnp.where(lane_lo, jnp.broadcast_to(x[:, c:c + 1], (L, LANE)),
                         jnp.broadcast_to(x[:, c + 1:c + 2], (L, LANE)))

    outs = []
    for pair in range(HEADS // 2):
        cs = slice(128 * pair, 128 * pair + 128)
        q_pair = blk[:, cs].astype(BF16)
        k_pair = blk[:, 256 + 128 * pair:256 + 128 * pair + 128] * (HEAD_DIM ** -0.5)
        v_pair = blk[:, 512 + 128 * pair:512 + 128 * pair + 128]
        k_b = k_pair.astype(BF16)
        zero = jnp.zeros_like(k_b)
        k_rows = jnp.concatenate([jnp.where(lane_lo, k_b, zero), jnp.where(lane_lo, zero, k_b)], axis=0)
        s_pair = lax.dot_general(q_pair, k_rows, (((1,), (1,)), ((), ())), preferred_element_type=F32)
        w_halves = []
        for sub in range(2):
            hd = 2 * pair + sub
            c = 4 * d + hd
            expo = jnp.where(mask, a_t[c:c + 1, :] - jnp.broadcast_to(g[:, c:c + 1], (L, L)), NEG_BIG)
            w_halves.append((jnp.exp(expo) * s_pair[:, sub * L:(sub + 1) * L]).astype(BF16))
        w_pair = jnp.concatenate(w_halves, axis=1)
        v_b = v_pair.astype(BF16)
        v_rows = jnp.concatenate([jnp.where(lane_lo, v_b, zero), jnp.where(lane_lo, zero, v_b)], axis=0)
        intra = jnp.dot(w_pair, jnp.concatenate([v_rows, ones_ref[...]], axis=1), preferred_element_type=F32)
        slot = 2 * d + pair
        cn = cn_ref[slot]
        inter = jnp.dot(q_pair, cn.astype(BF16), preferred_element_type=F32)
        wi_p = head_lanes(w_inter, pair)
        num = wi_p * inter[:, 0:128] + intra[:, 0:128]
        den = wi_p * inter[:, 128:256] + intra[:, 128:256]
        outs.append(num / jnp.maximum(jnp.abs(den), head_lanes(e_negm, pair)))
        c_e = 4 * d + 2 * pair
        w_st = jnp.exp(jnp.where(sub_lo, a_t[c_e:c_e + 1, :] - g_end[:, c_e:c_e + 1],
                                 a_t[c_e + 1:c_e + 2, :] - g_end[:, c_e + 1:c_e + 2]))
        ktw = (k_pair.T * w_st).astype(BF16)
        upd = jnp.dot(ktw, jnp.concatenate([v_b, jnp.ones_like(v_b)], axis=1), preferred_element_type=F32)
        decay = wi_p[end:end + 1, :]
        cn_ref[slot] = jnp.concatenate([decay, decay], axis=1) * cn + jnp.where(bd, upd, 0.0)
    return jnp.concatenate(outs, axis=1)


def _mlstm_kernel(*refs, need_ctx):
    ctx_ref, lat_ref, ones_ref, nw_ref = refs[:4]
    if need_ctx:
        yc_ref, yl_ref, ybc_ref, ybl_ref, cn_ref, m_ref = refs[4:]
    else:
        yl_ref, ybc_ref, ybl_ref, cn_ref, m_ref = refs[4:]
        yc_ref = None
    L = A_CHUNK
    lane_lo = lax.broadcasted_iota(jnp.int32, (L, LANE), 1) < HEAD_DIM
    row = lax.broadcasted_iota(jnp.int32, (L, LANE), 0)
    sub_lo = row < HEAD_DIM
    r2 = lax.broadcasted_iota(jnp.int32, (L, L), 0)
    c2 = lax.broadcasted_iota(jnp.int32, (L, L), 1)
    mask_f, mask_b = r2 >= c2, r2 <= c2
    rb = lax.broadcasted_iota(jnp.int32, (LANE, 2 * LANE), 0) < HEAD_DIM
    cb = (lax.broadcasted_iota(jnp.int32, (LANE, 2 * LANE), 1) % LANE) < HEAD_DIM
    bd = rb == cb
    cn_ref[...] = jnp.zeros_like(cn_ref)
    m_ref[...] = jnp.zeros_like(m_ref)

    def scan(src_ref, yf_ref, yb_ref):
        n_chunks = src_ref.shape[1] // L

        def body(j, carry):
            sf = pl.multiple_of(j * L, L)
            sb = pl.multiple_of((n_chunks - 1 - j) * L, L)
            hf = _mlstm_chunk(src_ref[0, pl.ds(sf, L), :], ones_ref, cn_ref, m_ref, 0, mask_f, row, lane_lo, sub_lo, bd)
            hb = _mlstm_chunk(src_ref[0, pl.ds(sb, L), :], ones_ref, cn_ref, m_ref, 1, mask_b, row, lane_lo, sub_lo, bd)
            if yf_ref is not None:
                yf_ref[0, pl.ds(sf, L), :] = hf
                yb_ref[pl.ds(sb, L), :] = hb
            return carry

        lax.fori_loop(0, n_chunks, body, 0)

    def finalize(src_ref, y_ref, yb_ref):
        n_chunks = src_ref.shape[1] // L

        def body(j, carry):
            s = pl.multiple_of(j * L, L)
            for pair in range(HEADS // 2):
                cs = slice(128 * pair, 128 * pair + 128)
                y = y_ref[0, pl.ds(s, L), cs] + yb_ref[pl.ds(s, L), cs]
                y2 = y * y
                s_lo = jnp.sum(jnp.where(lane_lo, y2, 0.0), axis=-1, keepdims=True)
                s_all = jnp.sum(y2, axis=-1, keepdims=True)
                ms = jnp.where(lane_lo, s_lo, s_all - s_lo) * (1.0 / HEAD_DIM)
                o_gate = src_ref[0, pl.ds(s, L), 768 + 128 * pair:768 + 128 * pair + 128]
                y_ref[0, pl.ds(s, L), cs] = y * lax.rsqrt(ms + NORM_EPS) * nw_ref[:, cs] * _sigmoid(o_gate)
            return carry

        lax.fori_loop(0, n_chunks, body, 0)

    scan(ctx_ref, yc_ref, ybc_ref)
    scan(lat_ref, yl_ref, ybl_ref)
    if need_ctx:
        finalize(ctx_ref, yc_ref, ybc_ref)
    finalize(lat_ref, yl_ref, ybl_ref)


def _mlstm_call(pa_ctx, pa_lat, nw, need_ctx):
    b, nc, wa = pa_ctx.shape
    n = pa_lat.shape[1]
    L = A_CHUNK
    ones_blk = jnp.asarray(_mlstm_ones(L), BF16)
    c2 = lambda i: (0, 0)
    c3 = lambda i: (0, 0, 0)
    out_specs = [pl.BlockSpec((1, n, 256), lambda i: (i, 0, 0))]
    out_shape = [jax.ShapeDtypeStruct((b, n, 256), F32)]
    if need_ctx:
        out_specs.insert(0, pl.BlockSpec((1, nc, 256), lambda i: (i, 0, 0)))
        out_shape.insert(0, jax.ShapeDtypeStruct((b, nc, 256), F32))
    res = pl.pallas_call(
        functools.partial(_mlstm_kernel, need_ctx=need_ctx),
        grid=(b,),
        in_specs=[pl.BlockSpec((1, nc, wa), lambda i: (i, 0, 0)),
                  pl.BlockSpec((1, n, wa), lambda i: (i, 0, 0)),
                  pl.BlockSpec(ones_blk.shape, c2),
                  pl.BlockSpec((1, 256), c2)],
        out_specs=out_specs,
        out_shape=out_shape,
        scratch_shapes=[pltpu.VMEM((nc, 256), F32), pltpu.VMEM((n, 256), F32),
                        pltpu.VMEM((HEADS, LANE, 2 * LANE), F32), pltpu.VMEM((8, LANE), F32)],
        compiler_params=_cparams(1),
        name="mlstm",
    )(pa_ctx, pa_lat, ones_blk, nw)
    return (res[0], res[1]) if need_ctx else (None, res[0])


D_FAST_BASE = 16
D_FAST_MAX_EXPONENT = 60.0


def _hgrn_tables(L, base):
    t = np.arange(L)
    lev0 = int(math.log2(base))
    n_split = int(math.log2(L)) - lev0
    same_base = (t[:, None] // base) == (t[None, :] // base)
    lvl_f = np.where(same_base & (t[:, None] >= t[None, :]), 0, -1)
    lvl_b = np.where(same_base & (t[:, None] <= t[None, :]), 0, -1)
    mids_f, mids_b = [], []
    for i in range(n_split):
        half, size = base << i, base << (i + 1)
        same = (t[:, None] // size) == (t[None, :] // size)
        upper = (t // half) % 2 == 1
        lvl_f = np.where(same & upper[:, None] & ~upper[None, :], i + 1, lvl_f)
        lvl_b = np.where(same & ~upper[:, None] & upper[None, :], i + 1, lvl_b)
        starts = np.arange(0, L, size)
        mids_f.append([(int(s), size, int(s) + half - 1) for s in starts])
        mids_b.append([(int(s), size, int(s) + half) for s in starts])
    wide = lambda m: np.concatenate([m, m], axis=-1).astype(np.int32)
    return wide(lvl_f), wide(lvl_b), mids_f, mids_b


def _rows_of(b, spec):
    return jnp.concatenate([jnp.broadcast_to(b[r:r + 1, :], (n, b.shape[1])) for _, n, r in spec], axis=0)


def _hgrn_chunk(blk, f_pre, lb, lvl_ref, mids, s_ref, x_ref, d, row, lane_lo, bd, base):
    L = blk.shape[0]
    reverse = d == 1
    end = 0 if reverse else L - 1
    qs = _silu(blk[:, 0:256])
    v = blk[:, 256:512]
    sig, sig_neg = _sigmoid_both(f_pre)
    log_f = jnp.log(jnp.maximum(lb + (1.0 - lb) * sig, F_EPS))
    k = (1.0 - lb) * sig_neg
    b = _running(jnp.add, 0.0, log_f, row, reverse)
    b_end = b[end:end + 1, :]
    q_in = qs * jnp.exp(b)
    k_out = k * jnp.exp(b_end - b)
    decay = jnp.exp(b_end)
    factors = []
    if base > 1:
        zero_row = jnp.zeros((1, b.shape[1]), F32)
        if reverse:
            refs = [b[s + base:s + base + 1, :] if s + base < L else zero_row for s in range(0, L, base)]
        else:
            refs = [b[s - 1:s, :] if s > 0 else zero_row for s in range(0, L, base)]
        r0 = jnp.concatenate([jnp.broadcast_to(r, (base, b.shape[1])) for r in refs], axis=0)
        factors.append((jnp.exp(b - r0), jnp.exp(r0 - b)))
        last = [b[s:s + 1, :] if reverse else b[s + base - 1:s + base, :] for s in range(0, L, base)]
        worst = functools.reduce(jnp.maximum, [r - e for r, e in zip(refs, last)])
        x_ref[d:d + 1, :] = jnp.maximum(x_ref[d:d + 1, :], worst)
    else:
        factors.append((None, None))
    for spec in mids:
        e = jnp.exp(-jnp.abs(b - _rows_of(b, spec)))
        factors.append((e, e))
    nt = (((1,), (1,)), ((), ()))
    lvl = lvl_ref[...]
    outs = []
    for pair in range(HEADS // 2):
        cs = slice(128 * pair, 128 * pair + 128)
        q_p, k_p = qs[:, cs], k[:, cs]
        zero = jnp.zeros((L, LANE), BF16)
        v_b = v[:, cs].astype(BF16)
        v_rows = jnp.concatenate([jnp.where(lane_lo, v_b, zero), jnp.where(lane_lo, zero, v_b)], axis=0)
        scores = jnp.zeros((L, 2 * L), F32)
        for i, (eq, ek) in enumerate(factors):
            q_l = (q_p if eq is None else q_p * eq[:, cs]).astype(BF16)
            k_l = (k_p if ek is None else k_p * ek[:, cs]).astype(BF16)
            k_rows = jnp.concatenate([jnp.where(lane_lo, k_l, zero), jnp.where(lane_lo, zero, k_l)], axis=0)
            scores = jnp.where(lvl == i, lax.dot_general(q_l, k_rows, nt, preferred_element_type=F32), scores)
        slot = 2 * d + pair
        s_t = s_ref[slot]
        outs.append(jnp.dot(scores.astype(BF16), v_rows, preferred_element_type=F32)
                    + lax.dot_general(q_in[:, cs].astype(BF16), s_t.astype(BF16), nt, preferred_element_type=F32))
        upd = lax.dot_general(v_b, k_out[:, cs].astype(BF16), (((0,), (0,)), ((), ())), preferred_element_type=F32)
        s_ref[slot] = decay[:, cs] * s_t + jnp.where(bd, upd, 0.0)
    return jnp.concatenate(outs, axis=1)


def _hgrn_kernel(*refs, need_ctx, layer, mids_fast, mids_safe):
    ctx_ref, lat_ref, lvlf_fast_ref, lvlb_fast_ref, lvlf_safe_ref, lvlb_safe_ref, lbl_ref, nw_ref = refs[:8]
    if need_ctx:
        yc_ref, yl_ref, ybc_ref, ybl_ref, s_ref, x_ref = refs[8:]
    else:
        yl_ref, ybc_ref, ybl_ref, s_ref, x_ref = refs[8:]
        yc_ref = None
    L = D_CHUNK
    lane_lo = lax.broadcasted_iota(jnp.int32, (L, LANE), 1) < HEAD_DIM
    row = lax.broadcasted_iota(jnp.int32, (L, 2 * LANE), 0)
    bd = ((lax.broadcasted_iota(jnp.int32, (LANE, LANE), 0) < HEAD_DIM)
          == (lax.broadcasted_iota(jnp.int32, (LANE, LANE), 1) < HEAD_DIM))
    logits = lbl_ref[...]
    e = jnp.exp(logits - jnp.max(logits, axis=0, keepdims=True))
    prob = e / jnp.sum(e, axis=0, keepdims=True)
    lb = jnp.sum(prob[0:layer + 1], axis=0, keepdims=True) - prob[0:1]
    def scan(src_ref, yf_ref, yb_ref, lvlf_ref, lvlb_ref, mids, base):
        n_chunks = src_ref.shape[1] // L

        def body(j, carry):
            sf = pl.multiple_of(j * L, L)
            sb = pl.multiple_of((n_chunks - 1 - j) * L, L)
            blk_f = src_ref[0, pl.ds(sf, L), :]
            blk_b = src_ref[0, pl.ds(sb, L), :]
            of = _hgrn_chunk(blk_f, blk_f[:, 512:768], lb, lvlf_ref, mids[0], s_ref, x_ref, 0, row, lane_lo, bd, base)
            ob = _hgrn_chunk(blk_b, blk_b[:, 768:1024], lb, lvlb_ref, mids[1], s_ref, x_ref, 1, row, lane_lo, bd, base)
            if yf_ref is not None:
                yf_ref[0, pl.ds(sf, L), :] = of
                yb_ref[pl.ds(sb, L), :] = ob
            return carry

        lax.fori_loop(0, n_chunks, body, 0)

    def scans(lvlf_ref, lvlb_ref, mids, base):
        s_ref[...] = jnp.zeros_like(s_ref)
        scan(ctx_ref, yc_ref, ybc_ref, lvlf_ref, lvlb_ref, mids, base)
        scan(lat_ref, yl_ref, ybl_ref, lvlf_ref, lvlb_ref, mids, base)

    x_ref[...] = jnp.zeros_like(x_ref)
    scans(lvlf_fast_ref, lvlb_fast_ref, mids_fast, D_FAST_BASE)

    @pl.when(jnp.logical_not(jnp.max(x_ref[...]) <= D_FAST_MAX_EXPONENT))
    def _():
        scans(lvlf_safe_ref, lvlb_safe_ref, mids_safe, 1)

    def finalize(src_ref, y_ref, yb_ref):
        n_chunks = src_ref.shape[1] // L

        def body(j, carry):
            s = pl.multiple_of(j * L, L)
            for pair in range(HEADS // 2):
                cs = slice(128 * pair, 128 * pair + 128)
                y = y_ref[0, pl.ds(s, L), cs] + yb_ref[pl.ds(s, L), cs]
                y2 = y * y
                s_lo = jnp.sum(jnp.where(lane_lo, y2, 0.0), axis=-1, keepdims=True)
                s_all = jnp.sum(y2, axis=-1, keepdims=True)
                ms = jnp.where(lane_lo, s_lo, s_all - s_lo) * (1.0 / HEAD_DIM)
                g = src_ref[0, pl.ds(s, L), 1024 + 128 * pair:1024 + 128 * pair + 128]
                y_ref[0, pl.ds(s, L), cs] = y * lax.rsqrt(ms + NORM_EPS) * nw_ref[:, cs] * _silu(g)
            return carry

        lax.fori_loop(0, n_chunks, body, 0)

    if need_ctx:
        finalize(ctx_ref, yc_ref, ybc_ref)
    finalize(lat_ref, yl_ref, ybl_ref)


def _hgrn_call(pd_ctx, pd_lat, lb_logits, nw, layer, need_ctx):
    b, nc, wd = pd_ctx.shape
    n = pd_lat.shape[1]
    L = D_CHUNK
    lvlf_fast, lvlb_fast, midsf_fast, midsb_fast = _hgrn_tables(L, D_FAST_BASE)
    lvlf_safe, lvlb_safe, midsf_safe, midsb_safe = _hgrn_tables(L, 1)
    tables = [jnp.asarray(a) for a in (lvlf_fast, lvlb_fast, lvlf_safe, lvlb_safe)]
    c2 = lambda i: (0, 0)
    out_specs = [pl.BlockSpec((1, n, 256), lambda i: (i, 0, 0))]
    out_shape = [jax.ShapeDtypeStruct((b, n, 256), F32)]
    if need_ctx:
        out_specs.insert(0, pl.BlockSpec((1, nc, 256), lambda i: (i, 0, 0)))
        out_shape.insert(0, jax.ShapeDtypeStruct((b, nc, 256), F32))
    res = pl.pallas_call(
        functools.partial(_hgrn_kernel, need_ctx=need_ctx, layer=layer,
                          mids_fast=(midsf_fast, midsb_fast), mids_safe=(midsf_safe, midsb_safe)),
        grid=(b,),
        in_specs=[pl.BlockSpec((1, nc, wd), lambda i: (i, 0, 0)),
                  pl.BlockSpec((1, n, wd), lambda i: (i, 0, 0))]
                 + [pl.BlockSpec(a.shape, c2) for a in tables]
                 + [pl.BlockSpec(lb_logits.shape, c2), pl.BlockSpec((1, 256), c2)],
        out_specs=out_specs,
        out_shape=out_shape,
        scratch_shapes=[pltpu.VMEM((nc, 256), F32), pltpu.VMEM((n, 256), F32),
                        pltpu.VMEM((HEADS, LANE, LANE), F32), pltpu.VMEM((8, 256), F32)],
        compiler_params=_cparams(1),
        name="hgrn2",
    )(pd_ctx, pd_lat, *tables, lb_logits, nw)
    return (res[0], res[1]) if need_ctx else (None, res[0])


def _filter_feats(n):
    t = np.linspace(0.0, 1.0, n, dtype=np.float32)[:, None]
    bands = np.linspace(1e-4, B_BANDS - 1, B_BANDS, dtype=np.float32)[None]
    ang = (np.float32(2 * math.pi) * bands * np.arange(n, dtype=np.float32)[:, None] / np.float32(n)).astype(np.float32)
    z = np.concatenate([t, np.cos(ang), -np.sin(ang)], axis=-1).astype(np.float32)
    deltas = np.abs(np.linspace(math.log(B_DECAY_TARGET) / B_SLOW_DECAY,
                                math.log(B_DECAY_TARGET) / B_FAST_DECAY, 256, dtype=np.float32))
    neg_t_deltas = (-t * deltas[None]).astype(np.float32)
    return z, neg_t_deltas


def _filter_kernel(z_ref, ntd_ref, w1_ref, b1_ref, f1_ref, w2_ref, b2_ref, f2_ref, w3_ref, flip_ref, o_ref):
    n = z_ref.shape[0]
    T = flip_ref.shape[0]
    hd = jnp.sin(f1_ref[...] * (_dot_f32(z_ref[...], w1_ref[...]) + b1_ref[...]))
    hd = jnp.sin(f2_ref[...] * (_dot_f32(hd, w2_ref[...]) + b2_ref[...]))
    decay = jnp.exp(ntd_ref[...])
    fwd = _dot_f32(hd, w3_ref[:, 0:256]) * decay
    bwd = _dot_f32(hd, w3_ref[:, 256:512]) * decay
    flip = flip_ref[...]
    rev = []
    for i in range(n // T):
        parts = _split3(bwd[n - (i + 1) * T:n - i * T])
        rev.append(sum(jnp.dot(flip, p, preferred_element_type=F32) for p in parts))
    rev = jnp.concatenate(rev, axis=0)
    row = lax.broadcasted_iota(jnp.int32, rev.shape, 0)
    o_ref[0:n, :] = jnp.where(row == 0, 0.0, pltpu.roll(rev, 1, axis=0))
    o_ref[n:2 * n, :] = fwd


def _filter_call(n, w1, b1, f1, w2, b2, f2, w3):
    z, ntd = _filter_feats(n)
    k_pad = 32
    z = np.pad(z, ((0, 0), (0, k_pad - z.shape[1])))
    w1 = jnp.pad(w1, ((0, k_pad - w1.shape[0]), (0, 0)))
    flip = jnp.asarray(np.eye(CONV_BLOCK, dtype=np.float32)[::-1], BF16)
    args = (jnp.asarray(z), jnp.asarray(ntd), w1, b1, f1, w2, b2, f2, w3, flip)
    return pl.pallas_call(
        _filter_kernel,
        grid=(1,),
        in_specs=[pl.BlockSpec(a.shape, lambda i: (0, 0)) for a in args],
        out_specs=pl.BlockSpec((2 * n, 256), lambda i: (0, 0)),
        out_shape=jax.ShapeDtypeStruct((2 * n, 256), F32),
        compiler_params=_cparams(1),
        name="hyena_filter",
    )(*args)


def _long_conv_kernel(u_ref, k_ref, o_ref, *, n_blk, bsz, ch_per_step):
    T = CONV_BLOCK
    half = T // 2
    for c in range(ch_per_step):
        taps = k_ref[c]
        rolled = pltpu.roll(jnp.broadcast_to(taps, (half, taps.shape[1])), 0, axis=1, stride=1, stride_axis=0)
        acc = [None] * n_blk
        for d in range(-(n_blk - 1), n_blk):
            base = (d + n_blk) * T
            tile = jnp.concatenate([rolled[:, base:base + T], rolled[:, base - half:base - half + T]],
                                   axis=0).astype(BF16)
            j_lo, j_hi = max(0, -d), min(n_blk, n_blk - d)
            lhs = u_ref[c, j_lo * bsz:j_hi * bsz, :].astype(BF16)
            res = jnp.dot(lhs, tile, preferred_element_type=F32)
            for j in range(j_lo, j_hi):
                part = res[(j - j_lo) * bsz:(j - j_lo + 1) * bsz]
                i = j + d
                acc[i] = part if acc[i] is None else acc[i] + part
        o_ref[c] = jnp.concatenate(acc, axis=0)


def _long_conv_call(u_t, k_t, n_blk, bsz, ch_per_step):
    ch, rows, T = u_t.shape
    return pl.pallas_call(
        functools.partial(_long_conv_kernel, n_blk=n_blk, bsz=bsz, ch_per_step=ch_per_step),
        grid=(ch // ch_per_step,),
        in_specs=[pl.BlockSpec((ch_per_step, rows, T), lambda i: (i, 0, 0)),
                  pl.BlockSpec((ch_per_step, 1, k_t.shape[-1]), lambda i: (i, 0, 0))],
        out_specs=pl.BlockSpec((ch_per_step, rows, T), lambda i: (i, 0, 0)),
        out_shape=jax.ShapeDtypeStruct((ch, rows, T), F32),
        compiler_params=_cparams(1),
        name="hyena_long_conv",
    )(u_t, k_t)


def _hyena_conv(u, filt):
    bsz, n, ch = u.shape
    T = CONV_BLOCK
    n_blk = n // T
    k_t = filt.T.reshape(ch, 1, 2 * n)
    u_t = u.reshape(bsz, n_blk, T, ch).transpose(3, 1, 0, 2).reshape(ch, n_blk * bsz, T)
    y_t = _long_conv_call(u_t, k_t, n_blk, bsz, 4)
    return y_t.reshape(ch, n_blk, bsz, T).transpose(2, 1, 3, 0).reshape(bsz, n, ch)


def _merge_kernel(x_ref, mod_ref, nw_ref, ya_ref, x0_ref, u_ref, yconv_ref, yc_ref, yd_ref, hyb_ref,
                  wg_ref, wb_ref, wo_ref, o_ref, *, d):
    x = x_ref[...]
    h = _norm_mod(x, nw_ref[...], mod_ref[0, :, 0:d], mod_ref[0, :, d:2 * d]).astype(BF16)
    y_b = _load_tile(x0_ref) * (_load_tile(yconv_ref) + _load_tile(u_ref) * hyb_ref[...])
    ys = (_load_tile(ya_ref), y_b, _load_tile(yc_ref), _load_tile(yd_ref))
    acc = None
    for j in range(N_BRANCH):
        gate = _sigmoid(jnp.dot(h, wg_ref[j], preferred_element_type=F32))
        term = gate * jnp.dot(ys[j].astype(BF16), wb_ref[j], preferred_element_type=F32)
        acc = term if acc is None else acc + term
    y = jnp.dot(acc.astype(BF16), wo_ref[...], preferred_element_type=F32)
    o_ref[...] = x + mod_ref[0, :, 2 * d:3 * d] * y


def _merge_call(x2d, mod, nw, ya, x0, u, yconv, yc, yd, hyb, wg, wb, wo, rows_per_mod, tm, col_view=None):
    r, d = x2d.shape
    blocks_per_mod = rows_per_mod // tm
    c2 = lambda i: (0, 0)
    c3 = lambda i: (0, 0, 0)
    row = lambda w: pl.BlockSpec((tm, w), lambda i: (i, 0))
    if col_view is None:
        y_spec, view = row(256), (lambda y: y)
    else:
        rows, cols = col_view
        nrow = tm // cols
        tps = rows // nrow
        y_spec = pl.BlockSpec((cols, nrow, 256), lambda i: (i // tps, i % tps, 0))
        view = lambda y: y.reshape(-1, rows, 256)
    return pl.pallas_call(
        functools.partial(_merge_kernel, d=d),
        grid=(r // tm,),
        in_specs=[row(d), pl.BlockSpec((1, 1, mod.shape[-1]), lambda i: (i // blocks_per_mod, 0, 0)),
                  pl.BlockSpec((1, d), c2), y_spec, y_spec, y_spec, y_spec, y_spec, y_spec,
                  pl.BlockSpec((1, 256), c2),
                  pl.BlockSpec(wg.shape, c3), pl.BlockSpec(wb.shape, c3), pl.BlockSpec(wo.shape, c2)],
        out_specs=row(d),
        out_shape=jax.ShapeDtypeStruct((r, d), F32),
        compiler_params=_cparams(1),
        name="merge",
    )(x2d, mod, nw, view(ya), view(x0), view(u), view(yconv), view(yc), view(yd), hyb, wg, wb, wo)


def _ffn_kernel(x_ref, mod_ref, nw_ref, wa_ref, wg_ref, wd_ref, fnw_ref, o_ref, *, d, hid_chunk, final_norm):
    x = x_ref[...]
    h = _norm_mod(x, nw_ref[...], mod_ref[0, :, 3 * d:4 * d], mod_ref[0, :, 4 * d:5 * d]).astype(BF16)
    hidden = wa_ref.shape[1]
    acc = None
    for c in range(hidden // hid_chunk):
        cs = slice(c * hid_chunk, (c + 1) * hid_chunk)
        a = jnp.dot(h, wa_ref[:, cs], preferred_element_type=F32)
        g = jnp.dot(h, wg_ref[:, cs], preferred_element_type=F32)
        term = jnp.dot((_silu(a) * g).astype(BF16), wd_ref[cs, :], preferred_element_type=F32)
        acc = term if acc is None else acc + term
    y = x + mod_ref[0, :, 5 * d:6 * d] * acc
    if final_norm:
        y = y * lax.rsqrt(jnp.mean(y * y, axis=-1, keepdims=True) + NORM_EPS) * fnw_ref[...]
    o_ref[...] = y


def _ffn_call(x2d, mod, nw, wa, wg, wd, fnw, rows_per_mod, tm, final_norm):
    r, d = x2d.shape
    blocks_per_mod = rows_per_mod // tm
    c2 = lambda i: (0, 0)
    return pl.pallas_call(
        functools.partial(_ffn_kernel, d=d, hid_chunk=256, final_norm=final_norm),
        grid=(r // tm,),
        in_specs=[pl.BlockSpec((tm, d), lambda i: (i, 0)),
                  pl.BlockSpec((1, 1, mod.shape[-1]), lambda i: (i // blocks_per_mod, 0, 0)),
                  pl.BlockSpec((1, d), c2),
                  pl.BlockSpec(wa.shape, c2), pl.BlockSpec(wg.shape, c2), pl.BlockSpec(wd.shape, c2),
                  pl.BlockSpec((1, d), c2)],
        out_specs=pl.BlockSpec((tm, d), lambda i: (i, 0)),
        out_shape=jax.ShapeDtypeStruct((r, d), F32),
        compiler_params=_cparams(1),
        name="ffn",
    )(x2d, mod, nw, wa, wg, wd, fnw)


def _row_tile(rows):
    return 512 if rows % 512 == 0 else 256


def kernel(x, c, ctx, c_ctx, ada_w, ada_b, norm1_w, norm2_w, w_in, b_in, mlstm_norm_w, hy_short_w, hy_short_b, hy_w1, hy_b1, hy_freq1, hy_w2, hy_b2, hy_freq2, hy_w3, hy_bias, gm_norm_w, gm_norm_b, gm_ws, gm_bs, hg_lb_logits, hg_norm_w, w_gate, w_branch, w_out, w_ffn_in, w_ffn_out, final_norm_w):
    bsz, n, d = x.shape
    nc = ctx.shape[1]
    depth = ada_w.shape[0]
    rows = n // GRID_W
    mw = d // N_BRANCH
    hidden = w_ffn_out.shape[1]
    assert mw == 256 and mw == HEADS * HEAD_DIM

    mb = -(-(bsz + 1) // 8) * 8
    cvec = jnp.concatenate([c, c_ctx[None], jnp.zeros((mb - bsz - 1, d), F32)], axis=0)
    mods = _ada_call(cvec, ada_w, ada_b)

    s_a, s_g, s_b, s_c = 4 * mw, 4 * mw + 4 * HEADS, 7 * mw + 4 * HEADS, 9 * mw + 4 * HEADS
    gate_pad = LANE - 2 * HEADS

    xc = ctx
    for l in range(depth):
        need_ctx = l < depth - 1
        col_major = l % 2 == 1
        col_view = (rows, GRID_W) if col_major else None
        mod_lat = mods[l, :bsz].reshape(bsz, 1, 6 * d)
        mod_ctx = mods[l, bsz:bsz + 1].reshape(1, 1, 6 * d)
        wl, bl = w_in[l], b_in[l]
        wgt, bgt = wl[:, s_a:s_g].reshape(d, 4, HEADS), bl[s_a:s_g].reshape(4, HEADS)
        zw, zb = jnp.zeros((d, gate_pad), F32), jnp.zeros((gate_pad,), F32)
        w_segs = [jnp.concatenate([wl[:, :s_a], wgt[:, 0], wgt[:, 2], zw, wgt[:, 1], wgt[:, 3], zw], axis=1),
                  wl[:, s_g:s_b], wl[:, s_b:s_c], wl[:, s_c:]]
        b_segs = [jnp.concatenate([bl[:s_a], bgt[0], bgt[2], zb, bgt[1], bgt[3], zb]),
                  bl[s_g:s_b], bl[s_b:s_c], bl[s_c:]]
        w_segs = [w.astype(BF16) for w in w_segs]
        b_segs = [b.reshape(1, -1) for b in b_segs]
        nw1 = norm1_w[l].reshape(1, d)

        x2d = x.reshape(bsz * n, d)
        xc2d = xc.reshape(bsz * nc, d)
        hy_args = (hy_w1[l], hy_b1[l].reshape(1, -1), hy_freq1[l].reshape(1, -1), hy_w2[l],
                   hy_b2[l].reshape(1, -1), hy_freq2[l].reshape(1, -1), hy_w3[l])
        front = (hy_short_w[l], hy_short_b[l].reshape(1, -1), gm_ws[l].astype(BF16),
                 jnp.repeat(gm_bs[l].T, mw // C_GROUPS, axis=1), gm_norm_w[l].reshape(1, mw), gm_norm_b[l].reshape(1, mw))
        pa, x0, u, ycm, pd = _in_full_call(x2d, mod_lat, nw1, w_segs, b_segs, *front, n, n, _row_tile(n), col_view)
        if need_ctx:
            ca, x0c, uc, ycm_c, cd = _in_full_call(xc2d, mod_ctx, nw1, w_segs, b_segs, *front, bsz * nc, nc,
                                                   _row_tile(bsz * nc))
        else:
            ca, cd = _in_call(xc2d, mod_ctx, nw1, [w_segs[0], w_segs[3]], [b_segs[0], b_segs[3]],
                              bsz * nc, _row_tile(bsz * nc))

        yac, ya = _mlstm_call(ca.reshape(bsz, nc, -1), pa.reshape(bsz, n, -1),
                              mlstm_norm_w[l].reshape(1, mw), need_ctx)
        ydc, yd = _hgrn_call(cd.reshape(bsz, nc, -1), pd.reshape(bsz, n, -1), hg_lb_logits,
                             hg_norm_w[l].reshape(1, mw), l, need_ctx)

        hyb = hy_bias[l].reshape(1, mw)
        wg, wb, wo = w_gate[l].astype(BF16), w_branch[l].astype(BF16), w_out[l].astype(BF16)
        wfa, wfg = w_ffn_in[l][:, :hidden].astype(BF16), w_ffn_in[l][:, hidden:].astype(BF16)
        wfd = w_ffn_out[l].astype(BF16)
        nw2 = norm2_w[l].reshape(1, d)
        fnw = final_norm_w.reshape(1, d)

        yconv = _hyena_conv(u.reshape(bsz, n, mw), _filter_call(n, *hy_args))
        x2d = _merge_call(x2d, mod_lat, nw1, ya.reshape(bsz * n, mw), x0, u, yconv.reshape(bsz * n, mw), ycm,
                          yd.reshape(bsz * n, mw), hyb, wg, wb, wo, n, _row_tile(n), col_view)
        x2d = _ffn_call(x2d, mod_lat, nw2, wfa, wfg, wfd, fnw, n, _row_tile(n), l == depth - 1)
        x = x2d.reshape(bsz, n, d)

        if need_ctx:
            tmc = _row_tile(bsz * nc)
            yconv_c = _hyena_conv(uc.reshape(bsz, nc, mw), _filter_call(nc, *hy_args))
            xc2d = _merge_call(xc2d, mod_ctx, nw1, yac.reshape(bsz * nc, mw), x0c, uc,
                               yconv_c.reshape(bsz * nc, mw), ycm_c, ydc.reshape(bsz * nc, mw),
                               hyb, wg, wb, wo, bsz * nc, tmc)
            xc2d = _ffn_call(xc2d, mod_ctx, nw2, wfa, wfg, wfd, fnw, bsz * nc, tmc, False)
            xc = xc2d.reshape(bsz, nc, d)
    return x
```

```python
import functools
import math

import numpy as np
import jax
import jax.numpy as jnp
from jax import lax
from jax.experimental import pallas as pl
from jax.experimental.pallas import tpu as pltpu

F32 = jnp.float32
BF16 = jnp.bfloat16

GRID_W = 64
NORM_EPS = 1e-6
N_BRANCH = 4
NEG_BIG = -1e30
F_EPS = 1e-30
HEAD_DIM = 64
HEADS = 4
LANE = 128
A_CHUNK = 128
D_CHUNK = 128
C_CHUNK = 128
C_GROUPS = 4
B_BANDS = 8
B_DECAY_TARGET = 1e-2
B_FAST_DECAY = 0.3
B_SLOW_DECAY = 1.5
MERGE_PARTS = 2
CONV_BLOCK = 256
VMEM_LIMIT = 56 * 1024 * 1024


def _cparams(n_axes):
    return pltpu.CompilerParams(dimension_semantics=("arbitrary",) * n_axes,
                                vmem_limit_bytes=VMEM_LIMIT)


def _split3(x):
    hi = x.astype(BF16)
    r1 = x - hi.astype(F32)
    mid = r1.astype(BF16)
    lo = (r1 - mid.astype(F32)).astype(BF16)
    return hi, mid, lo


def _dot_f32(a, b):
    a_hi, a_mid, _ = _split3(a)
    b_hi, b_mid, _ = _split3(b)
    d = lambda u, v: jnp.dot(u, v, preferred_element_type=F32)
    return d(a_hi, b_hi) + (d(a_hi, b_mid) + d(a_mid, b_hi))


def _sigmoid(x):
    return 1.0 / (1.0 + jnp.exp(-x))


def _sigmoid_both(x):
    e = jnp.exp(-jnp.abs(x))
    big = 1.0 / (1.0 + e)
    small = e * big
    pos = x >= 0.0
    return jnp.where(pos, big, small), jnp.where(pos, small, big)


def _silu(x):
    return x * _sigmoid(x)


def _log_sigmoid(x):
    return jnp.minimum(x, 0.0) - jnp.log(1.0 + jnp.exp(-jnp.abs(x)))


def _norm_mod(x, nw, shift, scale):
    ms = jnp.mean(x * x, axis=-1, keepdims=True)
    y = x * lax.rsqrt(ms + NORM_EPS) * nw
    return y * (1.0 + scale) + shift


def _ada_kernel(c_ref, w_ref, b_ref, o_ref):
    o_ref[0] = _dot_f32(_silu(c_ref[...]), w_ref[0]) + b_ref[0]


def _ada_call(cvec, ada_w, ada_b):
    depth, d, n6 = ada_w.shape
    mb = cvec.shape[0]
    tn = 512
    return pl.pallas_call(
        _ada_kernel,
        grid=(depth, n6 // tn),
        in_specs=[pl.BlockSpec((mb, d), lambda l, j: (0, 0)),
                  pl.BlockSpec((1, d, tn), lambda l, j: (l, 0, j)),
                  pl.BlockSpec((1, 1, tn), lambda l, j: (l, 0, j))],
        out_specs=pl.BlockSpec((1, mb, tn), lambda l, j: (l, 0, j)),
        out_shape=jax.ShapeDtypeStruct((depth, mb, n6), F32),
        compiler_params=_cparams(2),
        name="ada_mod",
    )(cvec, ada_w, ada_b.reshape(depth, 1, n6))


def _in_kernel(*refs, d, n_seg):
    x_ref, mod_ref, nw_ref = refs[:3]
    w_refs = refs[3:3 + n_seg]
    b_refs = refs[3 + n_seg:3 + 2 * n_seg]
    o_refs = refs[3 + 2 * n_seg:]
    h = _norm_mod(x_ref[...], nw_ref[...], mod_ref[0, :, 0:d], mod_ref[0, :, d:2 * d]).astype(BF16)
    for w_ref, b_ref, o_ref in zip(w_refs, b_refs, o_refs):
        o_ref[...] = jnp.dot(h, w_ref[...], preferred_element_type=F32) + b_ref[...]


def _in_call(x2d, mod, nw, ws, bs, rows_per_mod, tm):
    r, d = x2d.shape
    n_seg = len(ws)
    blocks_per_mod = rows_per_mod // tm
    const = lambda i: (0, 0)
    in_specs = [pl.BlockSpec((tm, d), lambda i: (i, 0)),
                pl.BlockSpec((1, 1, mod.shape[-1]), lambda i: (i // blocks_per_mod, 0, 0)),
                pl.BlockSpec((1, d), const)]
    in_specs += [pl.BlockSpec(w.shape, const) for w in ws]
    in_specs += [pl.BlockSpec(b.shape, const) for b in bs]
    return pl.pallas_call(
        functools.partial(_in_kernel, d=d, n_seg=n_seg),
        grid=(r // tm,),
        in_specs=in_specs,
        out_specs=[pl.BlockSpec((tm, w.shape[1]), lambda i: (i, 0)) for w in ws],
        out_shape=[jax.ShapeDtypeStruct((r, w.shape[1]), F32) for w in ws],
        compiler_params=_cparams(1),
        name="in_proj",
    )(x2d, mod, nw, *ws, *bs)


def _load_tile(x_ref):
    if len(x_ref.shape) == 2:
        return x_ref[...]
    return jnp.concatenate([x_ref[:, c, :] for c in range(x_ref.shape[1])], axis=0)


def _tile_specs(tm, d, col_view):
    if col_view is None:
        hb = tm // 8
        return (pl.BlockSpec((tm, d), lambda i: (i, 0)),
                lambda n_rows: pl.BlockSpec((8, d), lambda i: (jnp.maximum(i * hb - 1, 0), 0)),
                lambda n_rows: pl.BlockSpec((8, d), lambda i: (jnp.minimum((i + 1) * hb, n_rows // 8 - 1), 0)),
                lambda x2d: x2d)
    rows, cols = col_view
    ncol = tm // rows
    tps = cols // ncol
    hb = ncol // 8
    return (pl.BlockSpec((rows, ncol, d), lambda i: (i // tps, i % tps, 0)),
            lambda n_rows: pl.BlockSpec((rows, 8, d), lambda i: (i // tps, jnp.maximum((i % tps) * hb - 1, 0), 0)),
            lambda n_rows: pl.BlockSpec((rows, 8, d),
                                        lambda i: (i // tps, jnp.minimum((i % tps + 1) * hb, cols // 8 - 1), 0)),
            lambda x2d: x2d.reshape(-1, cols, d))


def _gmlp_chunk(x, ws_ref, bs_ref, nw_ref, nb_ref, lane_lo):
    g = 0.5 * x * (1.0 + jnp.tanh(math.sqrt(2.0 / math.pi) * (x + 0.044715 * (x * x * x))))
    u, v = g[:, 0:256], g[:, 256:512]
    mu = jnp.mean(v, axis=-1, keepdims=True)
    vc = v - mu
    var = jnp.mean(vc * vc, axis=-1, keepdims=True)
    vn = (vc * lax.rsqrt(var + NORM_EPS) * nw_ref[...] + nb_ref[...]).astype(BF16)
    mixed = []
    for pair in range(C_GROUPS // 2):
        v_p = vn[:, pair * LANE:(pair + 1) * LANE]
        mixed.append(jnp.where(lane_lo, jnp.dot(ws_ref[2 * pair], v_p, preferred_element_type=F32),
                               jnp.dot(ws_ref[2 * pair + 1], v_p, preferred_element_type=F32)))
    return u * (jnp.concatenate(mixed, axis=1) + bs_ref[...])


def _in_full_kernel(x_ref, xp_ref, xn_ref, mod_ref, nw_ref, wa_ref, wb_ref, wc_ref, wd_ref,
                    ba_ref, bb_ref, bc_ref, bd_ref, sw_ref, sb_ref, gws_ref, gbs_ref, gnw_ref, gnb_ref,
                    oa_ref, ox0_ref, ou_ref, oc_ref, od_ref, *, d, seq_len):
    shift, scale, nw = mod_ref[0, :, 0:d], mod_ref[0, :, d:2 * d], nw_ref[...]
    x = _load_tile(x_ref)
    tm = x.shape[0]
    h = _norm_mod(x, nw, shift, scale).astype(BF16)
    pc = jnp.dot(h, wc_ref[...], preferred_element_type=F32) + bc_ref[...]
    pb = jnp.dot(h, wb_ref[...], preferred_element_type=F32) + bb_ref[...]
    oa_ref[...] = jnp.dot(h, wa_ref[...], preferred_element_type=F32) + ba_ref[...]
    od_ref[...] = jnp.dot(h, wd_ref[...], preferred_element_type=F32) + bd_ref[...]
    lane_lo = lax.broadcasted_iota(jnp.int32, (C_CHUNK, LANE), 1) < 256 // C_GROUPS
    for c in range(tm // C_CHUNK):
        oc_ref[c * C_CHUNK:(c + 1) * C_CHUNK, :] = _gmlp_chunk(pc[c * C_CHUNK:(c + 1) * C_CHUNK], gws_ref, gbs_ref,
                                                                gnw_ref, gnb_ref, lane_lo)
    if len(xp_ref.shape) == 2:
        halo = jnp.concatenate([xp_ref[...], xn_ref[...]], axis=0)
    else:
        halo = jnp.concatenate([xp_ref[xp_ref.shape[0] - 1], xn_ref[0]], axis=0)
    pb_halo = jnp.dot(_norm_mod(halo, nw, shift, scale).astype(BF16), wb_ref[...],
                      preferred_element_type=F32) + bb_ref[...]
    row = lax.broadcasted_iota(jnp.int32, pb.shape, 0)
    pos = (pl.program_id(0) * tm + row) & (seq_len - 1)
    prev = jnp.where(row == 0, pb_halo[7:8, :], pltpu.roll(pb, 1, axis=0))
    nxt = jnp.where(row == tm - 1, pb_halo[8:9, :], pltpu.roll(pb, tm - 1, axis=0))
    prev = jnp.where(pos == 0, 0.0, prev)
    nxt = jnp.where(pos == seq_len - 1, 0.0, nxt)
    y = prev * sw_ref[0:1, :] + pb * sw_ref[1:2, :] + nxt * sw_ref[2:3, :] + sb_ref[...]
    ox0_ref[...] = y[:, 0:256]
    ou_ref[...] = y[:, 256:512] * y[:, 512:768]


def _in_full_call(x2d, mod, nw, ws, bs, conv_w, conv_b, gws, gbs, gnw, gnb, rows_per_mod, seq_len, tm,
                  col_view=None):
    r, d = x2d.shape
    assert seq_len & (seq_len - 1) == 0 and tm % C_CHUNK == 0
    blocks_per_mod = rows_per_mod // tm
    c2 = lambda i: (0, 0)
    full = lambda a: pl.BlockSpec(a.shape, (lambda i: (0,) * a.ndim))
    row = lambda w: pl.BlockSpec((tm, w), lambda i: (i, 0))
    x_spec, prev_spec, next_spec, view = _tile_specs(tm, d, col_view)
    in_specs = [x_spec, prev_spec(r), next_spec(r),
                pl.BlockSpec((1, 1, mod.shape[-1]), lambda i: (i // blocks_per_mod, 0, 0)),
                pl.BlockSpec((1, d), c2)]
    consts = list(ws) + list(bs) + [conv_w, conv_b, gws, gbs, gnw, gnb]
    in_specs += [full(a) for a in consts]
    widths = (ws[0].shape[1], 256, 256, 256, ws[3].shape[1])
    return pl.pallas_call(
        functools.partial(_in_full_kernel, d=d, seq_len=seq_len),
        grid=(r // tm,),
        in_specs=in_specs,
        out_specs=[row(w) for w in widths],
        out_shape=[jax.ShapeDtypeStruct((r, w), F32) for w in widths],
        compiler_params=_cparams(1),
        name="in_proj_full",
    )(view(x2d), view(x2d), view(x2d), mod, nw, *consts)


def _mlstm_ones(L):
    ones_blk = np.zeros((2 * L, LANE), np.float32)
    ones_blk[:L, :HEAD_DIM] = 1.0
    ones_blk[L:, HEAD_DIM:] = 1.0
    return ones_blk


def _running(op, fill, x, row, reverse):
    L = x.shape[0]
    sh = 1
    while sh < L:
        if reverse:
            shifted, valid = pltpu.roll(x, L - sh, axis=0), row < L - sh
        else:
            shifted, valid = pltpu.roll(x, sh, axis=0), row >= sh
        x = op(x, jnp.where(valid, shifted, fill))
        sh *= 2
    return x


def _mlstm_chunk(blk, ones_ref, cn_ref, m_ref, d, mask, row, lane_lo, sub_lo, bd):
    L = blk.shape[0]
    reverse = d == 1
    end = 0 if reverse else L - 1
    cum = _running(jnp.add, 0.0, _log_sigmoid(blk[:, 1152:1280]), row, reverse)
    a = blk[:, 1024:1152] - cum
    m_prev = m_ref[d:d + 1, :]
    g = jnp.maximum(_running(jnp.maximum, NEG_BIG, a, row, reverse), m_prev)
    g_end = g[end:end + 1, :]
    w_inter = jnp.exp(m_prev - g)
    e_negm = jnp.exp(-(cum + g))
    m_ref[d:d + 1, :] = cum[end:end + 1, :] + g_end
    a_t = a.T

    def head_lanes(x, pair):
        c = 4 * d + 2 * pair
        return jnp.where(lane_lo, jnp.broadcast_to(x[:, c:c + 1], (L, LANE)),
                         jnp.broadcast_to(x[:, c + 1:c + 2], (L, LANE)))

    outs = []
    for pair in range(HEADS // 2):
        cs = slice(128 * pair, 128 * pair + 128)
        q_pair = blk[:, cs].astype(BF16)
        k_pair = blk[:, 256 + 128 * pair:256 + 128 * pair + 128] * (HEAD_DIM ** -0.5)
        v_pair = blk[:, 512 + 128 * pair:512 + 128 * pair + 128]
        k_b = k_pair.astype(BF16)
        zero = jnp.zeros_like(k_b)
        k_rows = jnp.concatenate([jnp.where(lane_lo, k_b, zero), jnp.where(lane_lo, zero, k_b)], axis=0)
        s_pair = lax.dot_general(q_pair, k_rows, (((1,), (1,)), ((), ())), preferred_element_type=F32)
        w_halves = []
        for sub in range(2):
            hd = 2 * pair + sub
            c = 4 * d + hd
            expo = jnp.where(mask, a_t[c:c + 1, :] - jnp.broadcast_to(g[:, c:c + 1], (L, L)), NEG_BIG)
            w_halves.append((jnp.exp(expo) * s_pair[:, sub * L:(sub + 1) * L]).astype(BF16))
        w_pair = jnp.concatenate(w_halves, axis=1)
        v_b = v_pair.astype(BF16)
        v_rows = jnp.concatenate([jnp.where(lane_lo, v_b, zero), jnp.where(lane_lo, zero, v_b)], axis=0)
        intra = jnp.dot(w_pair, jnp.concatenate([v_rows, ones_ref[...]], axis=1), preferred_element_type=F32)
        slot = 2 * d + pair
        cn = cn_ref[slot]
        inter = jnp.dot(q_pair, cn.astype(BF16), preferred_element_type=F32)
        wi_p = head_lanes(w_inter, pair)
        num = wi_p * inter[:, 0:128] + intra[:, 0:128]
        den = wi_p * inter[:, 128:256] + intra[:, 128:256]
        outs.append(num / jnp.maximum(jnp.abs(den), head_lanes(e_negm, pair)))
        c_e = 4 * d + 2 * pair
        w_st = jnp.exp(jnp.where(sub_lo, a_t[c_e:c_e + 1, :] - g_end[:, c_e:c_e + 1],
                                 a_t[c_e + 1:c_e + 2, :] - g_end[:, c_e + 1:c_e + 2]))
        ktw = (k_pair.T * w_st).astype(BF16)
        upd = jnp.dot(ktw, jnp.concatenate([v_b, jnp.ones_like(v_b)], axis=1), preferred_element_type=F32)
        decay = wi_p[end:end + 1, :]
        cn_ref[slot] = jnp.concatenate([decay, decay], axis=1) * cn + jnp.where(bd, upd, 0.0)
    return jnp.concatenate(outs, axis=1)


def _mlstm_kernel(*refs, need_ctx):
    ctx_ref, lat_ref, ones_ref, nw_ref = refs[:4]
    if need_ctx:
        yc_ref, yl_ref, ybc_ref, ybl_ref, cn_ref, m_ref = refs[4:]
    else:
        yl_ref, ybc_ref, ybl_ref, cn_ref, m_ref = refs[4:]
        yc_ref = None
    L = A_CHUNK
    lane_lo = lax.broadcasted_iota(jnp.int32, (L, LANE), 1) < HEAD_DIM
    row = lax.broadcasted_iota(jnp.int32, (L, LANE), 0)
    sub_lo = row < HEAD_DIM
    r2 = lax.broadcasted_iota(jnp.int32, (L, L), 0)
    c2 = lax.broadcasted_iota(jnp.int32, (L, L), 1)
    mask_f, mask_b = r2 >= c2, r2 <= c2
    rb = lax.broadcasted_iota(jnp.int32, (LANE, 2 * LANE), 0) < HEAD_DIM
    cb = (lax.broadcasted_iota(jnp.int32, (LANE, 2 * LANE), 1) % LANE) < HEAD_DIM
    bd = rb == cb
    cn_ref[...] = jnp.zeros_like(cn_ref)
    m_ref[...] = jnp.zeros_like(m_ref)

    def scan(src_ref, yf_ref, yb_ref):
        n_chunks = src_ref.shape[1] // L

        def body(j, carry):
            sf = pl.multiple_of(j * L, L)
            sb = pl.multiple_of((n_chunks - 1 - j) * L, L)
            hf = _mlstm_chunk(src_ref[0, pl.ds(sf, L), :], ones_ref, cn_ref, m_ref, 0, mask_f, row, lane_lo, sub_lo, bd)
            hb = _mlstm_chunk(src_ref[0, pl.ds(sb, L), :], ones_ref, cn_ref, m_ref, 1, mask_b, row, lane_lo, sub_lo, bd)
            if yf_ref is not None:
                yf_ref[0, pl.ds(sf, L), :] = hf
                yb_ref[pl.ds(sb, L), :] = hb
            return carry

        lax.fori_loop(0, n_chunks, body, 0)

    def finalize(src_ref, y_ref, yb_ref):
        n_chunks = src_ref.shape[1] // L

        def body(j, carry):
            s = pl.multiple_of(j * L, L)
            for pair in range(HEADS // 2):
                cs = slice(128 * pair, 128 * pair + 128)
                y = y_ref[0, pl.ds(s, L), cs] + yb_ref[pl.ds(s, L), cs]
                y2 = y * y
                s_lo = jnp.sum(jnp.where(lane_lo, y2, 0.0), axis=-1, keepdims=True)
                s_all = jnp.sum(y2, axis=-1, keepdims=True)
                ms = jnp.where(lane_lo, s_lo, s_all - s_lo) * (1.0 / HEAD_DIM)
                o_gate = src_ref[0, pl.ds(s, L), 768 + 128 * pair:768 + 128 * pair + 128]
                y_ref[0, pl.ds(s, L), cs] = y * lax.rsqrt(ms + NORM_EPS) * nw_ref[:, cs] * _sigmoid(o_gate)
            return carry

        lax.fori_loop(0, n_chunks, body, 0)

    scan(ctx_ref, yc_ref, ybc_ref)
    scan(lat_ref, yl_ref, ybl_ref)
    if need_ctx:
        finalize(ctx_ref, yc_ref, ybc_ref)
    finalize(lat_ref, yl_ref, ybl_ref)


def _mlstm_call(pa_ctx, pa_lat, nw, need_ctx):
    b, nc, wa = pa_ctx.shape
    n = pa_lat.shape[1]
    L = A_CHUNK
    ones_blk = jnp.asarray(_mlstm_ones(L), BF16)
    c2 = lambda i: (0, 0)
    out_specs = [pl.BlockSpec((1, n, 256), lambda i: (i, 0, 0))]
    out_shape = [jax.ShapeDtypeStruct((b, n, 256), F32)]
    if need_ctx:
        out_specs.insert(0, pl.BlockSpec((1, nc, 256), lambda i: (i, 0, 0)))
        out_shape.insert(0, jax.ShapeDtypeStruct((b, nc, 256), F32))
    res = pl.pallas_call(
        functools.partial(_mlstm_kernel, need_ctx=need_ctx),
        grid=(b,),
        in_specs=[pl.BlockSpec((1, nc, wa), lambda i: (i, 0, 0)),
                  pl.BlockSpec((1, n, wa), lambda i: (i, 0, 0)),
                  pl.BlockSpec(ones_blk.shape, c2),
                  pl.BlockSpec((1, 256), c2)],
        out_specs=out_specs,
        out_shape=out_shape,
        scratch_shapes=[pltpu.VMEM((nc, 256), F32), pltpu.VMEM((n, 256), F32),
                        pltpu.VMEM((HEADS, LANE, 2 * LANE), F32), pltpu.VMEM((8, LANE), F32)],
        compiler_params=_cparams(1),
        name="mlstm",
    )(pa_ctx, pa_lat, ones_blk, nw)
    return (res[0], res[1]) if need_ctx else (None, res[0])


D_FAST_BASE = 16
D_FAST_MAX_EXPONENT = 60.0


def _hgrn_tables(L, base):
    t = np.arange(L)
    lev0 = int(math.log2(base))
    n_split = int(math.log2(L)) - lev0
    same_base = (t[:, None] // base) == (t[None, :] // base)
    lvl_f = np.where(same_base & (t[:, None] >= t[None, :]), 0, -1)
    lvl_b = np.where(same_base & (t[:, None] <= t[None, :]), 0, -1)
    mids_f, mids_b = [], []
    for i in range(n_split):
        half, size = base << i, base << (i + 1)
        same = (t[:, None] // size) == (t[None, :] // size)
        upper = (t // half) % 2 == 1
        lvl_f = np.where(same & upper[:, None] & ~upper[None, :], i + 1, lvl_f)
        lvl_b = np.where(same & ~upper[:, None] & upper[None, :], i + 1, lvl_b)
        starts = np.arange(0, L, size)
        mids_f.append([(int(s), size, int(s) + half - 1) for s in starts])
        mids_b.append([(int(s), size, int(s) + half) for s in starts])
    wide = lambda m: np.concatenate([m, m], axis=-1).astype(np.int32)
    return wide(lvl_f), wide(lvl_b), mids_f, mids_b


def _rows_of(b, spec):
    return jnp.concatenate([jnp.broadcast_to(b[r:r + 1, :], (n, b.shape[1])) for _, n, r in spec], axis=0)


def _hgrn_chunk(blk, f_pre, lb, lvl_ref, mids, s_ref, x_ref, d, row, lane_lo, bd, base):
    L = blk.shape[0]
    reverse = d == 1
    end = 0 if reverse else L - 1
    qs = _silu(blk[:, 0:256])
    v = blk[:, 256:512]
    sig, sig_neg = _sigmoid_both(f_pre)
    log_f = jnp.log(jnp.maximum(lb + (1.0 - lb) * sig, F_EPS))
    k = (1.0 - lb) * sig_neg
    b = _running(jnp.add, 0.0, log_f, row, reverse)
    b_end = b[end:end + 1, :]
    q_in = qs * jnp.exp(b)
    k_out = k * jnp.exp(b_end - b)
    decay = jnp.exp(b_end)
    factors = []
    if base > 1:
        zero_row = jnp.zeros((1, b.shape[1]), F32)
        if reverse:
            refs = [b[s + base:s + base + 1, :] if s + base < L else zero_row for s in range(0, L, base)]
        else:
            refs = [b[s - 1:s, :] if s > 0 else zero_row for s in range(0, L, base)]
        r0 = jnp.concatenate([jnp.broadcast_to(r, (base, b.shape[1])) for r in refs], axis=0)
        factors.append((jnp.exp(b - r0), jnp.exp(r0 - b)))
        last = [b[s:s + 1, :] if reverse else b[s + base - 1:s + base, :] for s in range(0, L, base)]
        worst = functools.reduce(jnp.maximum, [r - e for r, e in zip(refs, last)])
        x_ref[d:d + 1, :] = jnp.maximum(x_ref[d:d + 1, :], worst)
    else:
        factors.append((None, None))
    for spec in mids:
        e = jnp.exp(-jnp.abs(b - _rows_of(b, spec)))
        factors.append((e, e))
    nt = (((1,), (1,)), ((), ()))
    lvl = lvl_ref[...]
    outs = []
    for pair in range(HEADS // 2):
        cs = slice(128 * pair, 128 * pair + 128)
        q_p, k_p = qs[:, cs], k[:, cs]
        zero = jnp.zeros((L, LANE), BF16)
        v_b = v[:, cs].astype(BF16)
        v_rows = jnp.concatenate([jnp.where(lane_lo, v_b, zero), jnp.where(lane_lo, zero, v_b)], axis=0)
        scores = jnp.zeros((L, 2 * L), F32)
        for i, (eq, ek) in enumerate(factors):
            q_l = (q_p if eq is None else q_p * eq[:, cs]).astype(BF16)
            k_l = (k_p if ek is None else k_p * ek[:, cs]).astype(BF16)
            k_rows = jnp.concatenate([jnp.where(lane_lo, k_l, zero), jnp.where(lane_lo, zero, k_l)], axis=0)
            scores = jnp.where(lvl == i, lax.dot_general(q_l, k_rows, nt, preferred_element_type=F32), scores)
        slot = 2 * d + pair
        s_t = s_ref[slot]
        outs.append(jnp.dot(scores.astype(BF16), v_rows, preferred_element_type=F32)
                    + lax.dot_general(q_in[:, cs].astype(BF16), s_t.astype(BF16), nt, preferred_element_type=F32))
        upd = lax.dot_general(v_b, k_out[:, cs].astype(BF16), (((0,), (0,)), ((), ())), preferred_element_type=F32)
        s_ref[slot] = decay[:, cs] * s_t + jnp.where(bd, upd, 0.0)
    return jnp.concatenate(outs, axis=1)


def _hgrn_kernel(*refs, need_ctx, layer, mids_fast, mids_safe):
    ctx_ref, lat_ref, lvlf_fast_ref, lvlb_fast_ref, lvlf_safe_ref, lvlb_safe_ref, lbl_ref, nw_ref = refs[:8]
    if need_ctx:
        yc_ref, yl_ref, ybc_ref, ybl_ref, s_ref, x_ref = refs[8:]
    else:
        yl_ref, ybc_ref, ybl_ref, s_ref, x_ref = refs[8:]
        yc_ref = None
    L = D_CHUNK
    lane_lo = lax.broadcasted_iota(jnp.int32, (L, LANE), 1) < HEAD_DIM
    row = lax.broadcasted_iota(jnp.int32, (L, 2 * LANE), 0)
    bd = ((lax.broadcasted_iota(jnp.int32, (LANE, LANE), 0) < HEAD_DIM)
          == (lax.broadcasted_iota(jnp.int32, (LANE, LANE), 1) < HEAD_DIM))
    logits = lbl_ref[...]
    e = jnp.exp(logits - jnp.max(logits, axis=0, keepdims=True))
    prob = e / jnp.sum(e, axis=0, keepdims=True)
    lb = jnp.sum(prob[0:layer + 1], axis=0, keepdims=True) - prob[0:1]
    def scan(src_ref, yf_ref, yb_ref, lvlf_ref, lvlb_ref, mids, base):
        n_chunks = src_ref.shape[1] // L

        def body(j, carry):
            sf = pl.multiple_of(j * L, L)
            sb = pl.multiple_of((n_chunks - 1 - j) * L, L)
            blk_f = src_ref[0, pl.ds(sf, L), :]
            blk_b = src_ref[0, pl.ds(sb, L), :]
            of = _hgrn_chunk(blk_f, blk_f[:, 512:768], lb, lvlf_ref, mids[0], s_ref, x_ref, 0, row, lane_lo, bd, base)
            ob = _hgrn_chunk(blk_b, blk_b[:, 768:1024], lb, lvlb_ref, mids[1], s_ref, x_ref, 1, row, lane_lo, bd, base)
            if yf_ref is not None:
                yf_ref[0, pl.ds(sf, L), :] = of
                yb_ref[pl.ds(sb, L), :] = ob
            return carry

        lax.fori_loop(0, n_chunks, body, 0)

    def scans(lvlf_ref, lvlb_ref, mids, base):
        s_ref[...] = jnp.zeros_like(s_ref)
        scan(ctx_ref, yc_ref, ybc_ref, lvlf_ref, lvlb_ref, mids, base)
        scan(lat_ref, yl_ref, ybl_ref, lvlf_ref, lvlb_ref, mids, base)

    x_ref[...] = jnp.zeros_like(x_ref)
    scans(lvlf_fast_ref, lvlb_fast_ref, mids_fast, D_FAST_BASE)

    @pl.when(jnp.logical_not(jnp.max(x_ref[...]) <= D_FAST_MAX_EXPONENT))
    def _():
        scans(lvlf_safe_ref, lvlb_safe_ref, mids_safe, 1)

    def finalize(src_ref, y_ref, yb_ref):
        n_chunks = src_ref.shape[1] // L

        def body(j, carry):
            s = pl.multiple_of(j * L, L)
            for pair in range(HEADS // 2):
                cs = slice(128 * pair, 128 * pair + 128)
                y = y_ref[0, pl.ds(s, L), cs] + yb_ref[pl.ds(s, L), cs]
                y2 = y * y
                s_lo = jnp.sum(jnp.where(lane_lo, y2, 0.0), axis=-1, keepdims=True)
                s_all = jnp.sum(y2, axis=-1, keepdims=True)
                ms = jnp.where(lane_lo, s_lo, s_all - s_lo) * (1.0 / HEAD_DIM)
                g = src_ref[0, pl.ds(s, L), 1024 + 128 * pair:1024 + 128 * pair + 128]
                y_ref[0, pl.ds(s, L), cs] = y * lax.rsqrt(ms + NORM_EPS) * nw_ref[:, cs] * _silu(g)
            return carry

        lax.fori_loop(0, n_chunks, body, 0)

    if need_ctx:
        finalize(ctx_ref, yc_ref, ybc_ref)
    finalize(lat_ref, yl_ref, ybl_ref)


def _hgrn_call(pd_ctx, pd_lat, lb_logits, nw, layer, need_ctx):
    b, nc, wd = pd_ctx.shape
    n = pd_lat.shape[1]
    L = D_CHUNK
    lvlf_fast, lvlb_fast, midsf_fast, midsb_fast = _hgrn_tables(L, D_FAST_BASE)
    lvlf_safe, lvlb_safe, midsf_safe, midsb_safe = _hgrn_tables(L, 1)
    tables = [jnp.asarray(a) for a in (lvlf_fast, lvlb_fast, lvlf_safe, lvlb_safe)]
    c2 = lambda i: (0, 0)
    out_specs = [pl.BlockSpec((1, n, 256), lambda i: (i, 0, 0))]
    out_shape = [jax.ShapeDtypeStruct((b, n, 256), F32)]
    if need_ctx:
        out_specs.insert(0, pl.BlockSpec((1, nc, 256), lambda i: (i, 0, 0)))
        out_shape.insert(0, jax.ShapeDtypeStruct((b, nc, 256), F32))
    res = pl.pallas_call(
        functools.partial(_hgrn_kernel, need_ctx=need_ctx, layer=layer,
                          mids_fast=(midsf_fast, midsb_fast), mids_safe=(midsf_safe, midsb_safe)),
        grid=(b,),
        in_specs=[pl.BlockSpec((1, nc, wd), lambda i: (i, 0, 0)),
                  pl.BlockSpec((1, n, wd), lambda i: (i, 0, 0))]
                 + [pl.BlockSpec(a.shape, c2) for a in tables]
                 + [pl.BlockSpec(lb_logits.shape, c2), pl.BlockSpec((1, 256), c2)],
        out_specs=out_specs,
        out_shape=out_shape,
        scratch_shapes=[pltpu.VMEM((nc, 256), F32), pltpu.VMEM((n, 256), F32),
                        pltpu.VMEM((HEADS, LANE, LANE), F32), pltpu.VMEM((8, 256), F32)],
        compiler_params=_cparams(1),
        name="hgrn2",
    )(pd_ctx, pd_lat, *tables, lb_logits, nw)
    return (res[0], res[1]) if need_ctx else (None, res[0])


def _filter_feats(n):
    t = np.linspace(0.0, 1.0, n, dtype=np.float32)[:, None]
    bands = np.linspace(1e-4, B_BANDS - 1, B_BANDS, dtype=np.float32)[None]
    ang = (np.float32(2 * math.pi) * bands * np.arange(n, dtype=np.float32)[:, None] / np.float32(n)).astype(np.float32)
    z = np.concatenate([t, np.cos(ang), -np.sin(ang)], axis=-1).astype(np.float32)
    deltas = np.abs(np.linspace(math.log(B_DECAY_TARGET) / B_SLOW_DECAY,
                                math.log(B_DECAY_TARGET) / B_FAST_DECAY, 256, dtype=np.float32))
    neg_t_deltas = (-t * deltas[None]).astype(np.float32)
    return z, neg_t_deltas


def _filter_kernel(z_ref, ntd_ref, w1_ref, b1_ref, f1_ref, w2_ref, b2_ref, f2_ref, w3_ref, flip_ref, o_ref):
    n = z_ref.shape[0]
    T = flip_ref.shape[0]
    hd = jnp.sin(f1_ref[...] * (_dot_f32(z_ref[...], w1_ref[...]) + b1_ref[...]))
    hd = jnp.sin(f2_ref[...] * (_dot_f32(hd, w2_ref[...]) + b2_ref[...]))
    decay = jnp.exp(ntd_ref[...])
    fwd = _dot_f32(hd, w3_ref[:, 0:256]) * decay
    bwd = _dot_f32(hd, w3_ref[:, 256:512]) * decay
    flip = flip_ref[...]
    rev = []
    for i in range(n // T):
        parts = _split3(bwd[n - (i + 1) * T:n - i * T])
        rev.append(sum(jnp.dot(flip, p, preferred_element_type=F32) for p in parts))
    rev = jnp.concatenate(rev, axis=0)
    row = lax.broadcasted_iota(jnp.int32, rev.shape, 0)
    o_ref[0:n, :] = jnp.where(row == 0, 0.0, pltpu.roll(rev, 1, axis=0))
    o_ref[n:2 * n, :] = fwd


def _filter_call(n, w1, b1, f1, w2, b2, f2, w3):
    z, ntd = _filter_feats(n)
    k_pad = 32
    z = np.pad(z, ((0, 0), (0, k_pad - z.shape[1])))
    w1 = jnp.pad(w1, ((0, k_pad - w1.shape[0]), (0, 0)))
    flip = jnp.asarray(np.eye(CONV_BLOCK, dtype=np.float32)[::-1], BF16)
    args = (jnp.asarray(z), jnp.asarray(ntd), w1, b1, f1, w2, b2, f2, w3, flip)
    return pl.pallas_call(
        _filter_kernel,
        grid=(1,),
        in_specs=[pl.BlockSpec(a.shape, lambda i: (0, 0)) for a in args],
        out_specs=pl.BlockSpec((2 * n, 256), lambda i: (0, 0)),
        out_shape=jax.ShapeDtypeStruct((2 * n, 256), F32),
        compiler_params=_cparams(1),
        name="hyena_filter",
    )(*args)


def _long_conv_kernel(u_ref, k_ref, o_ref, *, n_blk, bsz, ch_per_step):
    T = CONV_BLOCK
    half = T // 2
    for c in range(ch_per_step):
        taps = k_ref[c]
        rolled = pltpu.roll(jnp.broadcast_to(taps, (half, taps.shape[1])), 0, axis=1, stride=1, stride_axis=0)
        acc = [None] * n_blk
        for d in range(-(n_blk - 1), n_blk):
            base = (d + n_blk) * T
            tile = jnp.concatenate([rolled[:, base:base + T], rolled[:, base - half:base - half + T]],
                                   axis=0).astype(BF16)
            j_lo, j_hi = max(0, -d), min(n_blk, n_blk - d)
            lhs = u_ref[c, j_lo * bsz:j_hi * bsz, :].astype(BF16)
            res = jnp.dot(lhs, tile, preferred_element_type=F32)
            for j in range(j_lo, j_hi):
                part = res[(j - j_lo) * bsz:(j - j_lo + 1) * bsz]
                i = j + d
                acc[i] = part if acc[i] is None else acc[i] + part
        o_ref[c] = jnp.concatenate(acc, axis=0)


def _long_conv_call(u_t, k_t, n_blk, bsz, ch_per_step):
    ch, rows, T = u_t.shape
    return pl.pallas_call(
        functools.partial(_long_conv_kernel, n_blk=n_blk, bsz=bsz, ch_per_step=ch_per_step),
        grid=(ch // ch_per_step,),
        in_specs=[pl.BlockSpec((ch_per_step, rows, T), lambda i: (i, 0, 0)),
                  pl.BlockSpec((ch_per_step, 1, k_t.shape[-1]), lambda i: (i, 0, 0))],
        out_specs=pl.BlockSpec((ch_per_step, rows, T), lambda i: (i, 0, 0)),
        out_shape=jax.ShapeDtypeStruct((ch, rows, T), F32),
        compiler_params=_cparams(1),
        name="hyena_long_conv",
    )(u_t, k_t)


def _hyena_conv(u, filt):
    bsz, n, ch = u.shape
    T = CONV_BLOCK
    n_blk = n // T
    k_t = filt.T.reshape(ch, 1, 2 * n)
    u_t = u.reshape(bsz, n_blk, T, ch).transpose(3, 1, 0, 2).reshape(ch, n_blk * bsz, T)
    y_t = _long_conv_call(u_t, k_t, n_blk, bsz, 4)
    return y_t.reshape(ch, n_blk, bsz, T).transpose(2, 1, 3, 0).reshape(bsz, n, ch)


def _merge_kernel(x_ref, mod_ref, nw_ref, ya_ref, x0_ref, u_ref, yconv_ref, yc_ref, yd_ref, hyb_ref,
                  wg_ref, wb_ref, wo_ref, o_ref, *, d):
    x = x_ref[...]
    h = _norm_mod(x, nw_ref[...], mod_ref[0, :, 0:d], mod_ref[0, :, d:2 * d]).astype(BF16)
    y_b = _load_tile(x0_ref) * (_load_tile(yconv_ref) + _load_tile(u_ref) * hyb_ref[...])
    ys = (_load_tile(ya_ref), y_b, _load_tile(yc_ref), _load_tile(yd_ref))
    ys = [v.astype(BF16) for v in ys]
    y = None
    n_part = d // MERGE_PARTS
    for part_i in range(MERGE_PARTS):
        cs = slice(part_i * n_part, (part_i + 1) * n_part)
        acc = None
        for j in range(N_BRANCH):
            gate = _sigmoid(jnp.dot(h, wg_ref[j, :, cs], preferred_element_type=F32))
            term = gate * jnp.dot(ys[j], wb_ref[j, :, cs], preferred_element_type=F32)
            acc = term if acc is None else acc + term
        part = jnp.dot(acc.astype(BF16), wo_ref[cs, :], preferred_element_type=F32)
        y = part if y is None else y + part
    o_ref[...] = x + mod_ref[0, :, 2 * d:3 * d] * y


def _merge_call(x2d, mod, nw, ya, x0, u, yconv, yc, yd, hyb, wg, wb, wo, rows_per_mod, tm, col_view=None):
    r, d = x2d.shape
    blocks_per_mod = rows_per_mod // tm
    c2 = lambda i: (0, 0)
    c3 = lambda i: (0, 0, 0)
    row = lambda w: pl.BlockSpec((tm, w), lambda i: (i, 0))
    if col_view is None:
        y_spec, view = row(256), (lambda y: y)
    else:
        rows, cols = col_view
        nrow = tm // cols
        tps = rows // nrow
        y_spec = pl.BlockSpec((cols, nrow, 256), lambda i: (i // tps, i % tps, 0))
        view = lambda y: y.reshape(-1, rows, 256)
    return pl.pallas_call(
        functools.partial(_merge_kernel, d=d),
        grid=(r // tm,),
        in_specs=[row(d), pl.BlockSpec((1, 1, mod.shape[-1]), lambda i: (i // blocks_per_mod, 0, 0)),
                  pl.BlockSpec((1, d), c2), y_spec, y_spec, y_spec, y_spec, y_spec, y_spec,
                  pl.BlockSpec((1, 256), c2),
                  pl.BlockSpec(wg.shape, c3), pl.BlockSpec(wb.shape, c3), pl.BlockSpec(wo.shape, c2)],
        out_specs=row(d),
        out_shape=jax.ShapeDtypeStruct((r, d), F32),
        compiler_params=_cparams(1),
        name="merge",
    )(x2d, mod, nw, view(ya), view(x0), view(u), view(yconv), view(yc), view(yd), hyb, wg, wb, wo)


def _ffn_kernel(x_ref, mod_ref, nw_ref, wu_ref, wd_ref, fnw_ref, o_ref, *, d, hid_chunk, final_norm):
    x = x_ref[...]
    h = _norm_mod(x, nw_ref[...], mod_ref[0, :, 3 * d:4 * d], mod_ref[0, :, 4 * d:5 * d]).astype(BF16)
    hidden = wd_ref.shape[0]
    acc = None
    for c in range(hidden // hid_chunk):
        cs = slice(c * hid_chunk, (c + 1) * hid_chunk)
        a = jnp.dot(h, wu_ref[:, cs], preferred_element_type=F32)
        g = jnp.dot(h, wu_ref[:, hidden + c * hid_chunk:hidden + (c + 1) * hid_chunk], preferred_element_type=F32)
        term = jnp.dot((_silu(a) * g).astype(BF16), wd_ref[cs, :], preferred_element_type=F32)
        acc = term if acc is None else acc + term
    y = x + mod_ref[0, :, 5 * d:6 * d] * acc
    if final_norm:
        y = y * lax.rsqrt(jnp.mean(y * y, axis=-1, keepdims=True) + NORM_EPS) * fnw_ref[...]
    o_ref[...] = y


def _ffn_call(x2d, mod, nw, wu, wd, fnw, rows_per_mod, tm, final_norm):
    r, d = x2d.shape
    blocks_per_mod = rows_per_mod // tm
    c2 = lambda i: (0, 0)
    return pl.pallas_call(
        functools.partial(_ffn_kernel, d=d, hid_chunk=256, final_norm=final_norm),
        grid=(r // tm,),
        in_specs=[pl.BlockSpec((tm, d), lambda i: (i, 0)),
                  pl.BlockSpec((1, 1, mod.shape[-1]), lambda i: (i // blocks_per_mod, 0, 0)),
                  pl.BlockSpec((1, d), c2),
                  pl.BlockSpec(wu.shape, c2), pl.BlockSpec(wd.shape, c2),
                  pl.BlockSpec((1, d), c2)],
        out_specs=pl.BlockSpec((tm, d), lambda i: (i, 0)),
        out_shape=jax.ShapeDtypeStruct((r, d), F32),
        compiler_params=_cparams(1),
        name="ffn",
    )(x2d, mod, nw, wu, wd, fnw)


def _row_tile(rows):
    return 512 if rows % 512 == 0 else 256


def kernel(x, c, ctx, c_ctx, ada_w, ada_b, norm1_w, norm2_w, w_in, b_in, mlstm_norm_w, hy_short_w, hy_short_b, hy_w1, hy_b1, hy_freq1, hy_w2, hy_b2, hy_freq2, hy_w3, hy_bias, gm_norm_w, gm_norm_b, gm_ws, gm_bs, hg_lb_logits, hg_norm_w, w_gate, w_branch, w_out, w_ffn_in, w_ffn_out, final_norm_w):
    bsz, n, d = x.shape
    nc = ctx.shape[1]
    depth = ada_w.shape[0]
    rows = n // GRID_W
    mw = d // N_BRANCH
    assert mw == 256 and mw == HEADS * HEAD_DIM

    mb = -(-(bsz + 1) // 8) * 8
    cvec = jnp.concatenate([c, c_ctx[None], jnp.zeros((mb - bsz - 1, d), F32)], axis=0)
    mods = _ada_call(cvec, ada_w, ada_b)

    s_a, s_g, s_b, s_c = 4 * mw, 4 * mw + 4 * HEADS, 7 * mw + 4 * HEADS, 9 * mw + 4 * HEADS
    gate_pad = LANE - 2 * HEADS

    xc = ctx
    for l in range(depth):
        need_ctx = l < depth - 1
        col_major = l % 2 == 1
        col_view = (rows, GRID_W) if col_major else None
        mod_lat = mods[l, :bsz].reshape(bsz, 1, 6 * d)
        mod_ctx = mods[l, bsz:bsz + 1].reshape(1, 1, 6 * d)
        wl, bl = w_in[l].astype(BF16), b_in[l]
        wgt, bgt = wl[:, s_a:s_g].reshape(d, 4, HEADS), bl[s_a:s_g].reshape(4, HEADS)
        zw, zb = jnp.zeros((d, gate_pad), BF16), jnp.zeros((gate_pad,), F32)
        w_segs = [jnp.concatenate([wl[:, :s_a], wgt[:, 0], wgt[:, 2], zw, wgt[:, 1], wgt[:, 3], zw], axis=1),
                  wl[:, s_g:s_b], wl[:, s_b:s_c], wl[:, s_c:]]
        b_segs = [jnp.concatenate([bl[:s_a], bgt[0], bgt[2], zb, bgt[1], bgt[3], zb]),
                  bl[s_g:s_b], bl[s_b:s_c], bl[s_c:]]
        b_segs = [b.reshape(1, -1) for b in b_segs]
        nw1 = norm1_w[l].reshape(1, d)

        x2d = x.reshape(bsz * n, d)
        xc2d = xc.reshape(bsz * nc, d)
        hy_args = (hy_w1[l], hy_b1[l].reshape(1, -1), hy_freq1[l].reshape(1, -1), hy_w2[l],
                   hy_b2[l].reshape(1, -1), hy_freq2[l].reshape(1, -1), hy_w3[l])
        front = (hy_short_w[l], hy_short_b[l].reshape(1, -1), gm_ws[l].astype(BF16),
                 jnp.repeat(gm_bs[l].T, mw // C_GROUPS, axis=1), gm_norm_w[l].reshape(1, mw), gm_norm_b[l].reshape(1, mw))
        pa, x0, u, ycm, pd = _in_full_call(x2d, mod_lat, nw1, w_segs, b_segs, *front, n, n, _row_tile(n), col_view)
        if need_ctx:
            ca, x0c, uc, ycm_c, cd = _in_full_call(xc2d, mod_ctx, nw1, w_segs, b_segs, *front, bsz * nc, nc,
                                                   _row_tile(bsz * nc))
        else:
            ca, cd = _in_call(xc2d, mod_ctx, nw1, [w_segs[0], w_segs[3]], [b_segs[0], b_segs[3]],
                              bsz * nc, _row_tile(bsz * nc))

        yac, ya = _mlstm_call(ca.reshape(bsz, nc, -1), pa.reshape(bsz, n, -1),
                              mlstm_norm_w[l].reshape(1, mw), need_ctx)
        ydc, yd = _hgrn_call(cd.reshape(bsz, nc, -1), pd.reshape(bsz, n, -1), hg_lb_logits,
                             hg_norm_w[l].reshape(1, mw), l, need_ctx)

        hyb = hy_bias[l].reshape(1, mw)
        wg, wb, wo = w_gate[l].astype(BF16), w_branch[l].astype(BF16), w_out[l].astype(BF16)
        wfu = w_ffn_in[l].astype(BF16)
        wfd = w_ffn_out[l].astype(BF16)
        nw2 = norm2_w[l].reshape(1, d)
        fnw = final_norm_w.reshape(1, d)

        yconv = _hyena_conv(u.reshape(bsz, n, mw), _filter_call(n, *hy_args))
        x2d = _merge_call(x2d, mod_lat, nw1, ya.reshape(bsz * n, mw), x0, u, yconv.reshape(bsz * n, mw), ycm,
                          yd.reshape(bsz * n, mw), hyb, wg, wb, wo, n, _row_tile(n), col_view)
        x2d = _ffn_call(x2d, mod_lat, nw2, wfu, wfd, fnw, n, _row_tile(n), l == depth - 1)
        x = x2d.reshape(bsz, n, d)

        if need_ctx:
            tmc = _row_tile(bsz * nc)
            yconv_c = _hyena_conv(uc.reshape(bsz, nc, mw), _filter_call(nc, *hy_args))
            xc2d = _merge_call(xc2d, mod_ctx, nw1, yac.reshape(bsz * nc, mw), x0c, uc,
                               yconv_c.reshape(bsz * nc, mw), ycm_c, ydc.reshape(bsz * nc, mw),
                               hyb, wg, wb, wo, bsz * nc, tmc)
            xc2d = _ffn_call(xc2d, mod_ctx, nw2, wfu, wfd, fnw, bsz * nc, tmc, False)
            xc = xc2d.reshape(bsz, nc, d)
    return x
```

```python
import functools
import math

import numpy as np
import jax
import jax.numpy as jnp
from jax import lax
from jax.experimental import pallas as pl
from jax.experimental.pallas import tpu as pltpu

F32 = jnp.float32
BF16 = jnp.bfloat16

GRID_W = 64
NORM_EPS = 1e-6
N_BRANCH = 4
NEG_BIG = -1e30
F_EPS = 1e-30
HEAD_DIM = 64
HEADS = 4
LANE = 128
A_CHUNK = 128
D_CHUNK = 128
C_CHUNK = 128
C_GROUPS = 4
B_BANDS = 8
B_DECAY_TARGET = 1e-2
B_FAST_DECAY = 0.3
B_SLOW_DECAY = 1.5
MERGE_PARTS = 2
CONV_BLOCK = 256
VMEM_LIMIT = 56 * 1024 * 1024


def _cparams(n_axes):
    return pltpu.CompilerParams(dimension_semantics=("arbitrary",) * n_axes,
                                vmem_limit_bytes=VMEM_LIMIT)


def _split3(x):
    hi = x.astype(BF16)
    r1 = x - hi.astype(F32)
    mid = r1.astype(BF16)
    lo = (r1 - mid.astype(F32)).astype(BF16)
    return hi, mid, lo


def _dot_f32(a, b):
    a_hi, a_mid, _ = _split3(a)
    b_hi, b_mid, _ = _split3(b)
    d = lambda u, v: jnp.dot(u, v, preferred_element_type=F32)
    return d(a_hi, b_hi) + (d(a_hi, b_mid) + d(a_mid, b_hi))


def _sigmoid(x):
    return 1.0 / (1.0 + jnp.exp(-x))


def _sigmoid_both(x):
    e = jnp.exp(-jnp.abs(x))
    big = 1.0 / (1.0 + e)
    small = e * big
    pos = x >= 0.0
    return jnp.where(pos, big, small), jnp.where(pos, small, big)


def _silu(x):
    return x * _sigmoid(x)


def _log_sigmoid(x):
    return jnp.minimum(x, 0.0) - jnp.log(1.0 + jnp.exp(-jnp.abs(x)))


def _norm_mod(x, nw, shift, scale):
    ms = jnp.mean(x * x, axis=-1, keepdims=True)
    y = x * lax.rsqrt(ms + NORM_EPS) * nw
    return y * (1.0 + scale) + shift


def _ada_kernel(c_ref, w_ref, b_ref, o_ref):
    o_ref[0] = _dot_f32(_silu(c_ref[...]), w_ref[0]) + b_ref[0]


def _ada_call(cvec, ada_w, ada_b):
    depth, d, n6 = ada_w.shape
    mb = cvec.shape[0]
    tn = 512
    return pl.pallas_call(
        _ada_kernel,
        grid=(depth, n6 // tn),
        in_specs=[pl.BlockSpec((mb, d), lambda l, j: (0, 0)),
                  pl.BlockSpec((1, d, tn), lambda l, j: (l, 0, j)),
                  pl.BlockSpec((1, 1, tn), lambda l, j: (l, 0, j))],
        out_specs=pl.BlockSpec((1, mb, tn), lambda l, j: (l, 0, j)),
        out_shape=jax.ShapeDtypeStruct((depth, mb, n6), F32),
        compiler_params=_cparams(2),
        name="ada_mod",
    )(cvec, ada_w, ada_b.reshape(depth, 1, n6))


def _in_kernel(*refs, d, n_seg):
    x_ref, mod_ref, nw_ref = refs[:3]
    w_refs = refs[3:3 + n_seg]
    b_refs = refs[3 + n_seg:3 + 2 * n_seg]
    o_refs = refs[3 + 2 * n_seg:]
    h = _norm_mod(x_ref[...], nw_ref[...], mod_ref[0, :, 0:d], mod_ref[0, :, d:2 * d]).astype(BF16)
    for w_ref, b_ref, o_ref in zip(w_refs, b_refs, o_refs):
        o_ref[...] = jnp.dot(h, w_ref[...], preferred_element_type=F32) + b_ref[...]


def _in_call(x2d, mod, nw, ws, bs, rows_per_mod, tm):
    r, d = x2d.shape
    n_seg = len(ws)
    blocks_per_mod = rows_per_mod // tm
    const = lambda i: (0, 0)
    in_specs = [pl.BlockSpec((tm, d), lambda i: (i, 0)),
                pl.BlockSpec((1, 1, mod.shape[-1]), lambda i: (i // blocks_per_mod, 0, 0)),
                pl.BlockSpec((1, d), const)]
    in_specs += [pl.BlockSpec(w.shape, const) for w in ws]
    in_specs += [pl.BlockSpec(b.shape, const) for b in bs]
    return pl.pallas_call(
        functools.partial(_in_kernel, d=d, n_seg=n_seg),
        grid=(r // tm,),
        in_specs=in_specs,
        out_specs=[pl.BlockSpec((tm, w.shape[1]), lambda i: (i, 0)) for w in ws],
        out_shape=[jax.ShapeDtypeStruct((r, w.shape[1]), F32) for w in ws],
        compiler_params=_cparams(1),
        name="in_proj",
    )(x2d, mod, nw, *ws, *bs)


def _load_tile(x_ref):
    if len(x_ref.shape) == 2:
        return x_ref[...]
    return jnp.concatenate([x_ref[:, c, :] for c in range(x_ref.shape[1])], axis=0)


def _tile_specs(tm, d, col_view):
    if col_view is None:
        hb = tm // 8
        return (pl.BlockSpec((tm, d), lambda i: (i, 0)),
                lambda n_rows: pl.BlockSpec((8, d), lambda i: (jnp.maximum(i * hb - 1, 0), 0)),
                lambda n_rows: pl.BlockSpec((8, d), lambda i: (jnp.minimum((i + 1) * hb, n_rows // 8 - 1), 0)),
                lambda x2d: x2d)
    rows, cols = col_view
    ncol = tm // rows
    tps = cols // ncol
    hb = ncol // 8
    return (pl.BlockSpec((rows, ncol, d), lambda i: (i // tps, i % tps, 0)),
            lambda n_rows: pl.BlockSpec((rows, 8, d), lambda i: (i // tps, jnp.maximum((i % tps) * hb - 1, 0), 0)),
            lambda n_rows: pl.BlockSpec((rows, 8, d),
                                        lambda i: (i // tps, jnp.minimum((i % tps + 1) * hb, cols // 8 - 1), 0)),
            lambda x2d: x2d.reshape(-1, cols, d))


def _gmlp_chunk(x, ws_ref, bs_ref, nw_ref, nb_ref, lane_lo):
    g = 0.5 * x * (1.0 + jnp.tanh(math.sqrt(2.0 / math.pi) * (x + 0.044715 * (x * x * x))))
    u, v = g[:, 0:256], g[:, 256:512]
    mu = jnp.mean(v, axis=-1, keepdims=True)
    vc = v - mu
    var = jnp.mean(vc * vc, axis=-1, keepdims=True)
    vn = (vc * lax.rsqrt(var + NORM_EPS) * nw_ref[...] + nb_ref[...]).astype(BF16)
    mixed = []
    for pair in range(C_GROUPS // 2):
        v_p = vn[:, pair * LANE:(pair + 1) * LANE]
        mixed.append(jnp.where(lane_lo, jnp.dot(ws_ref[2 * pair], v_p, preferred_element_type=F32),
                               jnp.dot(ws_ref[2 * pair + 1], v_p, preferred_element_type=F32)))
    return u * (jnp.concatenate(mixed, axis=1) + bs_ref[...])


def _in_full_kernel(x_ref, xp_ref, xn_ref, mod_ref, nw_ref, wa_ref, wb_ref, wc_ref, wd_ref,
                    ba_ref, bb_ref, bc_ref, bd_ref, sw_ref, sb_ref, gws_ref, gbs_ref, gnw_ref, gnb_ref,
                    oa_ref, ox0_ref, ou_ref, oc_ref, od_ref, *, d, seq_len):
    shift, scale, nw = mod_ref[0, :, 0:d], mod_ref[0, :, d:2 * d], nw_ref[...]
    x = _load_tile(x_ref)
    tm = x.shape[0]
    h = _norm_mod(x, nw, shift, scale).astype(BF16)
    pc = jnp.dot(h, wc_ref[...], preferred_element_type=F32) + bc_ref[...]
    pb = jnp.dot(h, wb_ref[...], preferred_element_type=F32) + bb_ref[...]
    oa_ref[...] = jnp.dot(h, wa_ref[...], preferred_element_type=F32) + ba_ref[...]
    od_ref[...] = jnp.dot(h, wd_ref[...], preferred_element_type=F32) + bd_ref[...]
    lane_lo = lax.broadcasted_iota(jnp.int32, (C_CHUNK, LANE), 1) < 256 // C_GROUPS
    for c in range(tm // C_CHUNK):
        oc_ref[c * C_CHUNK:(c + 1) * C_CHUNK, :] = _gmlp_chunk(pc[c * C_CHUNK:(c + 1) * C_CHUNK], gws_ref, gbs_ref,
                                                                gnw_ref, gnb_ref, lane_lo)
    if len(xp_ref.shape) == 2:
        halo = jnp.concatenate([xp_ref[...], xn_ref[...]], axis=0)
    else:
        halo = jnp.concatenate([xp_ref[xp_ref.shape[0] - 1], xn_ref[0]], axis=0)
    pb_halo = jnp.dot(_norm_mod(halo, nw, shift, scale).astype(BF16), wb_ref[...],
                      preferred_element_type=F32) + bb_ref[...]
    row = lax.broadcasted_iota(jnp.int32, pb.shape, 0)
    pos = (pl.program_id(0) * tm + row) & (seq_len - 1)
    prev = jnp.where(row == 0, pb_halo[7:8, :], pltpu.roll(pb, 1, axis=0))
    nxt = jnp.where(row == tm - 1, pb_halo[8:9, :], pltpu.roll(pb, tm - 1, axis=0))
    prev = jnp.where(pos == 0, 0.0, prev)
    nxt = jnp.where(pos == seq_len - 1, 0.0, nxt)
    y = prev * sw_ref[0:1, :] + pb * sw_ref[1:2, :] + nxt * sw_ref[2:3, :] + sb_ref[...]
    ox0_ref[...] = y[:, 0:256]
    ou_ref[...] = y[:, 256:512] * y[:, 512:768]


def _in_full_call(x2d, mod, nw, ws, bs, conv_w, conv_b, gws, gbs, gnw, gnb, rows_per_mod, seq_len, tm,
                  col_view=None):
    r, d = x2d.shape
    assert seq_len & (seq_len - 1) == 0 and tm % C_CHUNK == 0
    blocks_per_mod = rows_per_mod // tm
    c2 = lambda i: (0, 0)
    full = lambda a: pl.BlockSpec(a.shape, (lambda i: (0,) * a.ndim))
    row = lambda w: pl.BlockSpec((tm, w), lambda i: (i, 0))
    x_spec, prev_spec, next_spec, view = _tile_specs(tm, d, col_view)
    in_specs = [x_spec, prev_spec(r), next_spec(r),
                pl.BlockSpec((1, 1, mod.shape[-1]), lambda i: (i // blocks_per_mod, 0, 0)),
                pl.BlockSpec((1, d), c2)]
    consts = list(ws) + list(bs) + [conv_w, conv_b, gws, gbs, gnw, gnb]
    in_specs += [full(a) for a in consts]
    widths = (ws[0].shape[1], 256, 256, 256, ws[3].shape[1])
    return pl.pallas_call(
        functools.partial(_in_full_kernel, d=d, seq_len=seq_len),
        grid=(r // tm,),
        in_specs=in_specs,
        out_specs=[row(w) for w in widths],
        out_shape=[jax.ShapeDtypeStruct((r, w), F32) for w in widths],
        compiler_params=_cparams(1),
        name="in_proj_full",
    )(view(x2d), view(x2d), view(x2d), mod, nw, *consts)


def _mlstm_ones(L):
    ones_blk = np.zeros((2 * L, LANE), np.float32)
    ones_blk[:L, :HEAD_DIM] = 1.0
    ones_blk[L:, HEAD_DIM:] = 1.0
    return ones_blk


def _running(op, fill, x, row, reverse):
    L = x.shape[0]
    sh = 1
    while sh < L:
        if reverse:
            shifted, valid = pltpu.roll(x, L - sh, axis=0), row < L - sh
        else:
            shifted, valid = pltpu.roll(x, sh, axis=0), row >= sh
        x = op(x, jnp.where(valid, shifted, fill))
        sh *= 2
    return x


def _mlstm_chunk(blk, ones_ref, cn_ref, m_ref, d, mask, row, lane_lo, sub_lo, bd):
    L = blk.shape[0]
    reverse = d == 1
    end = 0 if reverse else L - 1
    cum = _running(jnp.add, 0.0, _log_sigmoid(blk[:, 1152:1280]), row, reverse)
    a = blk[:, 1024:1152] - cum
    m_prev = m_ref[d:d + 1, :]
    g = jnp.maximum(_running(jnp.maximum, NEG_BIG, a, row, reverse), m_prev)
    g_end = g[end:end + 1, :]
    w_inter = jnp.exp(m_prev - g)
    e_negm = jnp.exp(-(cum + g))
    m_ref[d:d + 1, :] = cum[end:end + 1, :] + g_end
    a_t = a.T

    def head_lanes(x, pair):
        c = 4 * d + 2 * pair
        return jnp.where(lane_lo, jnp.broadcast_to(x[:, c:c + 1], (L, LANE)),
                         jnp.broadcast_to(x[:, c + 1:c + 2], (L, LANE)))

    outs = []
    for pair in range(HEADS // 2):
        cs = slice(128 * pair, 128 * pair + 128)
        q_pair = blk[:, cs].astype(BF16)
        k_pair = blk[:, 256 + 128 * pair:256 + 128 * pair + 128] * (HEAD_DIM ** -0.5)
        v_pair = blk[:, 512 + 128 * pair:512 + 128 * pair + 128]
        k_b = k_pair.astype(BF16)
        zero = jnp.zeros_like(k_b)
        k_rows = jnp.concatenate([jnp.where(lane_lo, k_b, zero), jnp.where(lane_lo, zero, k_b)], axis=0)
        s_pair = lax.dot_general(q_pair, k_rows, (((1,), (1,)), ((), ())), preferred_element_type=F32)
        w_halves = []
        for sub in range(2):
            hd = 2 * pair + sub
            c = 4 * d + hd
            expo = jnp.where(mask, a_t[c:c + 1, :] - jnp.broadcast_to(g[:, c:c + 1], (L, L)), NEG_BIG)
            w_halves.append((jnp.exp(expo) * s_pair[:, sub * L:(sub + 1) * L]).astype(BF16))
        w_pair = jnp.concatenate(w_halves, axis=1)
        v_b = v_pair.astype(BF16)
        v_rows = jnp.concatenate([jnp.where(lane_lo, v_b, zero), jnp.where(lane_lo, zero, v_b)], axis=0)
        intra = jnp.dot(w_pair, jnp.concatenate([v_rows, ones_ref[...]], axis=1), preferred_element_type=F32)
        slot = 2 * d + pair
        cn = cn_ref[slot]
        inter = jnp.dot(q_pair, cn.astype(BF16), preferred_element_type=F32)
        wi_p = head_lanes(w_inter, pair)
        num = wi_p * inter[:, 0:128] + intra[:, 0:128]
        den = wi_p * inter[:, 128:256] + intra[:, 128:256]
        outs.append(num / jnp.maximum(jnp.abs(den), head_lanes(e_negm, pair)))
        c_e = 4 * d + 2 * pair
        w_st = jnp.exp(jnp.where(sub_lo, a_t[c_e:c_e + 1, :] - g_end[:, c_e:c_e + 1],
                                 a_t[c_e + 1:c_e + 2, :] - g_end[:, c_e + 1:c_e + 2]))
        ktw = (k_pair.T * w_st).astype(BF16)
        upd = jnp.dot(ktw, jnp.concatenate([v_b, jnp.ones_like(v_b)], axis=1), preferred_element_type=F32)
        decay = wi_p[end:end + 1, :]
        cn_ref[slot] = jnp.concatenate([decay, decay], axis=1) * cn + jnp.where(bd, upd, 0.0)
    return jnp.concatenate(outs, axis=1)


def _mlstm_kernel(*refs, need_ctx):
    ctx_ref, lat_ref, ones_ref = refs[:3]
    if need_ctx:
        ycf_ref, ycb_ref, ylf_ref, ylb_ref, cn_ref, m_ref = refs[3:]
    else:
        ylf_ref, ylb_ref, cn_ref, m_ref = refs[3:]
        ycf_ref = ycb_ref = None
    L = A_CHUNK
    lane_lo = lax.broadcasted_iota(jnp.int32, (L, LANE), 1) < HEAD_DIM
    row = lax.broadcasted_iota(jnp.int32, (L, LANE), 0)
    sub_lo = row < HEAD_DIM
    r2 = lax.broadcasted_iota(jnp.int32, (L, L), 0)
    c2 = lax.broadcasted_iota(jnp.int32, (L, L), 1)
    mask_f, mask_b = r2 >= c2, r2 <= c2
    rb = lax.broadcasted_iota(jnp.int32, (LANE, 2 * LANE), 0) < HEAD_DIM
    cb = (lax.broadcasted_iota(jnp.int32, (LANE, 2 * LANE), 1) % LANE) < HEAD_DIM
    bd = rb == cb
    cn_ref[...] = jnp.zeros_like(cn_ref)
    m_ref[...] = jnp.zeros_like(m_ref)

    def scan(src_ref, yf_ref, yb_ref):
        n_chunks = src_ref.shape[1] // L

        def body(j, carry):
            sf = pl.multiple_of(j * L, L)
            sb = pl.multiple_of((n_chunks - 1 - j) * L, L)
            hf = _mlstm_chunk(src_ref[0, pl.ds(sf, L), :], ones_ref, cn_ref, m_ref, 0, mask_f, row, lane_lo, sub_lo, bd)
            hb = _mlstm_chunk(src_ref[0, pl.ds(sb, L), :], ones_ref, cn_ref, m_ref, 1, mask_b, row, lane_lo, sub_lo, bd)
            if yf_ref is not None:
                yf_ref[0, pl.ds(sf, L), :] = hf
                yb_ref[0, pl.ds(sb, L), :] = hb
            return carry

        lax.fori_loop(0, n_chunks, body, 0)

    scan(ctx_ref, ycf_ref, ycb_ref)
    scan(lat_ref, ylf_ref, ylb_ref)


def _mlstm_call(pa_ctx, pa_lat, need_ctx):
    b, nc, wa = pa_ctx.shape
    n = pa_lat.shape[1]
    L = A_CHUNK
    ones_blk = jnp.asarray(_mlstm_ones(L), BF16)
    c2 = lambda i: (0, 0)
    out_specs = [pl.BlockSpec((1, n, 256), lambda i: (i, 0, 0))] * 2
    out_shape = [jax.ShapeDtypeStruct((b, n, 256), F32)] * 2
    if need_ctx:
        out_specs = [pl.BlockSpec((1, nc, 256), lambda i: (i, 0, 0))] * 2 + out_specs
        out_shape = [jax.ShapeDtypeStruct((b, nc, 256), F32)] * 2 + out_shape
    res = pl.pallas_call(
        functools.partial(_mlstm_kernel, need_ctx=need_ctx),
        grid=(b,),
        in_specs=[pl.BlockSpec((1, nc, wa), lambda i: (i, 0, 0)),
                  pl.BlockSpec((1, n, wa), lambda i: (i, 0, 0)),
                  pl.BlockSpec(ones_blk.shape, c2)],
        out_specs=out_specs,
        out_shape=out_shape,
        scratch_shapes=[pltpu.VMEM((HEADS, LANE, 2 * LANE), F32), pltpu.VMEM((8, LANE), F32)],
        compiler_params=_cparams(1),
        name="mlstm",
    )(pa_ctx, pa_lat, ones_blk)
    return tuple(res) if need_ctx else (None, None) + tuple(res)


D_FAST_BASE = 16
D_FAST_MAX_EXPONENT = 60.0


def _hgrn_tables(L, base):
    t = np.arange(L)
    lev0 = int(math.log2(base))
    n_split = int(math.log2(L)) - lev0
    same_base = (t[:, None] // base) == (t[None, :] // base)
    lvl_f = np.where(same_base & (t[:, None] >= t[None, :]), 0, -1)
    lvl_b = np.where(same_base & (t[:, None] <= t[None, :]), 0, -1)
    mids_f, mids_b = [], []
    for i in range(n_split):
        half, size = base << i, base << (i + 1)
        same = (t[:, None] // size) == (t[None, :] // size)
        upper = (t // half) % 2 == 1
        lvl_f = np.where(same & upper[:, None] & ~upper[None, :], i + 1, lvl_f)
        lvl_b = np.where(same & ~upper[:, None] & upper[None, :], i + 1, lvl_b)
        starts = np.arange(0, L, size)
        mids_f.append([(int(s), size, int(s) + half - 1) for s in starts])
        mids_b.append([(int(s), size, int(s) + half) for s in starts])
    wide = lambda m: np.concatenate([m, m], axis=-1).astype(np.int32)
    return wide(lvl_f), wide(lvl_b), mids_f, mids_b


def _rows_of(b, spec):
    return jnp.concatenate([jnp.broadcast_to(b[r:r + 1, :], (n, b.shape[1])) for _, n, r in spec], axis=0)


def _hgrn_chunk(blk, f_pre, lb, lvl_ref, mids, s_ref, x_ref, d, row, lane_lo, bd, base):
    L = blk.shape[0]
    reverse = d == 1
    end = 0 if reverse else L - 1
    qs = _silu(blk[:, 0:256])
    v = blk[:, 256:512]
    sig, sig_neg = _sigmoid_both(f_pre)
    log_f = jnp.log(jnp.maximum(lb + (1.0 - lb) * sig, F_EPS))
    k = (1.0 - lb) * sig_neg
    b = _running(jnp.add, 0.0, log_f, row, reverse)
    b_end = b[end:end + 1, :]
    q_in = qs * jnp.exp(b)
    k_out = k * jnp.exp(b_end - b)
    decay = jnp.exp(b_end)
    factors = []
    if base > 1:
        zero_row = jnp.zeros((1, b.shape[1]), F32)
        if reverse:
            refs = [b[s + base:s + base + 1, :] if s + base < L else zero_row for s in range(0, L, base)]
        else:
            refs = [b[s - 1:s, :] if s > 0 else zero_row for s in range(0, L, base)]
        r0 = jnp.concatenate([jnp.broadcast_to(r, (base, b.shape[1])) for r in refs], axis=0)
        factors.append((jnp.exp(b - r0), jnp.exp(r0 - b)))
        last = [b[s:s + 1, :] if reverse else b[s + base - 1:s + base, :] for s in range(0, L, base)]
        worst = functools.reduce(jnp.maximum, [r - e for r, e in zip(refs, last)])
        x_ref[d:d + 1, :] = jnp.maximum(x_ref[d:d + 1, :], worst)
    else:
        factors.append((None, None))
    for spec in mids:
        e = jnp.exp(-jnp.abs(b - _rows_of(b, spec)))
        factors.append((e, e))
    nt = (((1,), (1,)), ((), ()))
    lvl = lvl_ref[...]
    outs = []
    for pair in range(HEADS // 2):
        cs = slice(128 * pair, 128 * pair + 128)
        q_p, k_p = qs[:, cs], k[:, cs]
        zero = jnp.zeros((L, LANE), BF16)
        v_b = v[:, cs].astype(BF16)
        v_rows = jnp.concatenate([jnp.where(lane_lo, v_b, zero), jnp.where(lane_lo, zero, v_b)], axis=0)
        scores = jnp.zeros((L, 2 * L), F32)
        for i, (eq, ek) in enumerate(factors):
            q_l = (q_p if eq is None else q_p * eq[:, cs]).astype(BF16)
            k_l = (k_p if ek is None else k_p * ek[:, cs]).astype(BF16)
            k_rows = jnp.concatenate([jnp.where(lane_lo, k_l, zero), jnp.where(lane_lo, zero, k_l)], axis=0)
            scores = jnp.where(lvl == i, lax.dot_general(q_l, k_rows, nt, preferred_element_type=F32), scores)
        slot = 2 * d + pair
        s_t = s_ref[slot]
        outs.append(jnp.dot(scores.astype(BF16), v_rows, preferred_element_type=F32)
                    + lax.dot_general(q_in[:, cs].astype(BF16), s_t.astype(BF16), nt, preferred_element_type=F32))
        upd = lax.dot_general(v_b, k_out[:, cs].astype(BF16), (((0,), (0,)), ((), ())), preferred_element_type=F32)
        s_ref[slot] = decay[:, cs] * s_t + jnp.where(bd, upd, 0.0)
    return jnp.concatenate(outs, axis=1)


def _hgrn_kernel(*refs, need_ctx, layer, mids_fast, mids_safe):
    ctx_ref, lat_ref, lvlf_fast_ref, lvlb_fast_ref, lvlf_safe_ref, lvlb_safe_ref, lbl_ref = refs[:7]
    if need_ctx:
        ycf_ref, ycb_ref, ylf_ref, ylb_ref, s_ref, x_ref = refs[7:]
    else:
        ylf_ref, ylb_ref, s_ref, x_ref = refs[7:]
        ycf_ref = ycb_ref = None
    L = D_CHUNK
    lane_lo = lax.broadcasted_iota(jnp.int32, (L, LANE), 1) < HEAD_DIM
    row = lax.broadcasted_iota(jnp.int32, (L, 2 * LANE), 0)
    bd = ((lax.broadcasted_iota(jnp.int32, (LANE, LANE), 0) < HEAD_DIM)
          == (lax.broadcasted_iota(jnp.int32, (LANE, LANE), 1) < HEAD_DIM))
    logits = lbl_ref[...]
    e = jnp.exp(logits - jnp.max(logits, axis=0, keepdims=True))
    prob = e / jnp.sum(e, axis=0, keepdims=True)
    lb = jnp.sum(prob[0:layer + 1], axis=0, keepdims=True) - prob[0:1]
    def scan(src_ref, yf_ref, yb_ref, lvlf_ref, lvlb_ref, mids, base):
        n_chunks = src_ref.shape[1] // L

        def body(j, carry):
            sf = pl.multiple_of(j * L, L)
            sb = pl.multiple_of((n_chunks - 1 - j) * L, L)
            blk_f = src_ref[0, pl.ds(sf, L), :]
            blk_b = src_ref[0, pl.ds(sb, L), :]
            of = _hgrn_chunk(blk_f, blk_f[:, 512:768], lb, lvlf_ref, mids[0], s_ref, x_ref, 0, row, lane_lo, bd, base)
            ob = _hgrn_chunk(blk_b, blk_b[:, 768:1024], lb, lvlb_ref, mids[1], s_ref, x_ref, 1, row, lane_lo, bd, base)
            if yf_ref is not None:
                yf_ref[0, pl.ds(sf, L), :] = of
                yb_ref[0, pl.ds(sb, L), :] = ob
            return carry

        lax.fori_loop(0, n_chunks, body, 0)

    def scans(lvlf_ref, lvlb_ref, mids, base):
        s_ref[...] = jnp.zeros_like(s_ref)
        scan(ctx_ref, ycf_ref, ycb_ref, lvlf_ref, lvlb_ref, mids, base)
        scan(lat_ref, ylf_ref, ylb_ref, lvlf_ref, lvlb_ref, mids, base)

    x_ref[...] = jnp.zeros_like(x_ref)
    scans(lvlf_fast_ref, lvlb_fast_ref, mids_fast, D_FAST_BASE)

    @pl.when(jnp.logical_not(jnp.max(x_ref[...]) <= D_FAST_MAX_EXPONENT))
    def _():
        scans(lvlf_safe_ref, lvlb_safe_ref, mids_safe, 1)


def _hgrn_call(pd_ctx, pd_lat, lb_logits, layer, need_ctx):
    b, nc, wd = pd_ctx.shape
    n = pd_lat.shape[1]
    L = D_CHUNK
    lvlf_fast, lvlb_fast, midsf_fast, midsb_fast = _hgrn_tables(L, D_FAST_BASE)
    lvlf_safe, lvlb_safe, midsf_safe, midsb_safe = _hgrn_tables(L, 1)
    tables = [jnp.asarray(a) for a in (lvlf_fast, lvlb_fast, lvlf_safe, lvlb_safe)]
    c2 = lambda i: (0, 0)
    out_specs = [pl.BlockSpec((1, n, 256), lambda i: (i, 0, 0))] * 2
    out_shape = [jax.ShapeDtypeStruct((b, n, 256), F32)] * 2
    if need_ctx:
        out_specs = [pl.BlockSpec((1, nc, 256), lambda i: (i, 0, 0))] * 2 + out_specs
        out_shape = [jax.ShapeDtypeStruct((b, nc, 256), F32)] * 2 + out_shape
    res = pl.pallas_call(
        functools.partial(_hgrn_kernel, need_ctx=need_ctx, layer=layer,
                          mids_fast=(midsf_fast, midsb_fast), mids_safe=(midsf_safe, midsb_safe)),
        grid=(b,),
        in_specs=[pl.BlockSpec((1, nc, wd), lambda i: (i, 0, 0)),
                  pl.BlockSpec((1, n, wd), lambda i: (i, 0, 0))]
                 + [pl.BlockSpec(a.shape, c2) for a in tables]
                 + [pl.BlockSpec(lb_logits.shape, c2)],
        out_specs=out_specs,
        out_shape=out_shape,
        scratch_shapes=[pltpu.VMEM((HEADS, LANE, LANE), F32), pltpu.VMEM((8, 256), F32)],
        compiler_params=_cparams(1),
        name="hgrn2",
    )(pd_ctx, pd_lat, *tables, lb_logits)
    return tuple(res) if need_ctx else (None, None) + tuple(res)


def _filter_feats(n):
    t = np.linspace(0.0, 1.0, n, dtype=np.float32)[:, None]
    bands = np.linspace(1e-4, B_BANDS - 1, B_BANDS, dtype=np.float32)[None]
    ang = (np.float32(2 * math.pi) * bands * np.arange(n, dtype=np.float32)[:, None] / np.float32(n)).astype(np.float32)
    z = np.concatenate([t, np.cos(ang), -np.sin(ang)], axis=-1).astype(np.float32)
    deltas = np.abs(np.linspace(math.log(B_DECAY_TARGET) / B_SLOW_DECAY,
                                math.log(B_DECAY_TARGET) / B_FAST_DECAY, 256, dtype=np.float32))
    neg_t_deltas = (-t * deltas[None]).astype(np.float32)
    return z, neg_t_deltas


def _filter_kernel(z_ref, ntd_ref, w1_ref, b1_ref, f1_ref, w2_ref, b2_ref, f2_ref, w3_ref, flip_ref, o_ref):
    n = z_ref.shape[0]
    T = flip_ref.shape[0]
    hd = jnp.sin(f1_ref[...] * (_dot_f32(z_ref[...], w1_ref[...]) + b1_ref[...]))
    hd = jnp.sin(f2_ref[...] * (_dot_f32(hd, w2_ref[...]) + b2_ref[...]))
    decay = jnp.exp(ntd_ref[...])
    fwd = _dot_f32(hd, w3_ref[:, 0:256]) * decay
    bwd = _dot_f32(hd, w3_ref[:, 256:512]) * decay
    flip = flip_ref[...]
    rev = []
    for i in range(n // T):
        parts = _split3(bwd[n - (i + 1) * T:n - i * T])
        rev.append(sum(jnp.dot(flip, p, preferred_element_type=F32) for p in parts))
    rev = jnp.concatenate(rev, axis=0)
    row = lax.broadcasted_iota(jnp.int32, rev.shape, 0)
    o_ref[0:n, :] = jnp.where(row == 0, 0.0, pltpu.roll(rev, 1, axis=0))
    o_ref[n:2 * n, :] = fwd


def _filter_call(n, w1, b1, f1, w2, b2, f2, w3):
    z, ntd = _filter_feats(n)
    k_pad = 32
    z = np.pad(z, ((0, 0), (0, k_pad - z.shape[1])))
    w1 = jnp.pad(w1, ((0, k_pad - w1.shape[0]), (0, 0)))
    flip = jnp.asarray(np.eye(CONV_BLOCK, dtype=np.float32)[::-1], BF16)
    args = (jnp.asarray(z), jnp.asarray(ntd), w1, b1, f1, w2, b2, f2, w3, flip)
    return pl.pallas_call(
        _filter_kernel,
        grid=(1,),
        in_specs=[pl.BlockSpec(a.shape, lambda i: (0, 0)) for a in args],
        out_specs=pl.BlockSpec((2 * n, 256), lambda i: (0, 0)),
        out_shape=jax.ShapeDtypeStruct((2 * n, 256), F32),
        compiler_params=_cparams(1),
        name="hyena_filter",
    )(*args)


def _long_conv_kernel(u_ref, k_ref, o_ref, *, n_blk, bsz, ch_per_step):
    T = CONV_BLOCK
    half = T // 2
    for c in range(ch_per_step):
        taps = k_ref[c]
        rolled = pltpu.roll(jnp.broadcast_to(taps, (half, taps.shape[1])), 0, axis=1, stride=1, stride_axis=0)
        acc = [None] * n_blk
        for d in range(-(n_blk - 1), n_blk):
            base = (d + n_blk) * T
            tile = jnp.concatenate([rolled[:, base:base + T], rolled[:, base - half:base - half + T]],
                                   axis=0).astype(BF16)
            j_lo, j_hi = max(0, -d), min(n_blk, n_blk - d)
            lhs = u_ref[c, j_lo * bsz:j_hi * bsz, :].astype(BF16)
            res = jnp.dot(lhs, tile, preferred_element_type=F32)
            for j in range(j_lo, j_hi):
                part = res[(j - j_lo) * bsz:(j - j_lo + 1) * bsz]
                i = j + d
                acc[i] = part if acc[i] is None else acc[i] + part
        o_ref[c] = jnp.concatenate(acc, axis=0)


def _long_conv_call(u_t, k_t, n_blk, bsz, ch_per_step):
    ch, rows, T = u_t.shape
    return pl.pallas_call(
        functools.partial(_long_conv_kernel, n_blk=n_blk, bsz=bsz, ch_per_step=ch_per_step),
        grid=(ch // ch_per_step,),
        in_specs=[pl.BlockSpec((ch_per_step, rows, T), lambda i: (i, 0, 0)),
                  pl.BlockSpec((ch_per_step, 1, k_t.shape[-1]), lambda i: (i, 0, 0))],
        out_specs=pl.BlockSpec((ch_per_step, rows, T), lambda i: (i, 0, 0)),
        out_shape=jax.ShapeDtypeStruct((ch, rows, T), F32),
        compiler_params=_cparams(1),
        name="hyena_long_conv",
    )(u_t, k_t)


def _hyena_conv(u, filt):
    bsz, n, ch = u.shape
    T = CONV_BLOCK
    n_blk = n // T
    k_t = filt.T.reshape(ch, 1, 2 * n)
    u_t = u.reshape(bsz, n_blk, T, ch).transpose(3, 1, 0, 2).reshape(ch, n_blk * bsz, T)
    y_t = _long_conv_call(u_t, k_t, n_blk, bsz, 4)
    return y_t.reshape(ch, n_blk, bsz, T).transpose(2, 1, 3, 0).reshape(bsz, n, ch)


def _head_rms(y, lane_lo):
    parts = []
    for pair in range(HEADS // 2):
        yp = y[:, pair * LANE:(pair + 1) * LANE]
        y2 = yp * yp
        s_lo = jnp.sum(jnp.where(lane_lo, y2, 0.0), axis=-1, keepdims=True)
        s_all = jnp.sum(y2, axis=-1, keepdims=True)
        ms = jnp.where(lane_lo, s_lo, s_all - s_lo) * (1.0 / HEAD_DIM)
        parts.append(yp * lax.rsqrt(ms + NORM_EPS))
    return jnp.concatenate(parts, axis=1)


def _merge_kernel(x_ref, mod_ref, nw_ref, yaf_ref, yab_ref, og_ref, x0_ref, u_ref, yconv_ref, yc_ref,
                  ydf_ref, ydb_ref, gg_ref, nwa_ref, nwd_ref, hyb_ref, wg_ref, wb_ref, wo_ref, o_ref, *, d):
    x = x_ref[...]
    h = _norm_mod(x, nw_ref[...], mod_ref[0, :, 0:d], mod_ref[0, :, d:2 * d]).astype(BF16)
    lane_lo = lax.broadcasted_iota(jnp.int32, (x.shape[0], LANE), 1) < HEAD_DIM
    y_a = _head_rms(_load_tile(yaf_ref) + _load_tile(yab_ref), lane_lo) * nwa_ref[...] * _sigmoid(_load_tile(og_ref))
    y_d = _head_rms(_load_tile(ydf_ref) + _load_tile(ydb_ref), lane_lo) * nwd_ref[...] * _silu(_load_tile(gg_ref))
    y_b = _load_tile(x0_ref) * (_load_tile(yconv_ref) + _load_tile(u_ref) * hyb_ref[...])
    ys = [v.astype(BF16) for v in (y_a, y_b, _load_tile(yc_ref), y_d)]
    y = None
    n_part = d // MERGE_PARTS
    for part_i in range(MERGE_PARTS):
        cs = slice(part_i * n_part, (part_i + 1) * n_part)
        acc = None
        for j in range(N_BRANCH):
            gate = _sigmoid(jnp.dot(h, wg_ref[j, :, cs], preferred_element_type=F32))
            term = gate * jnp.dot(ys[j], wb_ref[j, :, cs], preferred_element_type=F32)
            acc = term if acc is None else acc + term
        part = jnp.dot(acc.astype(BF16), wo_ref[cs, :], preferred_element_type=F32)
        y = part if y is None else y + part
    o_ref[...] = x + mod_ref[0, :, 2 * d:3 * d] * y


def _merge_call(x2d, mod, nw, mix_a, mix_d, x0, u, yconv, yc, nwa, nwd, hyb, wg, wb, wo, rows_per_mod, tm,
                col_view=None):
    r, d = x2d.shape
    blocks_per_mod = rows_per_mod // tm
    c2 = lambda i: (0, 0)
    c3 = lambda i: (0, 0, 0)
    row = lambda w: pl.BlockSpec((tm, w), lambda i: (i, 0))
    if col_view is None:
        y_spec = lambda cb=0: pl.BlockSpec((tm, 256), lambda i: (i, cb))
        view = lambda y: y.reshape(r, -1)
    else:
        rows, cols = col_view
        nrow = tm // cols
        tps = rows // nrow
        y_spec = lambda cb=0: pl.BlockSpec((cols, nrow, 256), lambda i: (i // tps, i % tps, cb))
        view = lambda y: y.reshape(-1, rows, y.shape[-1])
    yaf, yab, pa = mix_a
    ydf, ydb, pd = mix_d
    return pl.pallas_call(
        functools.partial(_merge_kernel, d=d),
        grid=(r // tm,),
        in_specs=[row(d), pl.BlockSpec((1, 1, mod.shape[-1]), lambda i: (i // blocks_per_mod, 0, 0)),
                  pl.BlockSpec((1, d), c2), y_spec(), y_spec(), y_spec(3), y_spec(), y_spec(), y_spec(), y_spec(),
                  y_spec(), y_spec(), y_spec(4),
                  pl.BlockSpec((1, 256), c2), pl.BlockSpec((1, 256), c2), pl.BlockSpec((1, 256), c2),
                  pl.BlockSpec(wg.shape, c3), pl.BlockSpec(wb.shape, c3), pl.BlockSpec(wo.shape, c2)],
        out_specs=row(d),
        out_shape=jax.ShapeDtypeStruct((r, d), F32),
        compiler_params=_cparams(1),
        name="merge",
    )(x2d, mod, nw, view(yaf), view(yab), view(pa), view(x0), view(u), view(yconv), view(yc),
      view(ydf), view(ydb), view(pd), nwa, nwd, hyb, wg, wb, wo)


def _ffn_kernel(x_ref, mod_ref, nw_ref, wu_ref, wd_ref, fnw_ref, o_ref, *, d, hid_chunk, final_norm):
    x = x_ref[...]
    h = _norm_mod(x, nw_ref[...], mod_ref[0, :, 3 * d:4 * d], mod_ref[0, :, 4 * d:5 * d]).astype(BF16)
    hidden = wd_ref.shape[0]
    acc = None
    for c in range(hidden // hid_chunk):
        cs = slice(c * hid_chunk, (c + 1) * hid_chunk)
        a = jnp.dot(h, wu_ref[:, cs], preferred_element_type=F32)
        g = jnp.dot(h, wu_ref[:, hidden + c * hid_chunk:hidden + (c + 1) * hid_chunk], preferred_element_type=F32)
        term = jnp.dot((_silu(a) * g).astype(BF16), wd_ref[cs, :], preferred_element_type=F32)
        acc = term if acc is None else acc + term
    y = x + mod_ref[0, :, 5 * d:6 * d] * acc
    if final_norm:
        y = y * lax.rsqrt(jnp.mean(y * y, axis=-1, keepdims=True) + NORM_EPS) * fnw_ref[...]
    o_ref[...] = y


def _ffn_call(x2d, mod, nw, wu, wd, fnw, rows_per_mod, tm, final_norm):
    r, d = x2d.shape
    blocks_per_mod = rows_per_mod // tm
    c2 = lambda i: (0, 0)
    return pl.pallas_call(
        functools.partial(_ffn_kernel, d=d, hid_chunk=256, final_norm=final_norm),
        grid=(r // tm,),
        in_specs=[pl.BlockSpec((tm, d), lambda i: (i, 0)),
                  pl.BlockSpec((1, 1, mod.shape[-1]), lambda i: (i // blocks_per_mod, 0, 0)),
                  pl.BlockSpec((1, d), c2),
                  pl.BlockSpec(wu.shape, c2), pl.BlockSpec(wd.shape, c2),
                  pl.BlockSpec((1, d), c2)],
        out_specs=pl.BlockSpec((tm, d), lambda i: (i, 0)),
        out_shape=jax.ShapeDtypeStruct((r, d), F32),
        compiler_params=_cparams(1),
        name="ffn",
    )(x2d, mod, nw, wu, wd, fnw)


def _row_tile(rows):
    return 512 if rows % 512 == 0 else 256


def kernel(x, c, ctx, c_ctx, ada_w, ada_b, norm1_w, norm2_w, w_in, b_in, mlstm_norm_w, hy_short_w, hy_short_b, hy_w1, hy_b1, hy_freq1, hy_w2, hy_b2, hy_freq2, hy_w3, hy_bias, gm_norm_w, gm_norm_b, gm_ws, gm_bs, hg_lb_logits, hg_norm_w, w_gate, w_branch, w_out, w_ffn_in, w_ffn_out, final_norm_w):
    bsz, n, d = x.shape
    nc = ctx.shape[1]
    depth = ada_w.shape[0]
    rows = n // GRID_W
    mw = d // N_BRANCH
    assert mw == 256 and mw == HEADS * HEAD_DIM

    mb = -(-(bsz + 1) // 8) * 8
    cvec = jnp.concatenate([c, c_ctx[None], jnp.zeros((mb - bsz - 1, d), F32)], axis=0)
    mods = _ada_call(cvec, ada_w, ada_b)

    s_a, s_g, s_b, s_c = 4 * mw, 4 * mw + 4 * HEADS, 7 * mw + 4 * HEADS, 9 * mw + 4 * HEADS
    gate_pad = LANE - 2 * HEADS

    xc = ctx
    for l in range(depth):
        need_ctx = l < depth - 1
        col_major = l % 2 == 1
        col_view = (rows, GRID_W) if col_major else None
        mod_lat = mods[l, :bsz].reshape(bsz, 1, 6 * d)
        mod_ctx = mods[l, bsz:bsz + 1].reshape(1, 1, 6 * d)
        wl, bl = w_in[l].astype(BF16), b_in[l]
        wgt, bgt = wl[:, s_a:s_g].reshape(d, 4, HEADS), bl[s_a:s_g].reshape(4, HEADS)
        zw, zb = jnp.zeros((d, gate_pad), BF16), jnp.zeros((gate_pad,), F32)
        w_segs = [jnp.concatenate([wl[:, :s_a], wgt[:, 0], wgt[:, 2], zw, wgt[:, 1], wgt[:, 3], zw], axis=1),
                  wl[:, s_g:s_b], wl[:, s_b:s_c], wl[:, s_c:]]
        b_segs = [jnp.concatenate([bl[:s_a], bgt[0], bgt[2], zb, bgt[1], bgt[3], zb]),
                  bl[s_g:s_b], bl[s_b:s_c], bl[s_c:]]
        b_segs = [b.reshape(1, -1) for b in b_segs]
        nw1 = norm1_w[l].reshape(1, d)

        x2d = x.reshape(bsz * n, d)
        xc2d = xc.reshape(bsz * nc, d)
        hy_args = (hy_w1[l], hy_b1[l].reshape(1, -1), hy_freq1[l].reshape(1, -1), hy_w2[l],
                   hy_b2[l].reshape(1, -1), hy_freq2[l].reshape(1, -1), hy_w3[l])
        front = (hy_short_w[l], hy_short_b[l].reshape(1, -1), gm_ws[l].astype(BF16),
                 jnp.repeat(gm_bs[l].T, mw // C_GROUPS, axis=1), gm_norm_w[l].reshape(1, mw), gm_norm_b[l].reshape(1, mw))
        pa, x0, u, ycm, pd = _in_full_call(x2d, mod_lat, nw1, w_segs, b_segs, *front, n, n, _row_tile(n), col_view)
        if need_ctx:
            ca, x0c, uc, ycm_c, cd = _in_full_call(xc2d, mod_ctx, nw1, w_segs, b_segs, *front, bsz * nc, nc,
                                                   _row_tile(bsz * nc))
        else:
            ca, cd = _in_call(xc2d, mod_ctx, nw1, [w_segs[0], w_segs[3]], [b_segs[0], b_segs[3]],
                              bsz * nc, _row_tile(bsz * nc))

        pa3, pd3 = pa.reshape(bsz, n, -1), pd.reshape(bsz, n, -1)
        ca3, cd3 = ca.reshape(bsz, nc, -1), cd.reshape(bsz, nc, -1)
        yacf, yacb, yaf, yab = _mlstm_call(ca3, pa3, need_ctx)
        ydcf, ydcb, ydf, ydb = _hgrn_call(cd3, pd3, hg_lb_logits, l, need_ctx)
        nwa, nwd = mlstm_norm_w[l].reshape(1, mw), hg_norm_w[l].reshape(1, mw)

        hyb = hy_bias[l].reshape(1, mw)
        wg, wb, wo = w_gate[l].astype(BF16), w_branch[l].astype(BF16), w_out[l].astype(BF16)
        wfu = w_ffn_in[l].astype(BF16)
        wfd = w_ffn_out[l].astype(BF16)
        nw2 = norm2_w[l].reshape(1, d)
        fnw = final_norm_w.reshape(1, d)

        yconv = _hyena_conv(u.reshape(bsz, n, mw), _filter_call(n, *hy_args))
        x2d = _merge_call(x2d, mod_lat, nw1, (yaf, yab, pa), (ydf, ydb, pd), x0, u, yconv, ycm, nwa, nwd,
                          hyb, wg, wb, wo, n, _row_tile(n), col_view)
        x2d = _ffn_call(x2d, mod_lat, nw2, wfu, wfd, fnw, n, _row_tile(n), l == depth - 1)
        x = x2d.reshape(bsz, n, d)

        if need_ctx:
            tmc = _row_tile(bsz * nc)
            yconv_c = _hyena_conv(uc.reshape(bsz, nc, mw), _filter_call(nc, *hy_args))
            xc2d = _merge_call(xc2d, mod_ctx, nw1, (yacf, yacb, ca), (ydcf, ydcb, cd), x0c, uc, yconv_c, ycm_c,
                               nwa, nwd, hyb, wg, wb, wo, bsz * nc, tmc)
            xc2d = _ffn_call(xc2d, mod_ctx, nw2, wfu, wfd, fnw, bsz * nc, tmc, False)
            xc = xc2d.reshape(bsz, nc, d)
    return x
```

```python
import functools
import math

import numpy as np
import jax
import jax.numpy as jnp
from jax import lax
from jax.experimental import pallas as pl
from jax.experimental.pallas import tpu as pltpu

F32 = jnp.float32
BF16 = jnp.bfloat16

GRID_W = 64
NORM_EPS = 1e-6
N_BRANCH = 4
NEG_BIG = -1e30
F_EPS = 1e-30
HEAD_DIM = 64
HEADS = 4
LANE = 128
A_CHUNK = 128
D_CHUNK = 128
C_CHUNK = 128
C_GROUPS = 4
B_BANDS = 8
B_DECAY_TARGET = 1e-2
B_FAST_DECAY = 0.3
B_SLOW_DECAY = 1.5
MERGE_PARTS = 2
CONV_BLOCK = 256
VMEM_LIMIT = 56 * 1024 * 1024


def _cparams(n_axes):
    return pltpu.CompilerParams(dimension_semantics=("arbitrary",) * n_axes,
                                vmem_limit_bytes=VMEM_LIMIT)


def _split3(x):
    hi = x.astype(BF16)
    r1 = x - hi.astype(F32)
    mid = r1.astype(BF16)
    lo = (r1 - mid.astype(F32)).astype(BF16)
    return hi, mid, lo


def _dot_f32(a, b):
    a_hi, a_mid, _ = _split3(a)
    b_hi, b_mid, _ = _split3(b)
    d = lambda u, v: jnp.dot(u, v, preferred_element_type=F32)
    return d(a_hi, b_hi) + (d(a_hi, b_mid) + d(a_mid, b_hi))


def _sigmoid(x):
    return 1.0 / (1.0 + jnp.exp(-x))


def _sigmoid_both(x):
    e = jnp.exp(-jnp.abs(x))
    big = 1.0 / (1.0 + e)
    small = e * big
    pos = x >= 0.0
    return jnp.where(pos, big, small), jnp.where(pos, small, big)


def _silu(x):
    return x * _sigmoid(x)


def _log_sigmoid(x):
    return jnp.minimum(x, 0.0) - jnp.log(1.0 + jnp.exp(-jnp.abs(x)))


def _norm_mod(x, nw, shift, scale):
    ms = jnp.mean(x * x, axis=-1, keepdims=True)
    y = x * lax.rsqrt(ms + NORM_EPS) * nw
    return y * (1.0 + scale) + shift


def _ada_kernel(c_ref, w_ref, b_ref, o_ref):
    o_ref[0] = _dot_f32(_silu(c_ref[...]), w_ref[0]) + b_ref[0]


def _ada_call(cvec, ada_w, ada_b):
    depth, d, n6 = ada_w.shape
    mb = cvec.shape[0]
    tn = 512
    return pl.pallas_call(
        _ada_kernel,
        grid=(depth, n6 // tn),
        in_specs=[pl.BlockSpec((mb, d), lambda l, j: (0, 0)),
                  pl.BlockSpec((1, d, tn), lambda l, j: (l, 0, j)),
                  pl.BlockSpec((1, 1, tn), lambda l, j: (l, 0, j))],
        out_specs=pl.BlockSpec((1, mb, tn), lambda l, j: (l, 0, j)),
        out_shape=jax.ShapeDtypeStruct((depth, mb, n6), F32),
        compiler_params=_cparams(2),
        name="ada_mod",
    )(cvec, ada_w, ada_b.reshape(depth, 1, n6))


def _in_kernel(*refs, d, n_seg):
    x_ref, mod_ref, nw_ref = refs[:3]
    w_refs = refs[3:3 + n_seg]
    b_refs = refs[3 + n_seg:3 + 2 * n_seg]
    o_refs = refs[3 + 2 * n_seg:]
    h = _norm_mod(x_ref[...], nw_ref[...], mod_ref[0, :, 0:d], mod_ref[0, :, d:2 * d]).astype(BF16)
    for w_ref, b_ref, o_ref in zip(w_refs, b_refs, o_refs):
        o_ref[...] = jnp.dot(h, w_ref[...], preferred_element_type=F32) + b_ref[...]


def _in_call(x2d, mod, nw, ws, bs, rows_per_mod, tm):
    r, d = x2d.shape
    n_seg = len(ws)
    blocks_per_mod = rows_per_mod // tm
    const = lambda i: (0, 0)
    in_specs = [pl.BlockSpec((tm, d), lambda i: (i, 0)),
                pl.BlockSpec((1, 1, mod.shape[-1]), lambda i: (i // blocks_per_mod, 0, 0)),
                pl.BlockSpec((1, d), const)]
    in_specs += [pl.BlockSpec(w.shape, const) for w in ws]
    in_specs += [pl.BlockSpec(b.shape, const) for b in bs]
    return pl.pallas_call(
        functools.partial(_in_kernel, d=d, n_seg=n_seg),
        grid=(r // tm,),
        in_specs=in_specs,
        out_specs=[pl.BlockSpec((tm, w.shape[1]), lambda i: (i, 0)) for w in ws],
        out_shape=[jax.ShapeDtypeStruct((r, w.shape[1]), F32) for w in ws],
        compiler_params=_cparams(1),
        name="in_proj",
    )(x2d, mod, nw, *ws, *bs)


def _load_tile(x_ref):
    if len(x_ref.shape) == 2:
        return x_ref[...]
    return jnp.concatenate([x_ref[:, c, :] for c in range(x_ref.shape[1])], axis=0)


def _tile_specs(tm, d, col_view):
    if col_view is None:
        hb = tm // 8
        return (pl.BlockSpec((tm, d), lambda i: (i, 0)),
                lambda n_rows: pl.BlockSpec((8, d), lambda i: (jnp.maximum(i * hb - 1, 0), 0)),
                lambda n_rows: pl.BlockSpec((8, d), lambda i: (jnp.minimum((i + 1) * hb, n_rows // 8 - 1), 0)),
                lambda x2d: x2d)
    rows, cols = col_view
    ncol = tm // rows
    tps = cols // ncol
    hb = ncol // 8
    return (pl.BlockSpec((rows, ncol, d), lambda i: (i // tps, i % tps, 0)),
            lambda n_rows: pl.BlockSpec((rows, 8, d), lambda i: (i // tps, jnp.maximum((i % tps) * hb - 1, 0), 0)),
            lambda n_rows: pl.BlockSpec((rows, 8, d),
                                        lambda i: (i // tps, jnp.minimum((i % tps + 1) * hb, cols // 8 - 1), 0)),
            lambda x2d: x2d.reshape(-1, cols, d))


def _gmlp_chunk(x, ws_ref, bs_ref, nw_ref, nb_ref, lane_lo):
    g = 0.5 * x * (1.0 + jnp.tanh(math.sqrt(2.0 / math.pi) * (x + 0.044715 * (x * x * x))))
    u, v = g[:, 0:256], g[:, 256:512]
    mu = jnp.mean(v, axis=-1, keepdims=True)
    vc = v - mu
    var = jnp.mean(vc * vc, axis=-1, keepdims=True)
    vn = (vc * lax.rsqrt(var + NORM_EPS) * nw_ref[...] + nb_ref[...]).astype(BF16)
    mixed = []
    for pair in range(C_GROUPS // 2):
        v_p = vn[:, pair * LANE:(pair + 1) * LANE]
        mixed.append(jnp.where(lane_lo, jnp.dot(ws_ref[2 * pair], v_p, preferred_element_type=F32),
                               jnp.dot(ws_ref[2 * pair + 1], v_p, preferred_element_type=F32)))
    return u * (jnp.concatenate(mixed, axis=1) + bs_ref[...])


def _in_full_kernel(x_ref, xp_ref, xn_ref, mod_ref, nw_ref, wa_ref, wb_ref, wc_ref, wd_ref,
                    ba_ref, bb_ref, bc_ref, bd_ref, sw_ref, sb_ref, gws_ref, gbs_ref, gnw_ref, gnb_ref,
                    oa_ref, ox0_ref, ou_ref, oc_ref, od_ref, *, d, seq_len):
    shift, scale, nw = mod_ref[0, :, 0:d], mod_ref[0, :, d:2 * d], nw_ref[...]
    x = _load_tile(x_ref)
    tm = x.shape[0]
    h = _norm_mod(x, nw, shift, scale).astype(BF16)
    pc = jnp.dot(h, wc_ref[...], preferred_element_type=F32) + bc_ref[...]
    pb = jnp.dot(h, wb_ref[...], preferred_element_type=F32) + bb_ref[...]
    oa_ref[...] = jnp.dot(h, wa_ref[...], preferred_element_type=F32) + ba_ref[...]
    od_ref[...] = jnp.dot(h, wd_ref[...], preferred_element_type=F32) + bd_ref[...]
    lane_lo = lax.broadcasted_iota(jnp.int32, (C_CHUNK, LANE), 1) < 256 // C_GROUPS
    for c in range(tm // C_CHUNK):
        oc_ref[c * C_CHUNK:(c + 1) * C_CHUNK, :] = _gmlp_chunk(pc[c * C_CHUNK:(c + 1) * C_CHUNK], gws_ref, gbs_ref,
                                                                gnw_ref, gnb_ref, lane_lo)
    if len(xp_ref.shape) == 2:
        halo = jnp.concatenate([xp_ref[...], xn_ref[...]], axis=0)
    else:
        halo = jnp.concatenate([xp_ref[xp_ref.shape[0] - 1], xn_ref[0]], axis=0)
    pb_halo = jnp.dot(_norm_mod(halo, nw, shift, scale).astype(BF16), wb_ref[...],
                      preferred_element_type=F32) + bb_ref[...]
    row = lax.broadcasted_iota(jnp.int32, pb.shape, 0)
    pos = (pl.program_id(0) * tm + row) & (seq_len - 1)
    prev = jnp.where(row == 0, pb_halo[7:8, :], pltpu.roll(pb, 1, axis=0))
    nxt = jnp.where(row == tm - 1, pb_halo[8:9, :], pltpu.roll(pb, tm - 1, axis=0))
    prev = jnp.where(pos == 0, 0.0, prev)
    nxt = jnp.where(pos == seq_len - 1, 0.0, nxt)
    y = prev * sw_ref[0:1, :] + pb * sw_ref[1:2, :] + nxt * sw_ref[2:3, :] + sb_ref[...]
    ox0_ref[...] = y[:, 0:256]
    ou_ref[...] = y[:, 256:512] * y[:, 512:768]


def _in_full_call(x2d, mod, nw, ws, bs, conv_w, conv_b, gws, gbs, gnw, gnb, rows_per_mod, seq_len, tm,
                  col_view=None):
    r, d = x2d.shape
    assert seq_len & (seq_len - 1) == 0 and tm % C_CHUNK == 0
    blocks_per_mod = rows_per_mod // tm
    c2 = lambda i: (0, 0)
    full = lambda a: pl.BlockSpec(a.shape, (lambda i: (0,) * a.ndim))
    row = lambda w: pl.BlockSpec((tm, w), lambda i: (i, 0))
    x_spec, prev_spec, next_spec, view = _tile_specs(tm, d, col_view)
    in_specs = [x_spec, prev_spec(r), next_spec(r),
                pl.BlockSpec((1, 1, mod.shape[-1]), lambda i: (i // blocks_per_mod, 0, 0)),
                pl.BlockSpec((1, d), c2)]
    consts = list(ws) + list(bs) + [conv_w, conv_b, gws, gbs, gnw, gnb]
    in_specs += [full(a) for a in consts]
    widths = (ws[0].shape[1], 256, 256, 256, ws[3].shape[1])
    return pl.pallas_call(
        functools.partial(_in_full_kernel, d=d, seq_len=seq_len),
        grid=(r // tm,),
        in_specs=in_specs,
        out_specs=[row(w) for w in widths],
        out_shape=[jax.ShapeDtypeStruct((r, w), F32) for w in widths],
        compiler_params=_cparams(1),
        name="in_proj_full",
    )(view(x2d), view(x2d), view(x2d), mod, nw, *consts)


def _emit(y_ref, start, h, first):
    L = h.shape[0]
    if first:
        y_ref[0, pl.ds(start, L), :] = h
    else:
        y_ref[0, pl.ds(start, L), :] = y_ref[0, pl.ds(start, L), :] + h


def _mlstm_ones(L):
    ones_blk = np.zeros((2 * L, LANE), np.float32)
    ones_blk[:L, :HEAD_DIM] = 1.0
    ones_blk[L:, HEAD_DIM:] = 1.0
    return ones_blk


def _running(op, fill, x, row, reverse):
    L = x.shape[0]
    sh = 1
    while sh < L:
        if reverse:
            shifted, valid = pltpu.roll(x, L - sh, axis=0), row < L - sh
        else:
            shifted, valid = pltpu.roll(x, sh, axis=0), row >= sh
        x = op(x, jnp.where(valid, shifted, fill))
        sh *= 2
    return x


def _mlstm_chunk(blk, ones_ref, cn_ref, m_ref, d, mask, row, lane_lo, sub_lo, bd):
    L = blk.shape[0]
    reverse = d == 1
    end = 0 if reverse else L - 1
    cum = _running(jnp.add, 0.0, _log_sigmoid(blk[:, 1152:1280]), row, reverse)
    a = blk[:, 1024:1152] - cum
    m_prev = m_ref[d:d + 1, :]
    g = jnp.maximum(_running(jnp.maximum, NEG_BIG, a, row, reverse), m_prev)
    g_end = g[end:end + 1, :]
    w_inter = jnp.exp(m_prev - g)
    e_negm = jnp.exp(-(cum + g))
    m_ref[d:d + 1, :] = cum[end:end + 1, :] + g_end
    a_t = a.T

    def head_lanes(x, pair):
        c = 4 * d + 2 * pair
        return jnp.where(lane_lo, jnp.broadcast_to(x[:, c:c + 1], (L, LANE)),
                         jnp.broadcast_to(x[:, c + 1:c + 2], (L, LANE)))

    outs = []
    for pair in range(HEADS // 2):
        cs = slice(128 * pair, 128 * pair + 128)
        q_pair = blk[:, cs].astype(BF16)
        k_pair = blk[:, 256 + 128 * pair:256 + 128 * pair + 128] * (HEAD_DIM ** -0.5)
        v_pair = blk[:, 512 + 128 * pair:512 + 128 * pair + 128]
        k_b = k_pair.astype(BF16)
        zero = jnp.zeros_like(k_b)
        k_rows = jnp.concatenate([jnp.where(lane_lo, k_b, zero), jnp.where(lane_lo, zero, k_b)], axis=0)
        s_pair = lax.dot_general(q_pair, k_rows, (((1,), (1,)), ((), ())), preferred_element_type=F32)
        w_halves = []
        for sub in range(2):
            hd = 2 * pair + sub
            c = 4 * d + hd
            expo = jnp.where(mask, a_t[c:c + 1, :] - jnp.broadcast_to(g[:, c:c + 1], (L, L)), NEG_BIG)
            w_halves.append((jnp.exp(expo) * s_pair[:, sub * L:(sub + 1) * L]).astype(BF16))
        w_pair = jnp.concatenate(w_halves, axis=1)
        v_b = v_pair.astype(BF16)
        v_rows = jnp.concatenate([jnp.where(lane_lo, v_b, zero), jnp.where(lane_lo, zero, v_b)], axis=0)
        intra = jnp.dot(w_pair, jnp.concatenate([v_rows, ones_ref[...]], axis=1), preferred_element_type=F32)
        slot = 2 * d + pair
        cn = cn_ref[slot]
        inter = jnp.dot(q_pair, cn.astype(BF16), preferred_element_type=F32)
        wi_p = head_lanes(w_inter, pair)
        num = wi_p * inter[:, 0:128] + intra[:, 0:128]
        den = wi_p * inter[:, 128:256] + intra[:, 128:256]
        outs.append(num / jnp.maximum(jnp.abs(den), head_lanes(e_negm, pair)))
        c_e = 4 * d + 2 * pair
        w_st = jnp.exp(jnp.where(sub_lo, a_t[c_e:c_e + 1, :] - g_end[:, c_e:c_e + 1],
                                 a_t[c_e + 1:c_e + 2, :] - g_end[:, c_e + 1:c_e + 2]))
        ktw = (k_pair.T * w_st).astype(BF16)
        upd = jnp.dot(ktw, jnp.concatenate([v_b, jnp.ones_like(v_b)], axis=1), preferred_element_type=F32)
        decay = wi_p[end:end + 1, :]
        cn_ref[slot] = jnp.concatenate([decay, decay], axis=1) * cn + jnp.where(bd, upd, 0.0)
    return jnp.concatenate(outs, axis=1)


def _mlstm_kernel(*refs, need_ctx):
    ctx_ref, lat_ref, ones_ref = refs[:3]
    if need_ctx:
        yc_ref, yl_ref, cn_ref, m_ref = refs[3:]
    else:
        yl_ref, cn_ref, m_ref = refs[3:]
        yc_ref = None
    L = A_CHUNK
    lane_lo = lax.broadcasted_iota(jnp.int32, (L, LANE), 1) < HEAD_DIM
    row = lax.broadcasted_iota(jnp.int32, (L, LANE), 0)
    sub_lo = row < HEAD_DIM
    r2 = lax.broadcasted_iota(jnp.int32, (L, L), 0)
    c2 = lax.broadcasted_iota(jnp.int32, (L, L), 1)
    mask_f, mask_b = r2 >= c2, r2 <= c2
    rb = lax.broadcasted_iota(jnp.int32, (LANE, 2 * LANE), 0) < HEAD_DIM
    cb = (lax.broadcasted_iota(jnp.int32, (LANE, 2 * LANE), 1) % LANE) < HEAD_DIM
    bd = rb == cb
    cn_ref[...] = jnp.zeros_like(cn_ref)
    m_ref[...] = jnp.zeros_like(m_ref)

    def scan(src_ref, y_ref):
        n_chunks = src_ref.shape[1] // L
        assert n_chunks % 2 == 0

        def body(j, carry, first):
            sf = pl.multiple_of(j * L, L)
            sb = pl.multiple_of((n_chunks - 1 - j) * L, L)
            hf = _mlstm_chunk(src_ref[0, pl.ds(sf, L), :], ones_ref, cn_ref, m_ref, 0, mask_f, row, lane_lo, sub_lo, bd)
            hb = _mlstm_chunk(src_ref[0, pl.ds(sb, L), :], ones_ref, cn_ref, m_ref, 1, mask_b, row, lane_lo, sub_lo, bd)
            if y_ref is not None:
                _emit(y_ref, sf, hf, first)
                _emit(y_ref, sb, hb, first)
            return carry

        lax.fori_loop(0, n_chunks // 2, functools.partial(body, first=True), 0)
        lax.fori_loop(n_chunks // 2, n_chunks, functools.partial(body, first=False), 0)

    scan(ctx_ref, yc_ref)
    scan(lat_ref, yl_ref)


def _mlstm_call(pa_ctx, pa_lat, need_ctx):
    b, nc, wa = pa_ctx.shape
    n = pa_lat.shape[1]
    L = A_CHUNK
    ones_blk = jnp.asarray(_mlstm_ones(L), BF16)
    c2 = lambda i: (0, 0)
    out_specs = [pl.BlockSpec((1, n, 256), lambda i: (i, 0, 0))]
    out_shape = [jax.ShapeDtypeStruct((b, n, 256), F32)]
    if need_ctx:
        out_specs = [pl.BlockSpec((1, nc, 256), lambda i: (i, 0, 0))] + out_specs
        out_shape = [jax.ShapeDtypeStruct((b, nc, 256), F32)] + out_shape
    res = pl.pallas_call(
        functools.partial(_mlstm_kernel, need_ctx=need_ctx),
        grid=(b,),
        in_specs=[pl.BlockSpec((1, nc, wa), lambda i: (i, 0, 0)),
                  pl.BlockSpec((1, n, wa), lambda i: (i, 0, 0)),
                  pl.BlockSpec(ones_blk.shape, c2)],
        out_specs=out_specs,
        out_shape=out_shape,
        scratch_shapes=[pltpu.VMEM((HEADS, LANE, 2 * LANE), F32), pltpu.VMEM((8, LANE), F32)],
        compiler_params=_cparams(1),
        name="mlstm",
    )(pa_ctx, pa_lat, ones_blk)
    return tuple(res) if need_ctx else (None,) + tuple(res)


D_FAST_BASE = 16
D_FAST_MAX_EXPONENT = 60.0


def _hgrn_tables(L, base):
    t = np.arange(L)
    lev0 = int(math.log2(base))
    n_split = int(math.log2(L)) - lev0
    same_base = (t[:, None] // base) == (t[None, :] // base)
    lvl_f = np.where(same_base & (t[:, None] >= t[None, :]), 0, -1)
    lvl_b = np.where(same_base & (t[:, None] <= t[None, :]), 0, -1)
    mids_f, mids_b = [], []
    for i in range(n_split):
        half, size = base << i, base << (i + 1)
        same = (t[:, None] // size) == (t[None, :] // size)
        upper = (t // half) % 2 == 1
        lvl_f = np.where(same & upper[:, None] & ~upper[None, :], i + 1, lvl_f)
        lvl_b = np.where(same & ~upper[:, None] & upper[None, :], i + 1, lvl_b)
        starts = np.arange(0, L, size)
        mids_f.append([(int(s), size, int(s) + half - 1) for s in starts])
        mids_b.append([(int(s), size, int(s) + half) for s in starts])
    wide = lambda m: np.concatenate([m, m], axis=-1).astype(np.int32)
    return wide(lvl_f), wide(lvl_b), mids_f, mids_b


def _rows_of(b, spec):
    return jnp.concatenate([jnp.broadcast_to(b[r:r + 1, :], (n, b.shape[1])) for _, n, r in spec], axis=0)


def _hgrn_chunk(blk, f_pre, lb, lvl_ref, mids, s_ref, x_ref, d, row, lane_lo, bd, base):
    L = blk.shape[0]
    reverse = d == 1
    end = 0 if reverse else L - 1
    qs = _silu(blk[:, 0:256])
    v = blk[:, 256:512]
    sig, sig_neg = _sigmoid_both(f_pre)
    log_f = jnp.log(jnp.maximum(lb + (1.0 - lb) * sig, F_EPS))
    k = (1.0 - lb) * sig_neg
    b = _running(jnp.add, 0.0, log_f, row, reverse)
    b_end = b[end:end + 1, :]
    q_in = qs * jnp.exp(b)
    k_out = k * jnp.exp(b_end - b)
    decay = jnp.exp(b_end)
    factors = []
    if base > 1:
        zero_row = jnp.zeros((1, b.shape[1]), F32)
        if reverse:
            refs = [b[s + base:s + base + 1, :] if s + base < L else zero_row for s in range(0, L, base)]
        else:
            refs = [b[s - 1:s, :] if s > 0 else zero_row for s in range(0, L, base)]
        r0 = jnp.concatenate([jnp.broadcast_to(r, (base, b.shape[1])) for r in refs], axis=0)
        factors.append((jnp.exp(b - r0), jnp.exp(r0 - b)))
        last = [b[s:s + 1, :] if reverse else b[s + base - 1:s + base, :] for s in range(0, L, base)]
        worst = functools.reduce(jnp.maximum, [r - e for r, e in zip(refs, last)])
        x_ref[d:d + 1, :] = jnp.maximum(x_ref[d:d + 1, :], worst)
    else:
        factors.append((None, None))
    for spec in mids:
        e = jnp.exp(-jnp.abs(b - _rows_of(b, spec)))
        factors.append((e, e))
    nt = (((1,), (1,)), ((), ()))
    lvl = lvl_ref[...]
    outs = []
    for pair in range(HEADS // 2):
        cs = slice(128 * pair, 128 * pair + 128)
        q_p, k_p = qs[:, cs], k[:, cs]
        zero = jnp.zeros((L, LANE), BF16)
        v_b = v[:, cs].astype(BF16)
        v_rows = jnp.concatenate([jnp.where(lane_lo, v_b, zero), jnp.where(lane_lo, zero, v_b)], axis=0)
        scores = jnp.zeros((L, 2 * L), F32)
        for i, (eq, ek) in enumerate(factors):
            q_l = (q_p if eq is None else q_p * eq[:, cs]).astype(BF16)
            k_l = (k_p if ek is None else k_p * ek[:, cs]).astype(BF16)
            k_rows = jnp.concatenate([jnp.where(lane_lo, k_l, zero), jnp.where(lane_lo, zero, k_l)], axis=0)
            scores = jnp.where(lvl == i, lax.dot_general(q_l, k_rows, nt, preferred_element_type=F32), scores)
        slot = 2 * d + pair
        s_t = s_ref[slot]
        outs.append(jnp.dot(scores.astype(BF16), v_rows, preferred_element_type=F32)
                    + lax.dot_general(q_in[:, cs].astype(BF16), s_t.astype(BF16), nt, preferred_element_type=F32))
        upd = lax.dot_general(v_b, k_out[:, cs].astype(BF16), (((0,), (0,)), ((), ())), preferred_element_type=F32)
        s_ref[slot] = decay[:, cs] * s_t + jnp.where(bd, upd, 0.0)
    return jnp.concatenate(outs, axis=1)


def _hgrn_kernel(*refs, need_ctx, layer, mids_fast, mids_safe):
    ctx_ref, lat_ref, lvlf_fast_ref, lvlb_fast_ref, lvlf_safe_ref, lvlb_safe_ref, lbl_ref = refs[:7]
    if need_ctx:
        yc_ref, yl_ref, s_ref, x_ref = refs[7:]
    else:
        yl_ref, s_ref, x_ref = refs[7:]
        yc_ref = None
    L = D_CHUNK
    lane_lo = lax.broadcasted_iota(jnp.int32, (L, LANE), 1) < HEAD_DIM
    row = lax.broadcasted_iota(jnp.int32, (L, 2 * LANE), 0)
    bd = ((lax.broadcasted_iota(jnp.int32, (LANE, LANE), 0) < HEAD_DIM)
          == (lax.broadcasted_iota(jnp.int32, (LANE, LANE), 1) < HEAD_DIM))
    logits = lbl_ref[...]
    e = jnp.exp(logits - jnp.max(logits, axis=0, keepdims=True))
    prob = e / jnp.sum(e, axis=0, keepdims=True)
    lb = jnp.sum(prob[0:layer + 1], axis=0, keepdims=True) - prob[0:1]
    def scan(src_ref, y_ref, lvlf_ref, lvlb_ref, mids, base):
        n_chunks = src_ref.shape[1] // L
        assert n_chunks % 2 == 0

        def body(j, carry, first):
            sf = pl.multiple_of(j * L, L)
            sb = pl.multiple_of((n_chunks - 1 - j) * L, L)
            blk_f = src_ref[0, pl.ds(sf, L), :]
            blk_b = src_ref[0, pl.ds(sb, L), :]
            of = _hgrn_chunk(blk_f, blk_f[:, 512:768], lb, lvlf_ref, mids[0], s_ref, x_ref, 0, row, lane_lo, bd, base)
            ob = _hgrn_chunk(blk_b, blk_b[:, 768:1024], lb, lvlb_ref, mids[1], s_ref, x_ref, 1, row, lane_lo, bd, base)
            if y_ref is not None:
                _emit(y_ref, sf, of, first)
                _emit(y_ref, sb, ob, first)
            return carry

        lax.fori_loop(0, n_chunks // 2, functools.partial(body, first=True), 0)
        lax.fori_loop(n_chunks // 2, n_chunks, functools.partial(body, first=False), 0)

    def scans(lvlf_ref, lvlb_ref, mids, base):
        s_ref[...] = jnp.zeros_like(s_ref)
        scan(ctx_ref, yc_ref, lvlf_ref, lvlb_ref, mids, base)
        scan(lat_ref, yl_ref, lvlf_ref, lvlb_ref, mids, base)

    x_ref[...] = jnp.zeros_like(x_ref)
    scans(lvlf_fast_ref, lvlb_fast_ref, mids_fast, D_FAST_BASE)

    @pl.when(jnp.logical_not(jnp.max(x_ref[...]) <= D_FAST_MAX_EXPONENT))
    def _():
        scans(lvlf_safe_ref, lvlb_safe_ref, mids_safe, 1)


def _hgrn_call(pd_ctx, pd_lat, lb_logits, layer, need_ctx):
    b, nc, wd = pd_ctx.shape
    n = pd_lat.shape[1]
    L = D_CHUNK
    lvlf_fast, lvlb_fast, midsf_fast, midsb_fast = _hgrn_tables(L, D_FAST_BASE)
    lvlf_safe, lvlb_safe, midsf_safe, midsb_safe = _hgrn_tables(L, 1)
    tables = [jnp.asarray(a) for a in (lvlf_fast, lvlb_fast, lvlf_safe, lvlb_safe)]
    c2 = lambda i: (0, 0)
    out_specs = [pl.BlockSpec((1, n, 256), lambda i: (i, 0, 0))]
    out_shape = [jax.ShapeDtypeStruct((b, n, 256), F32)]
    if need_ctx:
        out_specs = [pl.BlockSpec((1, nc, 256), lambda i: (i, 0, 0))] + out_specs
        out_shape = [jax.ShapeDtypeStruct((b, nc, 256), F32)] + out_shape
    res = pl.pallas_call(
        functools.partial(_hgrn_kernel, need_ctx=need_ctx, layer=layer,
                          mids_fast=(midsf_fast, midsb_fast), mids_safe=(midsf_safe, midsb_safe)),
        grid=(b,),
        in_specs=[pl.BlockSpec((1, nc, wd), lambda i: (i, 0, 0)),
                  pl.BlockSpec((1, n, wd), lambda i: (i, 0, 0))]
                 + [pl.BlockSpec(a.shape, c2) for a in tables]
                 + [pl.BlockSpec(lb_logits.shape, c2)],
        out_specs=out_specs,
        out_shape=out_shape,
        scratch_shapes=[pltpu.VMEM((HEADS, LANE, LANE), F32), pltpu.VMEM((8, 256), F32)],
        compiler_params=_cparams(1),
        name="hgrn2",
    )(pd_ctx, pd_lat, *tables, lb_logits)
    return tuple(res) if need_ctx else (None,) + tuple(res)


def _filter_feats(n):
    t = np.linspace(0.0, 1.0, n, dtype=np.float32)[:, None]
    bands = np.linspace(1e-4, B_BANDS - 1, B_BANDS, dtype=np.float32)[None]
    ang = (np.float32(2 * math.pi) * bands * np.arange(n, dtype=np.float32)[:, None] / np.float32(n)).astype(np.float32)
    z = np.concatenate([t, np.cos(ang), -np.sin(ang)], axis=-1).astype(np.float32)
    deltas = np.abs(np.linspace(math.log(B_DECAY_TARGET) / B_SLOW_DECAY,
                                math.log(B_DECAY_TARGET) / B_FAST_DECAY, 256, dtype=np.float32))
    neg_t_deltas = (-t * deltas[None]).astype(np.float32)
    return z, neg_t_deltas


def _filter_kernel(z_ref, ntd_ref, w1_ref, b1_ref, f1_ref, w2_ref, b2_ref, f2_ref, w3_ref, flip_ref, o_ref):
    n = z_ref.shape[0]
    T = flip_ref.shape[0]
    hd = jnp.sin(f1_ref[...] * (_dot_f32(z_ref[...], w1_ref[...]) + b1_ref[...]))
    hd = jnp.sin(f2_ref[...] * (_dot_f32(hd, w2_ref[...]) + b2_ref[...]))
    decay = jnp.exp(ntd_ref[...])
    fwd = _dot_f32(hd, w3_ref[:, 0:256]) * decay
    bwd = _dot_f32(hd, w3_ref[:, 256:512]) * decay
    flip = flip_ref[...]
    rev = []
    for i in range(n // T):
        parts = _split3(bwd[n - (i + 1) * T:n - i * T])
        rev.append(sum(jnp.dot(flip, p, preferred_element_type=F32) for p in parts))
    rev = jnp.concatenate(rev, axis=0)
    row = lax.broadcasted_iota(jnp.int32, rev.shape, 0)
    o_ref[0:n, :] = jnp.where(row == 0, 0.0, pltpu.roll(rev, 1, axis=0))
    o_ref[n:2 * n, :] = fwd


def _filter_call(n, w1, b1, f1, w2, b2, f2, w3):
    z, ntd = _filter_feats(n)
    k_pad = 32
    z = np.pad(z, ((0, 0), (0, k_pad - z.shape[1])))
    w1 = jnp.pad(w1, ((0, k_pad - w1.shape[0]), (0, 0)))
    flip = jnp.asarray(np.eye(CONV_BLOCK, dtype=np.float32)[::-1], BF16)
    args = (jnp.asarray(z), jnp.asarray(ntd), w1, b1, f1, w2, b2, f2, w3, flip)
    return pl.pallas_call(
        _filter_kernel,
        grid=(1,),
        in_specs=[pl.BlockSpec(a.shape, lambda i: (0, 0)) for a in args],
        out_specs=pl.BlockSpec((2 * n, 256), lambda i: (0, 0)),
        out_shape=jax.ShapeDtypeStruct((2 * n, 256), F32),
        compiler_params=_cparams(1),
        name="hyena_filter",
    )(*args)


def _long_conv_kernel(u_ref, k_ref, o_ref, *, n_blk, bsz, ch_per_step):
    T = CONV_BLOCK
    half = T // 2
    for c in range(ch_per_step):
        taps = k_ref[c]
        rolled = pltpu.roll(jnp.broadcast_to(taps, (half, taps.shape[1])), 0, axis=1, stride=1, stride_axis=0)
        acc = [None] * n_blk
        for d in range(-(n_blk - 1), n_blk):
            base = (d + n_blk) * T
            tile = jnp.concatenate([rolled[:, base:base + T], rolled[:, base - half:base - half + T]],
                                   axis=0).astype(BF16)
            j_lo, j_hi = max(0, -d), min(n_blk, n_blk - d)
            lhs = u_ref[c, j_lo * bsz:j_hi * bsz, :].astype(BF16)
            res = jnp.dot(lhs, tile, preferred_element_type=F32)
            for j in range(j_lo, j_hi):
                part = res[(j - j_lo) * bsz:(j - j_lo + 1) * bsz]
                i = j + d
                acc[i] = part if acc[i] is None else acc[i] + part
        o_ref[c] = jnp.concatenate(acc, axis=0)


def _long_conv_call(u_t, k_t, n_blk, bsz, ch_per_step):
    ch, rows, T = u_t.shape
    return pl.pallas_call(
        functools.partial(_long_conv_kernel, n_blk=n_blk, bsz=bsz, ch_per_step=ch_per_step),
        grid=(ch // ch_per_step,),
        in_specs=[pl.BlockSpec((ch_per_step, rows, T), lambda i: (i, 0, 0)),
                  pl.BlockSpec((ch_per_step, 1, k_t.shape[-1]), lambda i: (i, 0, 0))],
        out_specs=pl.BlockSpec((ch_per_step, rows, T), lambda i: (i, 0, 0)),
        out_shape=jax.ShapeDtypeStruct((ch, rows, T), F32),
        compiler_params=_cparams(1),
        name="hyena_long_conv",
    )(u_t, k_t)


def _hyena_conv(u, filt):
    bsz, n, ch = u.shape
    T = CONV_BLOCK
    n_blk = n // T
    k_t = filt.T.reshape(ch, 1, 2 * n)
    u_t = u.reshape(bsz, n_blk, T, ch).transpose(3, 1, 0, 2).reshape(ch, n_blk * bsz, T)
    y_t = _long_conv_call(u_t, k_t, n_blk, bsz, 4)
    return y_t.reshape(ch, n_blk, bsz, T).transpose(2, 1, 3, 0).reshape(bsz, n, ch)


def _head_rms(y, lane_lo):
    parts = []
    for pair in range(HEADS // 2):
        yp = y[:, pair * LANE:(pair + 1) * LANE]
        y2 = yp * yp
        s_lo = jnp.sum(jnp.where(lane_lo, y2, 0.0), axis=-1, keepdims=True)
        s_all = jnp.sum(y2, axis=-1, keepdims=True)
        ms = jnp.where(lane_lo, s_lo, s_all - s_lo) * (1.0 / HEAD_DIM)
        parts.append(yp * lax.rsqrt(ms + NORM_EPS))
    return jnp.concatenate(parts, axis=1)


def _merge_kernel(x_ref, mod_ref, nw_ref, ya_ref, og_ref, x0_ref, u_ref, yconv_ref, yc_ref,
                  yd_ref, gg_ref, nwa_ref, nwd_ref, hyb_ref, wg_ref, wb_ref, wo_ref, o_ref, *, d):
    x = x_ref[...]
    h = _norm_mod(x, nw_ref[...], mod_ref[0, :, 0:d], mod_ref[0, :, d:2 * d]).astype(BF16)
    lane_lo = lax.broadcasted_iota(jnp.int32, (x.shape[0], LANE), 1) < HEAD_DIM
    y_a = _head_rms(_load_tile(ya_ref), lane_lo) * nwa_ref[...] * _sigmoid(_load_tile(og_ref))
    y_d = _head_rms(_load_tile(yd_ref), lane_lo) * nwd_ref[...] * _silu(_load_tile(gg_ref))
    y_b = _load_tile(x0_ref) * (_load_tile(yconv_ref) + _load_tile(u_ref) * hyb_ref[...])
    ys = [v.astype(BF16) for v in (y_a, y_b, _load_tile(yc_ref), y_d)]
    y = None
    n_part = d // MERGE_PARTS
    for part_i in range(MERGE_PARTS):
        cs = slice(part_i * n_part, (part_i + 1) * n_part)
        acc = None
        for j in range(N_BRANCH):
            gate = _sigmoid(jnp.dot(h, wg_ref[j, :, cs], preferred_element_type=F32))
            term = gate * jnp.dot(ys[j], wb_ref[j, :, cs], preferred_element_type=F32)
            acc = term if acc is None else acc + term
        part = jnp.dot(acc.astype(BF16), wo_ref[cs, :], preferred_element_type=F32)
        y = part if y is None else y + part
    o_ref[...] = x + mod_ref[0, :, 2 * d:3 * d] * y


def _merge_call(x2d, mod, nw, mix_a, mix_d, x0, u, yconv, yc, nwa, nwd, hyb, wg, wb, wo, rows_per_mod, tm,
                col_view=None):
    r, d = x2d.shape
    blocks_per_mod = rows_per_mod // tm
    c2 = lambda i: (0, 0)
    c3 = lambda i: (0, 0, 0)
    row = lambda w: pl.BlockSpec((tm, w), lambda i: (i, 0))
    if col_view is None:
        y_spec = lambda cb=0: pl.BlockSpec((tm, 256), lambda i: (i, cb))
        view = lambda y: y.reshape(r, -1)
    else:
        rows, cols = col_view
        nrow = tm // cols
        tps = rows // nrow
        y_spec = lambda cb=0: pl.BlockSpec((cols, nrow, 256), lambda i: (i // tps, i % tps, cb))
        view = lambda y: y.reshape(-1, rows, y.shape[-1])
    ya, pa = mix_a
    yd, pd = mix_d
    return pl.pallas_call(
        functools.partial(_merge_kernel, d=d),
        grid=(r // tm,),
        in_specs=[row(d), pl.BlockSpec((1, 1, mod.shape[-1]), lambda i: (i // blocks_per_mod, 0, 0)),
                  pl.BlockSpec((1, d), c2), y_spec(), y_spec(3), y_spec(), y_spec(), y_spec(), y_spec(),
                  y_spec(), y_spec(4),
                  pl.BlockSpec((1, 256), c2), pl.BlockSpec((1, 256), c2), pl.BlockSpec((1, 256), c2),
                  pl.BlockSpec(wg.shape, c3), pl.BlockSpec(wb.shape, c3), pl.BlockSpec(wo.shape, c2)],
        out_specs=row(d),
        out_shape=jax.ShapeDtypeStruct((r, d), F32),
        compiler_params=_cparams(1),
        name="merge",
    )(x2d, mod, nw, view(ya), view(pa), view(x0), view(u), view(yconv), view(yc),
      view(yd), view(pd), nwa, nwd, hyb, wg, wb, wo)


def _ffn_kernel(x_ref, mod_ref, nw_ref, wu_ref, wd_ref, fnw_ref, o_ref, *, d, hid_chunk, final_norm):
    x = x_ref[...]
    h = _norm_mod(x, nw_ref[...], mod_ref[0, :, 3 * d:4 * d], mod_ref[0, :, 4 * d:5 * d]).astype(BF16)
    hidden = wd_ref.shape[0]
    acc = None
    for c in range(hidden // hid_chunk):
        cs = slice(c * hid_chunk, (c + 1) * hid_chunk)
        a = jnp.dot(h, wu_ref[:, cs], preferred_element_type=F32)
        g = jnp.dot(h, wu_ref[:, hidden + c * hid_chunk:hidden + (c + 1) * hid_chunk], preferred_element_type=F32)
        term = jnp.dot((_silu(a) * g).astype(BF16), wd_ref[cs, :], preferred_element_type=F32)
        acc = term if acc is None else acc + term
    y = x + mod_ref[0, :, 5 * d:6 * d] * acc
    if final_norm:
        y = y * lax.rsqrt(jnp.mean(y * y, axis=-1, keepdims=True) + NORM_EPS) * fnw_ref[...]
    o_ref[...] = y


def _ffn_call(x2d, mod, nw, wu, wd, fnw, rows_per_mod, tm, final_norm):
    r, d = x2d.shape
    blocks_per_mod = rows_per_mod // tm
    c2 = lambda i: (0, 0)
    return pl.pallas_call(
        functools.partial(_ffn_kernel, d=d, hid_chunk=256, final_norm=final_norm),
        grid=(r // tm,),
        in_specs=[pl.BlockSpec((tm, d), lambda i: (i, 0)),
                  pl.BlockSpec((1, 1, mod.shape[-1]), lambda i: (i // blocks_per_mod, 0, 0)),
                  pl.BlockSpec((1, d), c2),
                  pl.BlockSpec(wu.shape, c2), pl.BlockSpec(wd.shape, c2),
                  pl.BlockSpec((1, d), c2)],
        out_specs=pl.BlockSpec((tm, d), lambda i: (i, 0)),
        out_shape=jax.ShapeDtypeStruct((r, d), F32),
        compiler_params=_cparams(1),
        name="ffn",
    )(x2d, mod, nw, wu, wd, fnw)


def _row_tile(rows):
    return 512 if rows % 512 == 0 else 256


def kernel(x, c, ctx, c_ctx, ada_w, ada_b, norm1_w, norm2_w, w_in, b_in, mlstm_norm_w, hy_short_w, hy_short_b, hy_w1, hy_b1, hy_freq1, hy_w2, hy_b2, hy_freq2, hy_w3, hy_bias, gm_norm_w, gm_norm_b, gm_ws, gm_bs, hg_lb_logits, hg_norm_w, w_gate, w_branch, w_out, w_ffn_in, w_ffn_out, final_norm_w):
    bsz, n, d = x.shape
    nc = ctx.shape[1]
    depth = ada_w.shape[0]
    rows = n // GRID_W
    mw = d // N_BRANCH
    assert mw == 256 and mw == HEADS * HEAD_DIM

    mb = -(-(bsz + 1) // 8) * 8
    cvec = jnp.concatenate([c, c_ctx[None], jnp.zeros((mb - bsz - 1, d), F32)], axis=0)
    mods = _ada_call(cvec, ada_w, ada_b)

    s_a, s_g, s_b, s_c = 4 * mw, 4 * mw + 4 * HEADS, 7 * mw + 4 * HEADS, 9 * mw + 4 * HEADS
    gate_pad = LANE - 2 * HEADS

    xc = ctx
    for l in range(depth):
        need_ctx = l < depth - 1
        col_major = l % 2 == 1
        col_view = (rows, GRID_W) if col_major else None
        mod_lat = mods[l, :bsz].reshape(bsz, 1, 6 * d)
        mod_ctx = mods[l, bsz:bsz + 1].reshape(1, 1, 6 * d)
        wl, bl = w_in[l].astype(BF16), b_in[l]
        wgt, bgt = wl[:, s_a:s_g].reshape(d, 4, HEADS), bl[s_a:s_g].reshape(4, HEADS)
        zw, zb = jnp.zeros((d, gate_pad), BF16), jnp.zeros((gate_pad,), F32)
        w_segs = [jnp.concatenate([wl[:, :s_a], wgt[:, 0], wgt[:, 2], zw, wgt[:, 1], wgt[:, 3], zw], axis=1),
                  wl[:, s_g:s_b], wl[:, s_b:s_c], wl[:, s_c:]]
        b_segs = [jnp.concatenate([bl[:s_a], bgt[0], bgt[2], zb, bgt[1], bgt[3], zb]),
                  bl[s_g:s_b], bl[s_b:s_c], bl[s_c:]]
        b_segs = [b.reshape(1, -1) for b in b_segs]
        nw1 = norm1_w[l].reshape(1, d)

        x2d = x.reshape(bsz * n, d)
        xc2d = xc.reshape(bsz * nc, d)
        hy_args = (hy_w1[l], hy_b1[l].reshape(1, -1), hy_freq1[l].reshape(1, -1), hy_w2[l],
                   hy_b2[l].reshape(1, -1), hy_freq2[l].reshape(1, -1), hy_w3[l])
        front = (hy_short_w[l], hy_short_b[l].reshape(1, -1), gm_ws[l].astype(BF16),
                 jnp.repeat(gm_bs[l].T, mw // C_GROUPS, axis=1), gm_norm_w[l].reshape(1, mw), gm_norm_b[l].reshape(1, mw))
        pa, x0, u, ycm, pd = _in_full_call(x2d, mod_lat, nw1, w_segs, b_segs, *front, n, n, _row_tile(n), col_view)
        if need_ctx:
            ca, x0c, uc, ycm_c, cd = _in_full_call(xc2d, mod_ctx, nw1, w_segs, b_segs, *front, bsz * nc, nc,
                                                   _row_tile(bsz * nc))
        else:
            ca, cd = _in_call(xc2d, mod_ctx, nw1, [w_segs[0], w_segs[3]], [b_segs[0], b_segs[3]],
                              bsz * nc, _row_tile(bsz * nc))

        pa3, pd3 = pa.reshape(bsz, n, -1), pd.reshape(bsz, n, -1)
        ca3, cd3 = ca.reshape(bsz, nc, -1), cd.reshape(bsz, nc, -1)
        yac, ya = _mlstm_call(ca3, pa3, need_ctx)
        ydc, yd = _hgrn_call(cd3, pd3, hg_lb_logits, l, need_ctx)
        nwa, nwd = mlstm_norm_w[l].reshape(1, mw), hg_norm_w[l].reshape(1, mw)

        hyb = hy_bias[l].reshape(1, mw)
        wg, wb, wo = w_gate[l].astype(BF16), w_branch[l].astype(BF16), w_out[l].astype(BF16)
        wfu = w_ffn_in[l].astype(BF16)
        wfd = w_ffn_out[l].astype(BF16)
        nw2 = norm2_w[l].reshape(1, d)
        fnw = final_norm_w.reshape(1, d)

        yconv = _hyena_conv(u.reshape(bsz, n, mw), _filter_call(n, *hy_args))
        x2d = _merge_call(x2d, mod_lat, nw1, (ya, pa), (yd, pd), x0, u, yconv, ycm, nwa, nwd,
                          hyb, wg, wb, wo, n, _row_tile(n), col_view)
        x2d = _ffn_call(x2d, mod_lat, nw2, wfu, wfd, fnw, n, _row_tile(n), l == depth - 1)
        x = x2d.reshape(bsz, n, d)

        if need_ctx:
            tmc = _row_tile(bsz * nc)
            yconv_c = _hyena_conv(uc.reshape(bsz, nc, mw), _filter_call(nc, *hy_args))
            xc2d = _merge_call(xc2d, mod_ctx, nw1, (yac, ca), (ydc, cd), x0c, uc, yconv_c, ycm_c,
                               nwa, nwd, hyb, wg, wb, wo, bsz * nc, tmc)
            xc2d = _ffn_call(xc2d, mod_ctx, nw2, wfu, wfd, fnw, bsz * nc, tmc, False)
            xc = xc2d.reshape(bsz, nc, d)
    return x
```

```python
import functools
import math

import numpy as np
import jax
import jax.numpy as jnp
from jax import lax
from jax.experimental import pallas as pl
from jax.experimental.pallas import tpu as pltpu

F32 = jnp.float32
BF16 = jnp.bfloat16

GRID_W = 64
NORM_EPS = 1e-6
N_BRANCH = 4
NEG_BIG = -1e30
F_EPS = 1e-30
HEAD_DIM = 64
HEADS = 4
LANE = 128
A_CHUNK = 128
D_CHUNK = 128
C_CHUNK = 128
C_GROUPS = 4
B_BANDS = 8
B_DECAY_TARGET = 1e-2
B_FAST_DECAY = 0.3
B_SLOW_DECAY = 1.5
MERGE_PARTS = 2
CONV_BLOCK = 256
VMEM_LIMIT = 56 * 1024 * 1024


def _cparams(n_axes):
    return pltpu.CompilerParams(dimension_semantics=("arbitrary",) * n_axes,
                                vmem_limit_bytes=VMEM_LIMIT)


def _split3(x):
    hi = x.astype(BF16)
    r1 = x - hi.astype(F32)
    mid = r1.astype(BF16)
    lo = (r1 - mid.astype(F32)).astype(BF16)
    return hi, mid, lo


def _dot_f32(a, b):
    a_hi, a_mid, _ = _split3(a)
    b_hi, b_mid, _ = _split3(b)
    d = lambda u, v: jnp.dot(u, v, preferred_element_type=F32)
    return d(a_hi, b_hi) + (d(a_hi, b_mid) + d(a_mid, b_hi))


def _sigmoid(x):
    return 1.0 / (1.0 + jnp.exp(-x))


def _sigmoid_both(x):
    e = jnp.exp(-jnp.abs(x))
    big = 1.0 / (1.0 + e)
    small = e * big
    pos = x >= 0.0
    return jnp.where(pos, big, small), jnp.where(pos, small, big)


def _silu(x):
    return x * _sigmoid(x)


def _log_sigmoid(x):
    return jnp.minimum(x, 0.0) - jnp.log(1.0 + jnp.exp(-jnp.abs(x)))


def _norm_mod(x, nw, shift, scale):
    ms = jnp.mean(x * x, axis=-1, keepdims=True)
    y = x * lax.rsqrt(ms + NORM_EPS) * nw
    return y * (1.0 + scale) + shift


def _ada_kernel(c_ref, w_ref, b_ref, o_ref):
    o_ref[0] = _dot_f32(_silu(c_ref[...]), w_ref[0]) + b_ref[0]


def _ada_call(cvec, ada_w, ada_b):
    depth, d, n6 = ada_w.shape
    mb = cvec.shape[0]
    tn = 512
    return pl.pallas_call(
        _ada_kernel,
        grid=(depth, n6 // tn),
        in_specs=[pl.BlockSpec((mb, d), lambda l, j: (0, 0)),
                  pl.BlockSpec((1, d, tn), lambda l, j: (l, 0, j)),
                  pl.BlockSpec((1, 1, tn), lambda l, j: (l, 0, j))],
        out_specs=pl.BlockSpec((1, mb, tn), lambda l, j: (l, 0, j)),
        out_shape=jax.ShapeDtypeStruct((depth, mb, n6), F32),
        compiler_params=_cparams(2),
        name="ada_mod",
    )(cvec, ada_w, ada_b.reshape(depth, 1, n6))


def _in_kernel(*refs, d, n_seg):
    x_ref, mod_ref, nw_ref = refs[:3]
    w_refs = refs[3:3 + n_seg]
    b_refs = refs[3 + n_seg:3 + 2 * n_seg]
    o_refs = refs[3 + 2 * n_seg:]
    h = _norm_mod(x_ref[...], nw_ref[...], mod_ref[0, :, 0:d], mod_ref[0, :, d:2 * d]).astype(BF16)
    for w_ref, b_ref, o_ref in zip(w_refs, b_refs, o_refs):
        o_ref[...] = jnp.dot(h, w_ref[...], preferred_element_type=F32) + b_ref[...]


def _in_call(x2d, mod, nw, ws, bs, rows_per_mod, tm):
    r, d = x2d.shape
    n_seg = len(ws)
    blocks_per_mod = rows_per_mod // tm
    const = lambda i: (0, 0)
    in_specs = [pl.BlockSpec((tm, d), lambda i: (i, 0)),
                pl.BlockSpec((1, 1, mod.shape[-1]), lambda i: (i // blocks_per_mod, 0, 0)),
                pl.BlockSpec((1, d), const)]
    in_specs += [pl.BlockSpec(w.shape, const) for w in ws]
    in_specs += [pl.BlockSpec(b.shape, const) for b in bs]
    return pl.pallas_call(
        functools.partial(_in_kernel, d=d, n_seg=n_seg),
        grid=(r // tm,),
        in_specs=in_specs,
        out_specs=[pl.BlockSpec((tm, w.shape[1]), lambda i: (i, 0)) for w in ws],
        out_shape=[jax.ShapeDtypeStruct((r, w.shape[1]), F32) for w in ws],
        compiler_params=_cparams(1),
        name="in_proj",
    )(x2d, mod, nw, *ws, *bs)


def _load_tile(x_ref):
    if len(x_ref.shape) == 2:
        return x_ref[...]
    return jnp.concatenate([x_ref[:, c, :] for c in range(x_ref.shape[1])], axis=0)


def _tile_specs(tm, d, col_view):
    if col_view is None:
        hb = tm // 8
        return (pl.BlockSpec((tm, d), lambda i: (i, 0)),
                lambda n_rows: pl.BlockSpec((8, d), lambda i: (jnp.maximum(i * hb - 1, 0), 0)),
                lambda n_rows: pl.BlockSpec((8, d), lambda i: (jnp.minimum((i + 1) * hb, n_rows // 8 - 1), 0)),
                lambda x2d: x2d)
    rows, cols = col_view
    ncol = tm // rows
    tps = cols // ncol
    hb = ncol // 8
    return (pl.BlockSpec((rows, ncol, d), lambda i: (i // tps, i % tps, 0)),
            lambda n_rows: pl.BlockSpec((rows, 8, d), lambda i: (i // tps, jnp.maximum((i % tps) * hb - 1, 0), 0)),
            lambda n_rows: pl.BlockSpec((rows, 8, d),
                                        lambda i: (i // tps, jnp.minimum((i % tps + 1) * hb, cols // 8 - 1), 0)),
            lambda x2d: x2d.reshape(-1, cols, d))


def _gmlp_chunk(x, ws_ref, bs_ref, nw_ref, nb_ref, lane_lo):
    g = 0.5 * x * (1.0 + jnp.tanh(math.sqrt(2.0 / math.pi) * (x + 0.044715 * (x * x * x))))
    u, v = g[:, 0:256], g[:, 256:512]
    mu = jnp.mean(v, axis=-1, keepdims=True)
    vc = v - mu
    var = jnp.mean(vc * vc, axis=-1, keepdims=True)
    vn = (vc * lax.rsqrt(var + NORM_EPS) * nw_ref[...] + nb_ref[...]).astype(BF16)
    mixed = []
    for pair in range(C_GROUPS // 2):
        v_p = vn[:, pair * LANE:(pair + 1) * LANE]
        mixed.append(jnp.where(lane_lo, jnp.dot(ws_ref[2 * pair], v_p, preferred_element_type=F32),
                               jnp.dot(ws_ref[2 * pair + 1], v_p, preferred_element_type=F32)))
    return u * (jnp.concatenate(mixed, axis=1) + bs_ref[...])


def _in_full_kernel(x_ref, xp_ref, xn_ref, mod_ref, nw_ref, wa_ref, wb_ref, wc_ref, wd_ref,
                    ba_ref, bb_ref, bc_ref, bd_ref, sw_ref, sb_ref, gws_ref, gbs_ref, gnw_ref, gnb_ref,
                    oa_ref, ox0_ref, ou_ref, oc_ref, od_ref, *, d, seq_len):
    shift, scale, nw = mod_ref[0, :, 0:d], mod_ref[0, :, d:2 * d], nw_ref[...]
    x = _load_tile(x_ref)
    tm = x.shape[0]
    h = _norm_mod(x, nw, shift, scale).astype(BF16)
    pc = jnp.dot(h, wc_ref[...], preferred_element_type=F32) + bc_ref[...]
    pb = jnp.dot(h, wb_ref[...], preferred_element_type=F32) + bb_ref[...]
    oa_ref[...] = jnp.dot(h, wa_ref[...], preferred_element_type=F32) + ba_ref[...]
    od_ref[...] = jnp.dot(h, wd_ref[...], preferred_element_type=F32) + bd_ref[...]
    lane_lo = lax.broadcasted_iota(jnp.int32, (C_CHUNK, LANE), 1) < 256 // C_GROUPS
    for c in range(tm // C_CHUNK):
        oc_ref[c * C_CHUNK:(c + 1) * C_CHUNK, :] = _gmlp_chunk(pc[c * C_CHUNK:(c + 1) * C_CHUNK], gws_ref, gbs_ref,
                                                                gnw_ref, gnb_ref, lane_lo)
    if len(xp_ref.shape) == 2:
        halo = jnp.concatenate([xp_ref[...], xn_ref[...]], axis=0)
    else:
        halo = jnp.concatenate([xp_ref[xp_ref.shape[0] - 1], xn_ref[0]], axis=0)
    pb_halo = jnp.dot(_norm_mod(halo, nw, shift, scale).astype(BF16), wb_ref[...],
                      preferred_element_type=F32) + bb_ref[...]
    row = lax.broadcasted_iota(jnp.int32, pb.shape, 0)
    pos = (pl.program_id(0) * tm + row) & (seq_len - 1)
    prev = jnp.where(row == 0, pb_halo[7:8, :], pltpu.roll(pb, 1, axis=0))
    nxt = jnp.where(row == tm - 1, pb_halo[8:9, :], pltpu.roll(pb, tm - 1, axis=0))
    prev = jnp.where(pos == 0, 0.0, prev)
    nxt = jnp.where(pos == seq_len - 1, 0.0, nxt)
    y = prev * sw_ref[0:1, :] + pb * sw_ref[1:2, :] + nxt * sw_ref[2:3, :] + sb_ref[...]
    ox0_ref[...] = y[:, 0:256]
    ou_ref[...] = y[:, 256:512] * y[:, 512:768]


def _in_full_call(x2d, mod, nw, ws, bs, conv_w, conv_b, gws, gbs, gnw, gnb, rows_per_mod, seq_len, tm,
                  col_view=None):
    r, d = x2d.shape
    assert seq_len & (seq_len - 1) == 0 and tm % C_CHUNK == 0
    blocks_per_mod = rows_per_mod // tm
    c2 = lambda i: (0, 0)
    full = lambda a: pl.BlockSpec(a.shape, (lambda i: (0,) * a.ndim))
    row = lambda w: pl.BlockSpec((tm, w), lambda i: (i, 0))
    x_spec, prev_spec, next_spec, view = _tile_specs(tm, d, col_view)
    in_specs = [x_spec, prev_spec(r), next_spec(r),
                pl.BlockSpec((1, 1, mod.shape[-1]), lambda i: (i // blocks_per_mod, 0, 0)),
                pl.BlockSpec((1, d), c2)]
    consts = list(ws) + list(bs) + [conv_w, conv_b, gws, gbs, gnw, gnb]
    in_specs += [full(a) for a in consts]
    widths = (ws[0].shape[1], 256, 256, 256, ws[3].shape[1])
    return pl.pallas_call(
        functools.partial(_in_full_kernel, d=d, seq_len=seq_len),
        grid=(r // tm,),
        in_specs=in_specs,
        out_specs=[row(w) for w in widths],
        out_shape=[jax.ShapeDtypeStruct((r, w), F32) for w in widths],
        compiler_params=_cparams(1),
        name="in_proj_full",
    )(view(x2d), view(x2d), view(x2d), mod, nw, *consts)


def _emit(y_ref, start, h, first):
    L = h.shape[0]
    if first:
        y_ref[0, pl.ds(start, L), :] = h
    else:
        y_ref[0, pl.ds(start, L), :] = y_ref[0, pl.ds(start, L), :] + h


def _mlstm_ones(L):
    ones_blk = np.zeros((2 * L, LANE), np.float32)
    ones_blk[:L, :HEAD_DIM] = 1.0
    ones_blk[L:, HEAD_DIM:] = 1.0
    return ones_blk


def _running(op, fill, x, row, reverse):
    L = x.shape[0]
    sh = 1
    while sh < L:
        if reverse:
            shifted, valid = pltpu.roll(x, L - sh, axis=0), row < L - sh
        else:
            shifted, valid = pltpu.roll(x, sh, axis=0), row >= sh
        x = op(x, jnp.where(valid, shifted, fill))
        sh *= 2
    return x


def _mlstm_chunk(blk, ones_ref, cn_ref, m_ref, d, mask, row, lane_lo, sub_lo, bd):
    L = blk.shape[0]
    reverse = d == 1
    end = 0 if reverse else L - 1
    cum = _running(jnp.add, 0.0, _log_sigmoid(blk[:, 1152:1280]), row, reverse)
    a = blk[:, 1024:1152] - cum
    m_prev = m_ref[d:d + 1, :]
    g = jnp.maximum(_running(jnp.maximum, NEG_BIG, a, row, reverse), m_prev)
    g_end = g[end:end + 1, :]
    w_inter = jnp.exp(m_prev - g)
    e_negm = jnp.exp(-(cum + g))
    m_ref[d:d + 1, :] = cum[end:end + 1, :] + g_end
    a_t = a.T

    def head_lanes(x, pair):
        c = 4 * d + 2 * pair
        return jnp.where(lane_lo, jnp.broadcast_to(x[:, c:c + 1], (L, LANE)),
                         jnp.broadcast_to(x[:, c + 1:c + 2], (L, LANE)))

    outs = []
    for pair in range(HEADS // 2):
        cs = slice(128 * pair, 128 * pair + 128)
        q_pair = blk[:, cs].astype(BF16)
        k_pair = blk[:, 256 + 128 * pair:256 + 128 * pair + 128] * (HEAD_DIM ** -0.5)
        v_pair = blk[:, 512 + 128 * pair:512 + 128 * pair + 128]
        k_b = k_pair.astype(BF16)
        zero = jnp.zeros_like(k_b)
        k_rows = jnp.concatenate([jnp.where(lane_lo, k_b, zero), jnp.where(lane_lo, zero, k_b)], axis=0)
        s_pair = lax.dot_general(q_pair, k_rows, (((1,), (1,)), ((), ())), preferred_element_type=F32)
        w_halves = []
        for sub in range(2):
            hd = 2 * pair + sub
            c = 4 * d + hd
            expo = jnp.where(mask, a_t[c:c + 1, :] - jnp.broadcast_to(g[:, c:c + 1], (L, L)), NEG_BIG)
            w_halves.append((jnp.exp(expo) * s_pair[:, sub * L:(sub + 1) * L]).astype(BF16))
        w_pair = jnp.concatenate(w_halves, axis=1)
        v_b = v_pair.astype(BF16)
        v_rows = jnp.concatenate([jnp.where(lane_lo, v_b, zero), jnp.where(lane_lo, zero, v_b)], axis=0)
        intra = jnp.dot(w_pair, jnp.concatenate([v_rows, ones_ref[...]], axis=1), preferred_element_type=F32)
        slot = 2 * d + pair
        cn = cn_ref[slot]
        inter = jnp.dot(q_pair, cn.astype(BF16), preferred_element_type=F32)
        wi_p = head_lanes(w_inter, pair)
        num = wi_p * inter[:, 0:128] + intra[:, 0:128]
        den = wi_p * inter[:, 128:256] + intra[:, 128:256]
        outs.append(num / jnp.maximum(jnp.abs(den), head_lanes(e_negm, pair)))
        c_e = 4 * d + 2 * pair
        w_st = jnp.exp(jnp.where(sub_lo, a_t[c_e:c_e + 1, :] - g_end[:, c_e:c_e + 1],
                                 a_t[c_e + 1:c_e + 2, :] - g_end[:, c_e + 1:c_e + 2]))
        ktw = (k_pair.T * w_st).astype(BF16)
        upd = jnp.dot(ktw, jnp.concatenate([v_b, jnp.ones_like(v_b)], axis=1), preferred_element_type=F32)
        decay = wi_p[end:end + 1, :]
        cn_ref[slot] = jnp.concatenate([decay, decay], axis=1) * cn + jnp.where(bd, upd, 0.0)
    return jnp.concatenate(outs, axis=1)


def _mlstm_kernel(*refs, need_ctx):
    ctx_ref, lat_ref, ones_ref = refs[:3]
    if need_ctx:
        yc_ref, yl_ref, cn_ref, m_ref = refs[3:]
    else:
        yl_ref, cn_ref, m_ref = refs[3:]
        yc_ref = None
    L = A_CHUNK
    lane_lo = lax.broadcasted_iota(jnp.int32, (L, LANE), 1) < HEAD_DIM
    row = lax.broadcasted_iota(jnp.int32, (L, LANE), 0)
    sub_lo = row < HEAD_DIM
    r2 = lax.broadcasted_iota(jnp.int32, (L, L), 0)
    c2 = lax.broadcasted_iota(jnp.int32, (L, L), 1)
    mask_f, mask_b = r2 >= c2, r2 <= c2
    rb = lax.broadcasted_iota(jnp.int32, (LANE, 2 * LANE), 0) < HEAD_DIM
    cb = (lax.broadcasted_iota(jnp.int32, (LANE, 2 * LANE), 1) % LANE) < HEAD_DIM
    bd = rb == cb
    cn_ref[...] = jnp.zeros_like(cn_ref)
    m_ref[...] = jnp.zeros_like(m_ref)

    def scan(src_ref, y_ref):
        n_chunks = src_ref.shape[1] // L
        assert n_chunks % 2 == 0

        def body(j, carry, first):
            sf = pl.multiple_of(j * L, L)
            sb = pl.multiple_of((n_chunks - 1 - j) * L, L)
            hf = _mlstm_chunk(src_ref[0, pl.ds(sf, L), :], ones_ref, cn_ref, m_ref, 0, mask_f, row, lane_lo, sub_lo, bd)
            hb = _mlstm_chunk(src_ref[0, pl.ds(sb, L), :], ones_ref, cn_ref, m_ref, 1, mask_b, row, lane_lo, sub_lo, bd)
            if y_ref is not None:
                _emit(y_ref, sf, hf, first)
                _emit(y_ref, sb, hb, first)
            return carry

        lax.fori_loop(0, n_chunks // 2, functools.partial(body, first=True), 0)
        lax.fori_loop(n_chunks // 2, n_chunks, functools.partial(body, first=False), 0)

    scan(ctx_ref, yc_ref)
    scan(lat_ref, yl_ref)


def _mlstm_call(pa_ctx, pa_lat, need_ctx):
    b, nc, wa = pa_ctx.shape
    n = pa_lat.shape[1]
    L = A_CHUNK
    ones_blk = jnp.asarray(_mlstm_ones(L), BF16)
    c2 = lambda i: (0, 0)
    out_specs = [pl.BlockSpec((1, n, 256), lambda i: (i, 0, 0))]
    out_shape = [jax.ShapeDtypeStruct((b, n, 256), F32)]
    if need_ctx:
        out_specs = [pl.BlockSpec((1, nc, 256), lambda i: (i, 0, 0))] + out_specs
        out_shape = [jax.ShapeDtypeStruct((b, nc, 256), F32)] + out_shape
    res = pl.pallas_call(
        functools.partial(_mlstm_kernel, need_ctx=need_ctx),
        grid=(b,),
        in_specs=[pl.BlockSpec((1, nc, wa), lambda i: (i, 0, 0)),
                  pl.BlockSpec((1, n, wa), lambda i: (i, 0, 0)),
                  pl.BlockSpec(ones_blk.shape, c2)],
        out_specs=out_specs,
        out_shape=out_shape,
        scratch_shapes=[pltpu.VMEM((HEADS, LANE, 2 * LANE), F32), pltpu.VMEM((8, LANE), F32)],
        compiler_params=_cparams(1),
        name="mlstm",
    )(pa_ctx, pa_lat, ones_blk)
    return tuple(res) if need_ctx else (None,) + tuple(res)


D_FAST_BASE = 16
D_FAST_MAX_EXPONENT = 60.0


def _hgrn_tables(L, base):
    t = np.arange(L)
    lev0 = int(math.log2(base))
    n_split = int(math.log2(L)) - lev0
    same_base = (t[:, None] // base) == (t[None, :] // base)
    lvl_f = np.where(same_base & (t[:, None] >= t[None, :]), 0, -1)
    lvl_b = np.where(same_base & (t[:, None] <= t[None, :]), 0, -1)
    mids_f, mids_b = [], []
    for i in range(n_split):
        half, size = base << i, base << (i + 1)
        same = (t[:, None] // size) == (t[None, :] // size)
        upper = (t // half) % 2 == 1
        lvl_f = np.where(same & upper[:, None] & ~upper[None, :], i + 1, lvl_f)
        lvl_b = np.where(same & ~upper[:, None] & upper[None, :], i + 1, lvl_b)
        starts = np.arange(0, L, size)
        mids_f.append([(int(s), size, int(s) + half - 1) for s in starts])
        mids_b.append([(int(s), size, int(s) + half) for s in starts])
    wide = lambda m: np.concatenate([m, m], axis=-1).astype(np.int32)
    return wide(lvl_f), wide(lvl_b), mids_f, mids_b


def _rows_of(b, spec):
    return jnp.concatenate([jnp.broadcast_to(b[r:r + 1, :], (n, b.shape[1])) for _, n, r in spec], axis=0)


def _hgrn_chunk(blk, f_pre, lb, lvl_ref, mids, s_ref, x_ref, d, row, lane_lo, bd, base):
    L = blk.shape[0]
    reverse = d == 1
    end = 0 if reverse else L - 1
    qs = _silu(blk[:, 0:256])
    v = blk[:, 256:512]
    sig, sig_neg = _sigmoid_both(f_pre)
    log_f = jnp.log(jnp.maximum(lb + (1.0 - lb) * sig, F_EPS))
    k = (1.0 - lb) * sig_neg
    b = _running(jnp.add, 0.0, log_f, row, reverse)
    b_end = b[end:end + 1, :]
    q_in = qs * jnp.exp(b)
    k_out = k * jnp.exp(b_end - b)
    decay = jnp.exp(b_end)
    factors = []
    if base > 1:
        zero_row = jnp.zeros((1, b.shape[1]), F32)
        if reverse:
            refs = [b[s + base:s + base + 1, :] if s + base < L else zero_row for s in range(0, L, base)]
        else:
            refs = [b[s - 1:s, :] if s > 0 else zero_row for s in range(0, L, base)]
        r0 = jnp.concatenate([jnp.broadcast_to(r, (base, b.shape[1])) for r in refs], axis=0)
        factors.append((jnp.exp(b - r0), jnp.exp(r0 - b)))
        last = [b[s:s + 1, :] if reverse else b[s + base - 1:s + base, :] for s in range(0, L, base)]
        worst = functools.reduce(jnp.maximum, [r - e for r, e in zip(refs, last)])
        x_ref[d:d + 1, :] = jnp.maximum(x_ref[d:d + 1, :], worst)
    else:
        factors.append((None, None))
    for spec in mids:
        e = jnp.exp(-jnp.abs(b - _rows_of(b, spec)))
        factors.append((e, e))
    nt = (((1,), (1,)), ((), ()))
    lvl = lvl_ref[...]
    outs = []
    for pair in range(HEADS // 2):
        cs = slice(128 * pair, 128 * pair + 128)
        q_p, k_p = qs[:, cs], k[:, cs]
        zero = jnp.zeros((L, LANE), BF16)
        v_b = v[:, cs].astype(BF16)
        v_rows = jnp.concatenate([jnp.where(lane_lo, v_b, zero), jnp.where(lane_lo, zero, v_b)], axis=0)
        scores = jnp.zeros((L, 2 * L), F32)
        for i, (eq, ek) in enumerate(factors):
            q_l = (q_p if eq is None else q_p * eq[:, cs]).astype(BF16)
            k_l = (k_p if ek is None else k_p * ek[:, cs]).astype(BF16)
            k_rows = jnp.concatenate([jnp.where(lane_lo, k_l, zero), jnp.where(lane_lo, zero, k_l)], axis=0)
            scores = jnp.where(lvl == i, lax.dot_general(q_l, k_rows, nt, preferred_element_type=F32), scores)
        slot = 2 * d + pair
        s_t = s_ref[slot]
        outs.append(jnp.dot(scores.astype(BF16), v_rows, preferred_element_type=F32)
                    + lax.dot_general(q_in[:, cs].astype(BF16), s_t.astype(BF16), nt, preferred_element_type=F32))
        upd = lax.dot_general(v_b, k_out[:, cs].astype(BF16), (((0,), (0,)), ((), ())), preferred_element_type=F32)
        s_ref[slot] = decay[:, cs] * s_t + jnp.where(bd, upd, 0.0)
    return jnp.concatenate(outs, axis=1)


def _hgrn_kernel(*refs, need_ctx, layer, mids_fast, mids_safe):
    ctx_ref, lat_ref, lvlf_fast_ref, lvlb_fast_ref, lvlf_safe_ref, lvlb_safe_ref, lbl_ref = refs[:7]
    if need_ctx:
        yc_ref, yl_ref, s_ref, x_ref = refs[7:]
    else:
        yl_ref, s_ref, x_ref = refs[7:]
        yc_ref = None
    L = D_CHUNK
    lane_lo = lax.broadcasted_iota(jnp.int32, (L, LANE), 1) < HEAD_DIM
    row = lax.broadcasted_iota(jnp.int32, (L, 2 * LANE), 0)
    bd = ((lax.broadcasted_iota(jnp.int32, (LANE, LANE), 0) < HEAD_DIM)
          == (lax.broadcasted_iota(jnp.int32, (LANE, LANE), 1) < HEAD_DIM))
    logits = lbl_ref[...]
    e = jnp.exp(logits - jnp.max(logits, axis=0, keepdims=True))
    prob = e / jnp.sum(e, axis=0, keepdims=True)
    lb = jnp.sum(prob[0:layer + 1], axis=0, keepdims=True) - prob[0:1]
    def scan(src_ref, y_ref, lvlf_ref, lvlb_ref, mids, base):
        n_chunks = src_ref.shape[1] // L
        assert n_chunks % 2 == 0

        def body(j, carry, first):
            sf = pl.multiple_of(j * L, L)
            sb = pl.multiple_of((n_chunks - 1 - j) * L, L)
            blk_f = src_ref[0, pl.ds(sf, L), :]
            blk_b = src_ref[0, pl.ds(sb, L), :]
            of = _hgrn_chunk(blk_f, blk_f[:, 512:768], lb, lvlf_ref, mids[0], s_ref, x_ref, 0, row, lane_lo, bd, base)
            ob = _hgrn_chunk(blk_b, blk_b[:, 768:1024], lb, lvlb_ref, mids[1], s_ref, x_ref, 1, row, lane_lo, bd, base)
            if y_ref is not None:
                _emit(y_ref, sf, of, first)
                _emit(y_ref, sb, ob, first)
            return carry

        lax.fori_loop(0, n_chunks // 2, functools.partial(body, first=True), 0)
        lax.fori_loop(n_chunks // 2, n_chunks, functools.partial(body, first=False), 0)

    def scans(lvlf_ref, lvlb_ref, mids, base):
        s_ref[...] = jnp.zeros_like(s_ref)
        scan(ctx_ref, yc_ref, lvlf_ref, lvlb_ref, mids, base)
        scan(lat_ref, yl_ref, lvlf_ref, lvlb_ref, mids, base)

    x_ref[...] = jnp.zeros_like(x_ref)
    scans(lvlf_fast_ref, lvlb_fast_ref, mids_fast, D_FAST_BASE)

    @pl.when(jnp.logical_not(jnp.max(x_ref[...]) <= D_FAST_MAX_EXPONENT))
    def _():
        scans(lvlf_safe_ref, lvlb_safe_ref, mids_safe, 1)


def _hgrn_call(pd_ctx, pd_lat, lb_logits, layer, need_ctx):
    b, nc, wd = pd_ctx.shape
    n = pd_lat.shape[1]
    L = D_CHUNK
    lvlf_fast, lvlb_fast, midsf_fast, midsb_fast = _hgrn_tables(L, D_FAST_BASE)
    lvlf_safe, lvlb_safe, midsf_safe, midsb_safe = _hgrn_tables(L, 1)
    tables = [jnp.asarray(a) for a in (lvlf_fast, lvlb_fast, lvlf_safe, lvlb_safe)]
    c2 = lambda i: (0, 0)
    out_specs = [pl.BlockSpec((1, n, 256), lambda i: (i, 0, 0))]
    out_shape = [jax.ShapeDtypeStruct((b, n, 256), F32)]
    if need_ctx:
        out_specs = [pl.BlockSpec((1, nc, 256), lambda i: (i, 0, 0))] + out_specs
        out_shape = [jax.ShapeDtypeStruct((b, nc, 256), F32)] + out_shape
    res = pl.pallas_call(
        functools.partial(_hgrn_kernel, need_ctx=need_ctx, layer=layer,
                          mids_fast=(midsf_fast, midsb_fast), mids_safe=(midsf_safe, midsb_safe)),
        grid=(b,),
        in_specs=[pl.BlockSpec((1, nc, wd), lambda i: (i, 0, 0)),
                  pl.BlockSpec((1, n, wd), lambda i: (i, 0, 0))]
                 + [pl.BlockSpec(a.shape, c2) for a in tables]
                 + [pl.BlockSpec(lb_logits.shape, c2)],
        out_specs=out_specs,
        out_shape=out_shape,
        scratch_shapes=[pltpu.VMEM((HEADS, LANE, LANE), F32), pltpu.VMEM((8, 256), F32)],
        compiler_params=_cparams(1),
        name="hgrn2",
    )(pd_ctx, pd_lat, *tables, lb_logits)
    return tuple(res) if need_ctx else (None,) + tuple(res)


def _filter_feats(n):
    t = np.linspace(0.0, 1.0, n, dtype=np.float32)[:, None]
    bands = np.linspace(1e-4, B_BANDS - 1, B_BANDS, dtype=np.float32)[None]
    ang = (np.float32(2 * math.pi) * bands * np.arange(n, dtype=np.float32)[:, None] / np.float32(n)).astype(np.float32)
    z = np.concatenate([t, np.cos(ang), -np.sin(ang)], axis=-1).astype(np.float32)
    deltas = np.abs(np.linspace(math.log(B_DECAY_TARGET) / B_SLOW_DECAY,
                                math.log(B_DECAY_TARGET) / B_FAST_DECAY, 256, dtype=np.float32))
    neg_t_deltas = (-t * deltas[None]).astype(np.float32)
    return z, neg_t_deltas


def _filter_kernel(z_ref, ntd_ref, w1_ref, b1_ref, f1_ref, w2_ref, b2_ref, f2_ref, w3_ref, flip_ref, o_ref):
    n = z_ref.shape[0]
    T = flip_ref.shape[0]
    hd = jnp.sin(f1_ref[...] * (_dot_f32(z_ref[...], w1_ref[...]) + b1_ref[...]))
    hd = jnp.sin(f2_ref[...] * (_dot_f32(hd, w2_ref[...]) + b2_ref[...]))
    decay = jnp.exp(ntd_ref[...])
    fwd = _dot_f32(hd, w3_ref[:, 0:256]) * decay
    bwd = _dot_f32(hd, w3_ref[:, 256:512]) * decay
    flip = flip_ref[...]
    rev = []
    for i in range(n // T):
        parts = _split3(bwd[n - (i + 1) * T:n - i * T])
        rev.append(sum(jnp.dot(flip, p, preferred_element_type=F32) for p in parts))
    rev = jnp.concatenate(rev, axis=0)
    row = lax.broadcasted_iota(jnp.int32, rev.shape, 0)
    o_ref[0:n, :] = jnp.where(row == 0, 0.0, pltpu.roll(rev, 1, axis=0))
    o_ref[n:2 * n, :] = fwd


def _filter_call(n, w1, b1, f1, w2, b2, f2, w3):
    z, ntd = _filter_feats(n)
    k_pad = 32
    z = np.pad(z, ((0, 0), (0, k_pad - z.shape[1])))
    w1 = jnp.pad(w1, ((0, k_pad - w1.shape[0]), (0, 0)))
    flip = jnp.asarray(np.eye(CONV_BLOCK, dtype=np.float32)[::-1], BF16)
    args = (jnp.asarray(z), jnp.asarray(ntd), w1, b1, f1, w2, b2, f2, w3, flip)
    return pl.pallas_call(
        _filter_kernel,
        grid=(1,),
        in_specs=[pl.BlockSpec(a.shape, lambda i: (0, 0)) for a in args],
        out_specs=pl.BlockSpec((2 * n, 256), lambda i: (0, 0)),
        out_shape=jax.ShapeDtypeStruct((2 * n, 256), F32),
        compiler_params=_cparams(1),
        name="hyena_filter",
    )(*args)


def _long_conv_kernel(u_ref, k_ref, o_ref, *, n_blk, bsz, ch_per_step):
    T = CONV_BLOCK
    half = T // 2
    for c in range(ch_per_step):
        taps = k_ref[c]
        rolled = pltpu.roll(jnp.broadcast_to(taps, (half, taps.shape[1])), 0, axis=1, stride=1, stride_axis=0)
        acc = [None] * n_blk
        for d in range(-(n_blk - 1), n_blk):
            base = (d + n_blk) * T
            tile = jnp.concatenate([rolled[:, base:base + T], rolled[:, base - half:base - half + T]],
                                   axis=0).astype(BF16)
            j_lo, j_hi = max(0, -d), min(n_blk, n_blk - d)
            lhs = u_ref[c, j_lo * bsz:j_hi * bsz, :].astype(BF16)
            res = jnp.dot(lhs, tile, preferred_element_type=F32)
            for j in range(j_lo, j_hi):
                part = res[(j - j_lo) * bsz:(j - j_lo + 1) * bsz]
                i = j + d
                acc[i] = part if acc[i] is None else acc[i] + part
        o_ref[c] = jnp.concatenate(acc, axis=0)


def _long_conv_call(u_t, k_t, n_blk, bsz, ch_per_step):
    ch, rows, T = u_t.shape
    return pl.pallas_call(
        functools.partial(_long_conv_kernel, n_blk=n_blk, bsz=bsz, ch_per_step=ch_per_step),
        grid=(ch // ch_per_step,),
        in_specs=[pl.BlockSpec((ch_per_step, rows, T), lambda i: (i, 0, 0)),
                  pl.BlockSpec((ch_per_step, 1, k_t.shape[-1]), lambda i: (i, 0, 0))],
        out_specs=pl.BlockSpec((ch_per_step, rows, T), lambda i: (i, 0, 0)),
        out_shape=jax.ShapeDtypeStruct((ch, rows, T), F32),
        compiler_params=_cparams(1),
        name="hyena_long_conv",
    )(u_t, k_t)


def _hyena_conv(u, filt):
    bsz, n, ch = u.shape
    T = CONV_BLOCK
    n_blk = n // T
    k_t = filt.T.reshape(ch, 1, 2 * n)
    u_t = u.reshape(bsz, n_blk, T, ch).transpose(3, 1, 0, 2).reshape(ch, n_blk * bsz, T)
    y_t = _long_conv_call(u_t, k_t, n_blk, bsz, 8)
    return y_t.reshape(ch, n_blk, bsz, T).transpose(2, 1, 3, 0).reshape(bsz, n, ch)


def _head_rms(y, lane_lo):
    parts = []
    for pair in range(HEADS // 2):
        yp = y[:, pair * LANE:(pair + 1) * LANE]
        y2 = yp * yp
        s_lo = jnp.sum(jnp.where(lane_lo, y2, 0.0), axis=-1, keepdims=True)
        s_all = jnp.sum(y2, axis=-1, keepdims=True)
        ms = jnp.where(lane_lo, s_lo, s_all - s_lo) * (1.0 / HEAD_DIM)
        parts.append(yp * lax.rsqrt(ms + NORM_EPS))
    return jnp.concatenate(parts, axis=1)


def _merge_kernel(x_ref, mod_ref, nw_ref, ya_ref, og_ref, x0_ref, u_ref, yconv_ref, yc_ref,
                  yd_ref, gg_ref, nwa_ref, nwd_ref, hyb_ref, wg_ref, wb_ref, wo_ref, o_ref, *, d):
    x = x_ref[...]
    h = _norm_mod(x, nw_ref[...], mod_ref[0, :, 0:d], mod_ref[0, :, d:2 * d]).astype(BF16)
    lane_lo = lax.broadcasted_iota(jnp.int32, (x.shape[0], LANE), 1) < HEAD_DIM
    y_a = _head_rms(_load_tile(ya_ref), lane_lo) * nwa_ref[...] * _sigmoid(_load_tile(og_ref))
    y_d = _head_rms(_load_tile(yd_ref), lane_lo) * nwd_ref[...] * _silu(_load_tile(gg_ref))
    y_b = _load_tile(x0_ref) * (_load_tile(yconv_ref) + _load_tile(u_ref) * hyb_ref[...])
    ys = [v.astype(BF16) for v in (y_a, y_b, _load_tile(yc_ref), y_d)]
    y = None
    n_part = d // MERGE_PARTS
    for part_i in range(MERGE_PARTS):
        cs = slice(part_i * n_part, (part_i + 1) * n_part)
        acc = None
        for j in range(N_BRANCH):
            gate = _sigmoid(jnp.dot(h, wg_ref[j, :, cs], preferred_element_type=F32))
            term = gate * jnp.dot(ys[j], wb_ref[j, :, cs], preferred_element_type=F32)
            acc = term if acc is None else acc + term
        part = jnp.dot(acc.astype(BF16), wo_ref[cs, :], preferred_element_type=F32)
        y = part if y is None else y + part
    o_ref[...] = x + mod_ref[0, :, 2 * d:3 * d] * y


def _merge_call(x2d, mod, nw, mix_a, mix_d, x0, u, yconv, yc, nwa, nwd, hyb, wg, wb, wo, rows_per_mod, tm,
                col_view=None):
    r, d = x2d.shape
    blocks_per_mod = rows_per_mod // tm
    c2 = lambda i: (0, 0)
    c3 = lambda i: (0, 0, 0)
    row = lambda w: pl.BlockSpec((tm, w), lambda i: (i, 0))
    if col_view is None:
        y_spec = lambda cb=0: pl.BlockSpec((tm, 256), lambda i: (i, cb))
        view = lambda y: y.reshape(r, -1)
    else:
        rows, cols = col_view
        nrow = tm // cols
        tps = rows // nrow
        y_spec = lambda cb=0: pl.BlockSpec((cols, nrow, 256), lambda i: (i // tps, i % tps, cb))
        view = lambda y: y.reshape(-1, rows, y.shape[-1])
    ya, pa = mix_a
    yd, pd = mix_d
    return pl.pallas_call(
        functools.partial(_merge_kernel, d=d),
        grid=(r // tm,),
        in_specs=[row(d), pl.BlockSpec((1, 1, mod.shape[-1]), lambda i: (i // blocks_per_mod, 0, 0)),
                  pl.BlockSpec((1, d), c2), y_spec(), y_spec(3), y_spec(), y_spec(), y_spec(), y_spec(),
                  y_spec(), y_spec(4),
                  pl.BlockSpec((1, 256), c2), pl.BlockSpec((1, 256), c2), pl.BlockSpec((1, 256), c2),
                  pl.BlockSpec(wg.shape, c3), pl.BlockSpec(wb.shape, c3), pl.BlockSpec(wo.shape, c2)],
        out_specs=row(d),
        out_shape=jax.ShapeDtypeStruct((r, d), F32),
        compiler_params=_cparams(1),
        name="merge",
    )(x2d, mod, nw, view(ya), view(pa), view(x0), view(u), view(yconv), view(yc),
      view(yd), view(pd), nwa, nwd, hyb, wg, wb, wo)


def _ffn_kernel(x_ref, mod_ref, nw_ref, wu_ref, wd_ref, fnw_ref, o_ref, *, d, hid_chunk, final_norm):
    x = x_ref[...]
    h = _norm_mod(x, nw_ref[...], mod_ref[0, :, 3 * d:4 * d], mod_ref[0, :, 4 * d:5 * d]).astype(BF16)
    hidden = wd_ref.shape[0]
    acc = None
    for c in range(hidden // hid_chunk):
        cs = slice(c * hid_chunk, (c + 1) * hid_chunk)
        a = jnp.dot(h, wu_ref[:, cs], preferred_element_type=F32)
        g = jnp.dot(h, wu_ref[:, hidden + c * hid_chunk:hidden + (c + 1) * hid_chunk], preferred_element_type=F32)
        term = jnp.dot((_silu(a) * g).astype(BF16), wd_ref[cs, :], preferred_element_type=F32)
        acc = term if acc is None else acc + term
    y = x + mod_ref[0, :, 5 * d:6 * d] * acc
    if final_norm:
        y = y * lax.rsqrt(jnp.mean(y * y, axis=-1, keepdims=True) + NORM_EPS) * fnw_ref[...]
    o_ref[...] = y


def _ffn_call(x2d, mod, nw, wu, wd, fnw, rows_per_mod, tm, final_norm):
    r, d = x2d.shape
    blocks_per_mod = rows_per_mod // tm
    c2 = lambda i: (0, 0)
    return pl.pallas_call(
        functools.partial(_ffn_kernel, d=d, hid_chunk=256, final_norm=final_norm),
        grid=(r // tm,),
        in_specs=[pl.BlockSpec((tm, d), lambda i: (i, 0)),
                  pl.BlockSpec((1, 1, mod.shape[-1]), lambda i: (i // blocks_per_mod, 0, 0)),
                  pl.BlockSpec((1, d), c2),
                  pl.BlockSpec(wu.shape, c2), pl.BlockSpec(wd.shape, c2),
                  pl.BlockSpec((1, d), c2)],
        out_specs=pl.BlockSpec((tm, d), lambda i: (i, 0)),
        out_shape=jax.ShapeDtypeStruct((r, d), F32),
        compiler_params=_cparams(1),
        name="ffn",
    )(x2d, mod, nw, wu, wd, fnw)


def _row_tile(rows):
    return 512 if rows % 512 == 0 else 256


def kernel(x, c, ctx, c_ctx, ada_w, ada_b, norm1_w, norm2_w, w_in, b_in, mlstm_norm_w, hy_short_w, hy_short_b, hy_w1, hy_b1, hy_freq1, hy_w2, hy_b2, hy_freq2, hy_w3, hy_bias, gm_norm_w, gm_norm_b, gm_ws, gm_bs, hg_lb_logits, hg_norm_w, w_gate, w_branch, w_out, w_ffn_in, w_ffn_out, final_norm_w):
    bsz, n, d = x.shape
    nc = ctx.shape[1]
    depth = ada_w.shape[0]
    rows = n // GRID_W
    mw = d // N_BRANCH
    assert mw == 256 and mw == HEADS * HEAD_DIM

    mb = -(-(bsz + 1) // 8) * 8
    cvec = jnp.concatenate([c, c_ctx[None], jnp.zeros((mb - bsz - 1, d), F32)], axis=0)
    mods = _ada_call(cvec, ada_w, ada_b)

    s_a, s_g, s_b, s_c = 4 * mw, 4 * mw + 4 * HEADS, 7 * mw + 4 * HEADS, 9 * mw + 4 * HEADS
    gate_pad = LANE - 2 * HEADS

    xc = ctx
    for l in range(depth):
        need_ctx = l < depth - 1
        col_major = l % 2 == 1
        col_view = (rows, GRID_W) if col_major else None
        mod_lat = mods[l, :bsz].reshape(bsz, 1, 6 * d)
        mod_ctx = mods[l, bsz:bsz + 1].reshape(1, 1, 6 * d)
        wl, bl = w_in[l].astype(BF16), b_in[l]
        wgt, bgt = wl[:, s_a:s_g].reshape(d, 4, HEADS), bl[s_a:s_g].reshape(4, HEADS)
        zw, zb = jnp.zeros((d, gate_pad), BF16), jnp.zeros((gate_pad,), F32)
        w_segs = [jnp.concatenate([wl[:, :s_a], wgt[:, 0], wgt[:, 2], zw, wgt[:, 1], wgt[:, 3], zw], axis=1),
                  wl[:, s_g:s_b], wl[:, s_b:s_c], wl[:, s_c:]]
        b_segs = [jnp.concatenate([bl[:s_a], bgt[0], bgt[2], zb, bgt[1], bgt[3], zb]),
                  bl[s_g:s_b], bl[s_b:s_c], bl[s_c:]]
        b_segs = [b.reshape(1, -1) for b in b_segs]
        nw1 = norm1_w[l].reshape(1, d)

        x2d = x.reshape(bsz * n, d)
        xc2d = xc.reshape(bsz * nc, d)
        hy_args = (hy_w1[l], hy_b1[l].reshape(1, -1), hy_freq1[l].reshape(1, -1), hy_w2[l],
                   hy_b2[l].reshape(1, -1), hy_freq2[l].reshape(1, -1), hy_w3[l])
        front = (hy_short_w[l], hy_short_b[l].reshape(1, -1), gm_ws[l].astype(BF16),
                 jnp.repeat(gm_bs[l].T, mw // C_GROUPS, axis=1), gm_norm_w[l].reshape(1, mw), gm_norm_b[l].reshape(1, mw))
        pa, x0, u, ycm, pd = _in_full_call(x2d, mod_lat, nw1, w_segs, b_segs, *front, n, n, _row_tile(n), col_view)
        if need_ctx:
            ca, x0c, uc, ycm_c, cd = _in_full_call(xc2d, mod_ctx, nw1, w_segs, b_segs, *front, bsz * nc, nc,
                                                   _row_tile(bsz * nc))
        else:
            ca, cd = _in_call(xc2d, mod_ctx, nw1, [w_segs[0], w_segs[3]], [b_segs[0], b_segs[3]],
                              bsz * nc, _row_tile(bsz * nc))

        pa3, pd3 = pa.reshape(bsz, n, -1), pd.reshape(bsz, n, -1)
        ca3, cd3 = ca.reshape(bsz, nc, -1), cd.reshape(bsz, nc, -1)
        yac, ya = _mlstm_call(ca3, pa3, need_ctx)
        ydc, yd = _hgrn_call(cd3, pd3, hg_lb_logits, l, need_ctx)
        nwa, nwd = mlstm_norm_w[l].reshape(1, mw), hg_norm_w[l].reshape(1, mw)

        hyb = hy_bias[l].reshape(1, mw)
        wg, wb, wo = w_gate[l].astype(BF16), w_branch[l].astype(BF16), w_out[l].astype(BF16)
        wfu = w_ffn_in[l].astype(BF16)
        wfd = w_ffn_out[l].astype(BF16)
        nw2 = norm2_w[l].reshape(1, d)
        fnw = final_norm_w.reshape(1, d)

        yconv = _hyena_conv(u.reshape(bsz, n, mw), _filter_call(n, *hy_args))
        x2d = _merge_call(x2d, mod_lat, nw1, (ya, pa), (yd, pd), x0, u, yconv, ycm, nwa, nwd,
                          hyb, wg, wb, wo, n, _row_tile(n), col_view)
        x2d = _ffn_call(x2d, mod_lat, nw2, wfu, wfd, fnw, n, _row_tile(n), l == depth - 1)
        x = x2d.reshape(bsz, n, d)

        if need_ctx:
            tmc = _row_tile(bsz * nc)
            yconv_c = _hyena_conv(uc.reshape(bsz, nc, mw), _filter_call(nc, *hy_args))
            xc2d = _merge_call(xc2d, mod_ctx, nw1, (yac, ca), (ydc, cd), x0c, uc, yconv_c, ycm_c,
                               nwa, nwd, hyb, wg, wb, wo, bsz * nc, tmc)
            xc2d = _ffn_call(xc2d, mod_ctx, nw2, wfu, wfd, fnw, bsz * nc, tmc, False)
            xc = xc2d.reshape(bsz, nc, d)
    return x
```
